```python
import math
import jax
import jax.numpy as jnp
from jax import lax
import numpy as np

D_MODEL = 1024
BATCH = 4
SEQ = 4096
DEPTH = 4

CHUNK = 64
EPS = 1e-6
GLA_HEADS = 4
GLA_DK = D_MODEL // 8
GLA_DV = D_MODEL // 4
GLA_RANK = 16
GLA_TAU = 16.0
MLSTM_HEADS = 4
MLSTM_DK = D_MODEL // 8
MLSTM_DV = D_MODEL // 4
MLSTM_CONV = 4
SSD_HEAD_DIM = 64
SSD_INNER = D_MODEL
SSD_HEADS = SSD_INNER // SSD_HEAD_DIM
SSD_GROUPS = 2
SSD_STATE = 128
SSD_CONV = 4
D_FF = 2816
FFN_CONV = 3
N_BRANCH = 3

GLA_QK = GLA_HEADS * GLA_DK
GLA_V = GLA_HEADS * GLA_DV
MLSTM_QK = MLSTM_HEADS * MLSTM_DK
MLSTM_V = MLSTM_HEADS * MLSTM_DV
SSD_BC = SSD_GROUPS * SSD_STATE
SSD_CONV_DIM = SSD_INNER + 2 * SSD_BC
IN_SIZES = (GLA_QK, GLA_QK, GLA_V, GLA_RANK, GLA_V, 2 * MLSTM_QK, MLSTM_V, MLSTM_HEADS, MLSTM_HEADS, MLSTM_V, SSD_INNER, SSD_CONV_DIM, SSD_HEADS, N_BRANCH * D_MODEL)
D_IN = sum(IN_SIZES)
SPLIT_POINTS = tuple(int(s) for s in np.cumsum(IN_SIZES)[:-1])

kernel_name = 'hybrid_gla_mlstm_ssd_convffn'


def rmsnorm(x, g):
    xf = x.astype(jnp.float32)
    y = xf * lax.rsqrt(jnp.mean(xf * xf, axis=-1, keepdims=True) + EPS)
    return (y * g).astype(x.dtype)


def group_rmsnorm(x, g, n_groups):
    bsz, t, w = x.shape
    xf = x.astype(jnp.float32).reshape(bsz, t, n_groups, w // n_groups)
    y = xf * lax.rsqrt(jnp.mean(xf * xf, axis=-1, keepdims=True) + EPS)
    return (y.reshape(bsz, t, w) * g).astype(x.dtype)


def causal_dwconv(x, w, b):
    k = w.shape[0]
    y = lax.conv_general_dilated(x, w[:, None, :], window_strides=(1,), padding=[(k - 1, 0)],
                                 dimension_numbers=('NWC', 'WIO', 'NWC'), feature_group_count=x.shape[-1])
    return y + b


def heads(x, n):
    return x.reshape(x.shape[:-1] + (n, -1))


def to_chunks(x):
    bsz, t = x.shape[:2]
    x = x.reshape((bsz, t // CHUNK, CHUNK) + x.shape[2:])
    return jnp.moveaxis(x, 3, 1)


def from_chunks(y):
    y = jnp.moveaxis(y, 1, 3)
    return y.reshape((y.shape[0], y.shape[1] * y.shape[2]) + y.shape[3:])


def causal_mask():
    return jnp.tril(jnp.ones((CHUNK, CHUNK), dtype=bool))


def chunk_scan(d_state, decay):
    def step(s, inp):
        d_n, dec_n = inp
        return dec_n * s + d_n, s
    s0 = jnp.zeros_like(d_state[:, :, 0])
    _, s_prev = lax.scan(step, s0, (jnp.moveaxis(d_state, 2, 0), jnp.moveaxis(decay, 2, 0)))
    return jnp.moveaxis(s_prev, 0, 2)


def gla(q, k, v, log_a):
    out_dtype = q.dtype
    f32 = jnp.float32
    q = to_chunks(q.astype(f32)) * (GLA_DK ** -0.5)
    k = to_chunks(k.astype(f32))
    v = to_chunks(v.astype(f32))
    b = jnp.cumsum(to_chunks(log_a), axis=3)
    b_last = b[:, :, :, -1:, :]
    q_dec = q * jnp.exp(b)
    scores = jnp.einsum('bhncd,bhnsd->bhncs', q_dec, k * jnp.exp(-b))
    scores = jnp.where(causal_mask(), scores, 0.0)
    o = jnp.einsum('bhncs,bhnsv->bhncv', scores, v)
    d_state = jnp.einsum('bhncd,bhncv->bhndv', k * jnp.exp(b_last - b), v)
    s_prev = chunk_scan(d_state, jnp.exp(jnp.swapaxes(b_last, -1, -2)))
    o = o + jnp.einsum('bhncd,bhndv->bhncv', q_dec, s_prev)
    return from_chunks(o).astype(out_dtype)


def mlstm(q, k, v, i_pre, f_pre):
    out_dtype = q.dtype
    f32 = jnp.float32
    q = to_chunks(q.astype(f32))
    k = to_chunks(k.astype(f32)) * (MLSTM_DK ** -0.5)
    v = to_chunks(v.astype(f32))
    ig = to_chunks(i_pre)
    b = jnp.cumsum(to_chunks(jax.nn.log_sigmoid(f_pre)), axis=-1)
    b_last = b[..., -1]
    a = b_last[..., None] - b + ig
    m_loc = jnp.max(a, axis=-1)
    w = jnp.exp(a - m_loc[..., None])
    d_c = jnp.einsum('bhnc,bhncd,bhncv->bhndv', w, k, v)
    d_n = jnp.einsum('bhnc,bhncd->bhnd', w, k)

    def step(carry, inp):
        c, n, m = carry
        dc_i, dn_i, mloc_i, bl_i = inp
        m_new = jnp.maximum(bl_i + m, mloc_i)
        s_old = jnp.exp(bl_i + m - m_new)
        s_new = jnp.exp(mloc_i - m_new)
        c_new = s_old[..., None, None] * c + s_new[..., None, None] * dc_i
        n_new = s_old[..., None] * n + s_new[..., None] * dn_i
        return (c_new, n_new, m_new), (c, n, m)

    init = (jnp.zeros_like(d_c[:, :, 0]), jnp.zeros_like(d_n[:, :, 0]), jnp.zeros_like(m_loc[:, :, 0]))
    xs = (jnp.moveaxis(d_c, 2, 0), jnp.moveaxis(d_n, 2, 0), jnp.moveaxis(m_loc, 2, 0), jnp.moveaxis(b_last, 2, 0))
    _, (c_prev, n_prev, m_prev) = lax.scan(step, init, xs)
    c_prev = jnp.moveaxis(c_prev, 0, 2)
    n_prev = jnp.moveaxis(n_prev, 0, 2)
    m_prev = jnp.moveaxis(m_prev, 0, 2)

    log_d = jnp.where(causal_mask(), b[..., :, None] - b[..., None, :] + ig[..., None, :], -jnp.inf)
    m_inter = b + m_prev[..., None]
    m_t = jnp.maximum(m_inter, jnp.max(log_d, axis=-1))
    wts = jnp.einsum('bhncd,bhnsd->bhncs', q, k) * jnp.exp(log_d - m_t[..., None])
    s_inter = jnp.exp(m_inter - m_t)
    num = jnp.einsum('bhncs,bhnsv->bhncv', wts, v) + s_inter[..., None] * jnp.einsum('bhncd,bhndv->bhncv', q, c_prev)
    den = jnp.sum(wts, axis=-1) + s_inter * jnp.einsum('bhncd,bhnd->bhnc', q, n_prev)
    h = num / jnp.maximum(jnp.abs(den), jnp.exp(-m_t))[..., None]
    return from_chunks(h).astype(out_dtype)


def ssd(x, dt, a_neg, b_in, c_in):
    out_dtype = x.dtype
    f32 = jnp.float32
    bsz = x.shape[0]
    j = SSD_HEADS // SSD_GROUPS
    xc = to_chunks(x.astype(f32))
    nc = xc.shape[2]
    xc = xc.reshape(bsz, SSD_GROUPS, j, nc, CHUNK, SSD_HEAD_DIM)
    bc = to_chunks(b_in.astype(f32))
    cc = to_chunks(c_in.astype(f32))
    dtc = to_chunks(dt).reshape(bsz, SSD_GROUPS, j, nc, CHUNK)
    b = jnp.cumsum(dtc * a_neg.reshape(SSD_GROUPS, j)[None, :, :, None, None], axis=-1)
    b_last = b[..., -1]
    cb = jnp.einsum('bgncz,bgnsz->bgncs', cc, bc)
    decay = jnp.exp(jnp.where(causal_mask(), b[..., :, None] - b[..., None, :], -jnp.inf))
    mix = cb[:, :, None] * decay * dtc[..., None, :]
    y = jnp.einsum('bgjncs,bgjnsp->bgjncp', mix, xc)
    w = jnp.exp(b_last[..., None] - b) * dtc
    d_state = jnp.einsum('bgjnc,bgncz,bgjncp->bgjnzp', w, bc, xc)
    d_state = d_state.reshape(bsz, SSD_HEADS, nc, SSD_STATE, SSD_HEAD_DIM)
    s_prev = chunk_scan(d_state, jnp.exp(b_last).reshape(bsz, SSD_HEADS, nc, 1, 1))
    s_prev = s_prev.reshape(bsz, SSD_GROUPS, j, nc, SSD_STATE, SSD_HEAD_DIM)
    y = y + jnp.einsum('bgncz,bgjnzp->bgjncp', cc, s_prev) * jnp.exp(b)[..., None]
    y = y.reshape(bsz, SSD_HEADS, nc, CHUNK, SSD_HEAD_DIM)
    return from_chunks(y).astype(out_dtype)


def mixer_layer(u, w_in, gla_wa, gla_ba, gla_norm, mlstm_conv_w, mlstm_conv_b, mlstm_bi, mlstm_bf, mlstm_norm,
                ssd_conv_w, ssd_conv_b, ssd_dt_bias, ssd_a_log, ssd_d, ssd_norm, gate_b, w_branch, w_out):
    f32 = jnp.float32
    bsz, t, _ = u.shape
    proj = u @ w_in
    (gq, gk, gv, ga, gg, mqk, mv, mi, mf, mo, sz, sxbc, sdt, gates) = jnp.split(proj, SPLIT_POINTS, axis=-1)

    log_a = jax.nn.log_sigmoid((ga @ gla_wa + gla_ba).astype(f32)) / GLA_TAU
    o_gla = gla(heads(gq, GLA_HEADS), heads(gk, GLA_HEADS), heads(gv, GLA_HEADS), heads(log_a, GLA_HEADS))
    y_gla = group_rmsnorm(o_gla.reshape(bsz, t, GLA_V), gla_norm, GLA_HEADS) * jax.nn.silu(gg)

    mqk = jax.nn.silu(causal_dwconv(mqk, mlstm_conv_w, mlstm_conv_b))
    mq, mk = jnp.split(mqk, 2, axis=-1)
    o_m = mlstm(heads(mq, MLSTM_HEADS), heads(mk, MLSTM_HEADS), heads(mv, MLSTM_HEADS),
                (mi + mlstm_bi).astype(f32), (mf + mlstm_bf).astype(f32))
    y_m = group_rmsnorm(o_m.reshape(bsz, t, MLSTM_V), mlstm_norm, MLSTM_HEADS) * jax.nn.sigmoid(mo)

    xbc = jax.nn.silu(causal_dwconv(sxbc, ssd_conv_w, ssd_conv_b))
    sx, sb, sc = jnp.split(xbc, [SSD_INNER, SSD_INNER + SSD_BC], axis=-1)
    dt = jax.nn.softplus((sdt + ssd_dt_bias).astype(f32))
    a_neg = -jnp.exp(ssd_a_log.astype(f32))
    xh = heads(sx, SSD_HEADS)
    y = ssd(xh, dt, a_neg, heads(sb, SSD_GROUPS), heads(sc, SSD_GROUPS))
    y = (y + ssd_d[:, None] * xh).reshape(bsz, t, SSD_INNER) * jax.nn.silu(sz)
    y_s = group_rmsnorm(y, ssd_norm, SSD_GROUPS)

    ys = jnp.stack([y_gla, y_m, y_s], axis=2)
    z = jnp.einsum('btkw,kwd->btkd', ys, w_branch)
    g = jax.nn.sigmoid(gates + gate_b).reshape(bsz, t, N_BRANCH, D_MODEL)
    return jnp.sum(g * z, axis=2) @ w_out


def conv_ffn(u, w_up, conv_w, conv_b, w_down):
    a = causal_dwconv(u @ w_up, conv_w, conv_b)
    gate, val = jnp.split(a, 2, axis=-1)
    return (jax.nn.silu(gate) * val) @ w_down


def setup_inputs(seed: int = 0) -> dict:
    key = jax.random.key(seed)
    ks = jax.random.split(key, 32)
    f32 = jnp.float32
    L = DEPTH
    res_scale = (2.0 * L) ** -0.5

    def nrm(k, shape, scale):
        return jax.random.normal(k, shape, f32) * scale

    dt0 = jnp.exp(jax.random.uniform(ks[15], (L, SSD_HEADS), f32, math.log(1e-3), math.log(1e-1)))
    return {
        'x': nrm(ks[0], (BATCH, SEQ, D_MODEL), 1.0),
        'norm_mix': 1.0 + nrm(ks[1], (L, D_MODEL), 0.02),
        'w_in': nrm(ks[2], (L, D_MODEL, D_IN), D_MODEL ** -0.5),
        'gla_wa': nrm(ks[3], (L, GLA_RANK, GLA_QK), GLA_RANK ** -0.5),
        'gla_ba': nrm(ks[4], (L, GLA_QK), 0.1),
        'gla_norm': 1.0 + nrm(ks[5], (L, GLA_V), 0.02),
        'mlstm_conv_w': nrm(ks[6], (L, MLSTM_CONV, 2 * MLSTM_QK), MLSTM_CONV ** -0.5),
        'mlstm_conv_b': nrm(ks[7], (L, 2 * MLSTM_QK), 0.02),
        'mlstm_bi': nrm(ks[8], (L, MLSTM_HEADS), 0.1),
        'mlstm_bf': jnp.linspace(3.0, 6.0, MLSTM_HEADS, dtype=f32)[None, :] + nrm(ks[9], (L, MLSTM_HEADS), 0.1),
        'mlstm_norm': 1.0 + nrm(ks[10], (L, MLSTM_V), 0.02),
        'ssd_conv_w': nrm(ks[11], (L, SSD_CONV, SSD_CONV_DIM), SSD_CONV ** -0.5),
        'ssd_conv_b': nrm(ks[12], (L, SSD_CONV_DIM), 0.02),
        'ssd_dt_bias': dt0 + jnp.log(-jnp.expm1(-dt0)),
        'ssd_a_log': jnp.log(jax.random.uniform(ks[13], (L, SSD_HEADS), f32, 1.0, 16.0)),
        'ssd_d': 1.0 + nrm(ks[14], (L, SSD_HEADS), 0.1),
        'ssd_norm': 1.0 + nrm(ks[16], (L, SSD_INNER), 0.02),
        'gate_b': nrm(ks[17], (L, N_BRANCH * D_MODEL), 0.1),
        'w_branch': nrm(ks[18], (L, N_BRANCH, D_MODEL, D_MODEL), D_MODEL ** -0.5),
        'w_out': nrm(ks[19], (L, D_MODEL, D_MODEL), D_MODEL ** -0.5 * res_scale),
        'norm_ffn': 1.0 + nrm(ks[20], (L, D_MODEL), 0.02),
        'w_up': nrm(ks[21], (L, D_MODEL, 2 * D_FF), D_MODEL ** -0.5),
        'ffn_conv_w': nrm(ks[22], (L, FFN_CONV, 2 * D_FF), FFN_CONV ** -0.5),
        'ffn_conv_b': nrm(ks[23], (L, 2 * D_FF), 0.02),
        'w_down': nrm(ks[24], (L, D_FF, D_MODEL), D_FF ** -0.5 * res_scale),
        'norm_final': 1.0 + nrm(ks[25], (D_MODEL,), 0.02),
    }


def reference(x, norm_mix, w_in, gla_wa, gla_ba, gla_norm, mlstm_conv_w, mlstm_conv_b, mlstm_bi, mlstm_bf,
              mlstm_norm, ssd_conv_w, ssd_conv_b, ssd_dt_bias, ssd_a_log, ssd_d, ssd_norm, gate_b, w_branch,
              w_out, norm_ffn, w_up, ffn_conv_w, ffn_conv_b, w_down, norm_final):
    h = x
    for l in range(DEPTH):
        h = h + mixer_layer(rmsnorm(h, norm_mix[l]), w_in[l], gla_wa[l], gla_ba[l], gla_norm[l],
                            mlstm_conv_w[l], mlstm_conv_b[l], mlstm_bi[l], mlstm_bf[l], mlstm_norm[l],
                            ssd_conv_w[l], ssd_conv_b[l], ssd_dt_bias[l], ssd_a_log[l], ssd_d[l], ssd_norm[l],
                            gate_b[l], w_branch[l], w_out[l])
        h = h + conv_ffn(rmsnorm(h, norm_ffn[l]), w_up[l], ffn_conv_w[l], ffn_conv_b[l], w_down[l])
    return rmsnorm(h, norm_final)
```

```python
import functools

import jax
import jax.numpy as jnp
from jax import lax
from jax.experimental import pallas as pl
from jax.experimental.pallas import tpu as pltpu

F32 = jnp.float32
BF16 = jnp.bfloat16
HIGHEST = lax.Precision.HIGHEST

EPS = 1e-6
CHUNK = 64
GLA_HEADS = 4
GLA_TAU = 16.0
MLSTM_HEADS = 4
SSD_HEADS = 16
SSD_GROUPS = 2
SSD_STATE = 128
SSD_HEAD_DIM = 64
FFN_CONV = 3
CONV_K = 4

SM_GA = 0
SM_MI = 16
SM_MF = 20
SM_DT = 24
SM_W = 128

VMEM_LIMIT = 52 * 1024 * 1024


def _dot(a, b):
    return jnp.dot(a.astype(BF16), b.astype(BF16), preferred_element_type=F32)


def _dot_nt(a, b):
    return lax.dot_general(a.astype(BF16), b.astype(BF16), (((1,), (1,)), ((), ())),
                           preferred_element_type=F32)


def _dot_tn(a, b):
    return jnp.dot(a.T.astype(BF16), b.astype(BF16), preferred_element_type=F32)


def _dot_f32(a, b):
    return jnp.dot(a, b, precision=HIGHEST, preferred_element_type=F32)


def _sigmoid(x):
    return 1.0 / (1.0 + jnp.exp(-x))


def _silu(x):
    return x * _sigmoid(x)


def _softplus(x):
    return jnp.maximum(x, 0.0) + jnp.log1p(jnp.exp(-jnp.abs(x)))


def _log_sigmoid(x):
    return jnp.minimum(x, 0.0) - jnp.log1p(jnp.exp(-jnp.abs(x)))


def _causal():
    row = lax.broadcasted_iota(jnp.int32, (CHUNK, CHUNK), 0)
    col = lax.broadcasted_iota(jnp.int32, (CHUNK, CHUNK), 1)
    return col <= row


def _conv_rows(xpad_ref, cw, cb, r0, rows, cols, taps):
    acc = cb
    for k in range(taps):
        shift = taps - 1 - k
        acc = acc + cw[k:k + 1, :] * xpad_ref[8 + r0 - shift:8 + r0 - shift + rows, cols]
    return acc


def _norm_matmul_kernel(x_ref, g_ref, w_ref, o_ref, xn_ref):
    @pl.when(pl.program_id(1) == 0)
    def _():
        x = x_ref[...]
        ms = jnp.mean(x * x, axis=-1, keepdims=True)
        xn_ref[...] = (x * lax.rsqrt(ms + EPS) * g_ref[...]).astype(BF16)

    o_ref[...] = jnp.dot(xn_ref[...], w_ref[...], preferred_element_type=F32).astype(o_ref.dtype)


def _norm_matmul(x, g, w, tm, tn, out_dtype):
    m, d = x.shape
    n = w.shape[1]
    return pl.pallas_call(
        _norm_matmul_kernel,
        grid=(m // tm, n // tn),
        in_specs=[
            pl.BlockSpec((tm, d), lambda i, j: (i, 0)),
            pl.BlockSpec((1, d), lambda i, j: (0, 0)),
            pl.BlockSpec((d, tn), lambda i, j: (0, j)),
        ],
        out_specs=pl.BlockSpec((tm, tn), lambda i, j: (i, j)),
        out_shape=jax.ShapeDtypeStruct((m, n), out_dtype),
        scratch_shapes=[pltpu.VMEM((tm, d), BF16)],
        compiler_params=pltpu.CompilerParams(
            dimension_semantics=("parallel", "arbitrary"), vmem_limit_bytes=VMEM_LIMIT),
        name="norm_matmul",
    )(x, g, w)


def _gla_kernel(q_ref, k_ref, v_ref, gg_ref, sm_ref, wa_ref, ba_ref, nw_ref, y_ref, st_ref, *, tb):
    dk = q_ref.shape[2] // GLA_HEADS
    dv = v_ref.shape[2] // GLA_HEADS

    @pl.when(pl.program_id(1) == 0)
    def _():
        st_ref[...] = jnp.zeros_like(st_ref)

    causal = _causal()
    tril = causal.astype(F32)
    scale = dk ** -0.5
    for c in range(tb // CHUNK):
        r = slice(c * CHUNK, (c + 1) * CHUNK)
        la = _log_sigmoid(_dot_f32(sm_ref[0, r, :], wa_ref[...]) + ba_ref[...]) * (1.0 / GLA_TAU)
        bcum = _dot_f32(tril, la)
        for h in range(GLA_HEADS):
            kc = slice(h * dk, (h + 1) * dk)
            vc = slice(h * dv, (h + 1) * dv)
            b = bcum[:, kc]
            bl = b[CHUNK - 1:CHUNK, :]
            q = q_ref[0, r, kc] * scale
            k = k_ref[0, r, kc]
            v = v_ref[0, r, vc]
            qd = q * jnp.exp(b)
            ki = k * jnp.exp(-b)
            kd = k * jnp.exp(bl - b)
            sc = jnp.where(causal, _dot_nt(qd, ki), 0.0)
            st = st_ref[h]
            o = _dot(sc, v) + _dot_nt(qd, st)
            st_ref[h] = st * jnp.exp(bl) + _dot_tn(v, kd)
            ms = jnp.mean(o * o, axis=-1, keepdims=True)
            y = (o * lax.rsqrt(ms + EPS) * nw_ref[:, vc]) * _silu(gg_ref[0, r, vc])
            y_ref[0, r, vc] = y.astype(y_ref.dtype)


def _gla(proj, wa_pad, ba, nw, tb):
    bsz, t, _ = proj.shape
    qk_w = ba.shape[1]
    v_w = nw.shape[1]
    dk = qk_w // GLA_HEADS
    dv = v_w // GLA_HEADS
    sm_idx = _OFF_SMALL // SM_W
    const = lambda b, i: (0, 0)
    return pl.pallas_call(
        functools.partial(_gla_kernel, tb=tb),
        grid=(bsz, t // tb),
        in_specs=[
            pl.BlockSpec((1, tb, qk_w), lambda b, i: (b, i, 0)),
            pl.BlockSpec((1, tb, qk_w), lambda b, i: (b, i, 1)),
            pl.BlockSpec((1, tb, v_w), lambda b, i: (b, i, 1)),
            pl.BlockSpec((1, tb, v_w), lambda b, i: (b, i, 2)),
            pl.BlockSpec((1, tb, SM_W), lambda b, i: (b, i, sm_idx)),
            pl.BlockSpec(wa_pad.shape, const),
            pl.BlockSpec(ba.shape, const),
            pl.BlockSpec(nw.shape, const),
        ],
        out_specs=pl.BlockSpec((1, tb, v_w), lambda b, i: (b, i, 0)),
        out_shape=jax.ShapeDtypeStruct((bsz, t, v_w), BF16),
        scratch_shapes=[pltpu.VMEM((GLA_HEADS, dv, dk), F32)],
        compiler_params=pltpu.CompilerParams(
            dimension_semantics=("parallel", "arbitrary"), vmem_limit_bytes=VMEM_LIMIT),
        name="gla",
    )(proj, proj, proj, proj, proj, wa_pad, ba, nw)


def _mlstm_kernel(qk_ref, v_ref, og_ref, sm_ref, cw_ref, cb_ref, brow_ref, bcol_ref, nw_ref, y_ref,
                  xpad_ref, c_ref, n_ref, m_ref, *, tb):
    qk_w = qk_ref.shape[2] // 2
    dk = qk_w // MLSTM_HEADS
    dv = v_ref.shape[2] // MLSTM_HEADS

    @pl.when(pl.program_id(1) == 0)
    def _():
        xpad_ref[0:8, :] = jnp.zeros((8, xpad_ref.shape[1]), F32)
        c_ref[...] = jnp.zeros_like(c_ref)
        n_ref[...] = jnp.zeros_like(n_ref)
        m_ref[...] = jnp.zeros_like(m_ref)

    xpad_ref[8:8 + tb, :] = qk_ref[0]
    causal = _causal()
    tril = causal.astype(F32)
    triu = jnp.logical_not(causal).astype(F32) + (
        lax.broadcasted_iota(jnp.int32, (CHUNK, CHUNK), 0) == lax.broadcasted_iota(jnp.int32, (CHUNK, CHUNK), 1)
    ).astype(F32)
    kscale = dk ** -0.5
    neg_inf = -jnp.inf
    for c in range(tb // CHUNK):
        r0 = c * CHUNK
        r = slice(r0, r0 + CHUNK)
        sm = sm_ref[0, r, :]
        pre_col = sm + brow_ref[...]
        pre_row = sm.T + bcol_ref[...]
        b_col = _dot_f32(tril, _log_sigmoid(pre_col))
        b_row = _dot_f32(_log_sigmoid(pre_row), triu)
        for h in range(MLSTM_HEADS):
            qc = slice(h * dk, (h + 1) * dk)
            kc = slice(qk_w + h * dk, qk_w + (h + 1) * dk)
            vc = slice(h * dv, (h + 1) * dv)
            q = _silu(_conv_rows(xpad_ref, cw_ref[:, qc], cb_ref[:, qc], r0, CHUNK, qc, CONV_K))
            k = _silu(_conv_rows(xpad_ref, cw_ref[:, kc], cb_ref[:, kc], r0, CHUNK, kc, CONV_K)) * kscale
            v = v_ref[0, r, vc]
            bc = b_col[:, SM_MF + h:SM_MF + h + 1]
            br = b_row[SM_MF + h:SM_MF + h + 1, :]
            ic = pre_col[:, SM_MI + h:SM_MI + h + 1]
            ir = pre_row[SM_MI + h:SM_MI + h + 1, :]
            bl = bc[CHUNK - 1:CHUNK, :]
            m_prev = m_ref[h][0:1, 0:1]
            n_prev = n_ref[h][0:1, :]
            c_prev = c_ref[h]

            log_d = jnp.where(causal, bc - br + ir, neg_inf)
            m_inter = bc + m_prev
            m_t = jnp.maximum(m_inter, jnp.max(log_d, axis=-1, keepdims=True))
            wts = _dot_nt(q, k) * jnp.exp(log_d - m_t)
            s_inter = jnp.exp(m_inter - m_t)
            num = _dot(wts, v) + s_inter * _dot(q, c_prev)
            den = jnp.sum(wts, axis=-1, keepdims=True) + s_inter * jnp.sum(q * n_prev, axis=-1, keepdims=True)
            hout = num / jnp.maximum(jnp.abs(den), jnp.exp(-m_t))

            a = bl - bc + ic
            m_loc = jnp.max(a, axis=0, keepdims=True)
            w = jnp.exp(a - m_loc)
            wk = w * k
            m_new = jnp.maximum(bl + m_prev, m_loc)
            s_old = jnp.exp(bl + m_prev - m_new)
            s_new = jnp.exp(m_loc - m_new)
            c_ref[h] = s_old * c_prev + s_new * _dot_tn(wk, v)
            n_new = s_old * n_prev + s_new * jnp.sum(wk, axis=0, keepdims=True)
            n_ref[h] = jnp.broadcast_to(n_new, n_ref.shape[1:])
            m_ref[h] = jnp.broadcast_to(m_new, m_ref.shape[1:])

            ms = jnp.mean(hout * hout, axis=-1, keepdims=True)
            y = (hout * lax.rsqrt(ms + EPS) * nw_ref[:, vc]) * _sigmoid(og_ref[0, r, vc])
            y_ref[0, r, vc] = y.astype(y_ref.dtype)

    xpad_ref[0:8, :] = xpad_ref[tb:tb + 8, :]


def _mlstm(proj, cw, cb, brow, bcol, nw, tb):
    bsz, t, _ = proj.shape
    qk2 = cw.shape[1]
    v_w = nw.shape[1]
    dk = qk2 // 2 // MLSTM_HEADS
    dv = v_w // MLSTM_HEADS
    sm_idx = _OFF_SMALL // SM_W
    base = _OFF_MLSTM // v_w
    const = lambda b, i: (0, 0)
    return pl.pallas_call(
        functools.partial(_mlstm_kernel, tb=tb),
        grid=(bsz, t // tb),
        in_specs=[
            pl.BlockSpec((1, tb, qk2), lambda b, i: (b, i, base)),
            pl.BlockSpec((1, tb, v_w), lambda b, i: (b, i, base + 1)),
            pl.BlockSpec((1, tb, v_w), lambda b, i: (b, i, base + 2)),
            pl.BlockSpec((1, tb, SM_W), lambda b, i: (b, i, sm_idx)),
            pl.BlockSpec(cw.shape, const),
            pl.BlockSpec(cb.shape, const),
            pl.BlockSpec(brow.shape, const),
            pl.BlockSpec(bcol.shape, const),
            pl.BlockSpec(nw.shape, const),
        ],
        out_specs=pl.BlockSpec((1, tb, v_w), lambda b, i: (b, i, 0)),
        out_shape=jax.ShapeDtypeStruct((bsz, t, v_w), BF16),
        scratch_shapes=[
            pltpu.VMEM((tb + 8, qk2), F32),
            pltpu.VMEM((MLSTM_HEADS, dk, dv), F32),
            pltpu.VMEM((MLSTM_HEADS, 8, dk), F32),
            pltpu.VMEM((MLSTM_HEADS, 8, 128), F32),
        ],
        compiler_params=pltpu.CompilerParams(
            dimension_semantics=("parallel", "arbitrary"), vmem_limit_bytes=VMEM_LIMIT),
        name="mlstm",
    )(proj, proj, proj, proj, cw, cb, brow, bcol, nw)


def _ssd_kernel(z_ref, x_ref, bc_ref, sm_ref, cwx_ref, cbx_ref, cwbc_ref, cbbc_ref, brow_ref, bcol_ref,
                alog_row_ref, alog_col_ref, dexp_ref, nw_ref, e_ref, y_ref,
                xpad_ref, bcpad_ref, s_ref, ysc_ref, *, tb):
    inner = x_ref.shape[2]
    gw = inner // SSD_GROUPS
    hpg = SSD_HEADS // SSD_GROUPS
    p = SSD_HEAD_DIM
    z = SSD_STATE

    @pl.when(pl.program_id(1) == 0)
    def _():
        xpad_ref[0:8, :] = jnp.zeros((8, xpad_ref.shape[1]), F32)
        bcpad_ref[0:8, :] = jnp.zeros((8, bcpad_ref.shape[1]), F32)
        s_ref[...] = jnp.zeros_like(s_ref)

    xpad_ref[8:8 + tb, :] = x_ref[0]
    bcpad_ref[8:8 + tb, :] = bc_ref[0]
    causal = _causal()
    tril = causal.astype(F32)
    triu = jnp.logical_not(causal).astype(F32) + (
        lax.broadcasted_iota(jnp.int32, (CHUNK, CHUNK), 0) == lax.broadcasted_iota(jnp.int32, (CHUNK, CHUNK), 1)
    ).astype(F32)
    neg_inf = -jnp.inf
    a_row = -jnp.exp(alog_row_ref[...])
    a_col = -jnp.exp(alog_col_ref[...])
    for c in range(tb // CHUNK):
        r0 = c * CHUNK
        r = slice(r0, r0 + CHUNK)
        sm = sm_ref[0, r, :]
        dt_col = _softplus(sm + brow_ref[...])
        dt_row = _softplus(sm.T + bcol_ref[...])
        b_col = _dot_f32(tril, dt_col * a_row)
        b_row = _dot_f32(dt_row * a_col, triu)
        bl_row = b_col[CHUNK - 1:CHUNK, :]
        w_col = jnp.exp(bl_row - b_col) * dt_col
        eb_col = jnp.exp(b_col)
        ebl = jnp.broadcast_to(jnp.exp(bl_row), (8, SM_W))
        stack = jnp.concatenate([w_col, eb_col, ebl], axis=0)
        hi = stack.astype(BF16)
        lo = (stack - hi.astype(F32)).astype(BF16)
        expd = jnp.dot(jnp.concatenate([hi, lo], axis=1), e_ref[...], preferred_element_type=F32)
        w_x = expd[0:CHUNK]
        eb_x = expd[CHUNK:2 * CHUNK]
        ebl_x = expd[2 * CHUNK:2 * CHUNK + 1]
        for g in range(SSD_GROUPS):
            gc = slice(g * gw, (g + 1) * gw)
            bcols = slice(g * z, (g + 1) * z)
            ccols = slice(SSD_GROUPS * z + g * z, SSD_GROUPS * z + (g + 1) * z)
            xg = _silu(_conv_rows(xpad_ref, cwx_ref[:, gc], cbx_ref[:, gc], r0, CHUNK, gc, CONV_K))
            bg = _silu(_conv_rows(bcpad_ref, cwbc_ref[:, bcols], cbbc_ref[:, bcols], r0, CHUNK, bcols, CONV_K))
            cg = _silu(_conv_rows(bcpad_ref, cwbc_ref[:, ccols], cbbc_ref[:, ccols], r0, CHUNK, ccols, CONV_K))
            cb = _dot_nt(cg, bg)
            s_prev = s_ref[g]
            inter = _dot(cg, s_prev) * eb_x[:, gc]
            s_ref[g] = s_prev * ebl_x[:, gc] + _dot_tn(bg, xg * w_x[:, gc])
            for j in range(hpg):
                h = g * hpg + j
                bc = b_col[:, SM_DT + h:SM_DT + h + 1]
                br = b_row[SM_DT + h:SM_DT + h + 1, :]
                dtr = dt_row[SM_DT + h:SM_DT + h + 1, :]
                mix = cb * jnp.exp(jnp.where(causal, bc - br, neg_inf)) * dtr
                ysc_ref[:, h * p:(h + 1) * p] = _dot(mix, xg[:, j * p:(j + 1) * p])
            y = ysc_ref[:, gc] + inter + dexp_ref[:, gc] * xg
            y = y * _silu(z_ref[0, r, gc])
            ms = jnp.mean(y * y, axis=-1, keepdims=True)
            y_ref[0, r, gc] = (y * lax.rsqrt(ms + EPS) * nw_ref[:, gc]).astype(y_ref.dtype)

    xpad_ref[0:8, :] = xpad_ref[tb:tb + 8, :]
    bcpad_ref[0:8, :] = bcpad_ref[tb:tb + 8, :]


def _ssd(proj, cwx, cbx, cwbc, cbbc, brow, bcol, alog_row, alog_col, dexp, nw, emat, tb):
    bsz, t, _ = proj.shape
    inner = nw.shape[1]
    bcw = cwbc.shape[1]
    sm_idx = _OFF_SMALL // SM_W
    const = lambda b, i: (0, 0)
    return pl.pallas_call(
        functools.partial(_ssd_kernel, tb=tb),
        grid=(bsz, t // tb),
        in_specs=[
            pl.BlockSpec((1, tb, inner), lambda b, i: (b, i, _OFF_SZ // inner)),
            pl.BlockSpec((1, tb, inner), lambda b, i: (b, i, _OFF_SX // inner)),
            pl.BlockSpec((1, tb, bcw), lambda b, i: (b, i, _OFF_SBC // bcw)),
            pl.BlockSpec((1, tb, SM_W), lambda b, i: (b, i, sm_idx)),
            pl.BlockSpec(cwx.shape, const),
            pl.BlockSpec(cbx.shape, const),
            pl.BlockSpec(cwbc.shape, const),
            pl.BlockSpec(cbbc.shape, const),
            pl.BlockSpec(brow.shape, const),
            pl.BlockSpec(bcol.shape, const),
            pl.BlockSpec(alog_row.shape, const),
            pl.BlockSpec(alog_col.shape, const),
            pl.BlockSpec(dexp.shape, const),
            pl.BlockSpec(nw.shape, const),
            pl.BlockSpec(emat.shape, const),
        ],
        out_specs=pl.BlockSpec((1, tb, inner), lambda b, i: (b, i, 0)),
        out_shape=jax.ShapeDtypeStruct((bsz, t, inner), BF16),
        scratch_shapes=[
            pltpu.VMEM((tb + 8, inner), F32),
            pltpu.VMEM((tb + 8, bcw), F32),
            pltpu.VMEM((SSD_GROUPS, SSD_STATE, inner // SSD_GROUPS), F32),
            pltpu.VMEM((CHUNK, inner), F32),
        ],
        compiler_params=pltpu.CompilerParams(
            dimension_semantics=("parallel", "arbitrary"), vmem_limit_bytes=VMEM_LIMIT),
        name="ssd",
    )(proj, proj, proj, proj, cwx, cbx, cwbc, cbbc, brow, bcol, alog_row, alog_col, dexp, nw, emat)


def _merge_kernel(ya_ref, yb_ref, yc_ref, gates_ref, gb_ref, wb_ref, wo_ref, h_ref, o_ref):
    d = h_ref.shape[1]
    acc = None
    for k, y_ref in enumerate((ya_ref, yb_ref, yc_ref)):
        zk = jnp.dot(y_ref[...], wb_ref[k], preferred_element_type=F32)
        gk = _sigmoid(gates_ref[:, k * d:(k + 1) * d] + gb_ref[:, k * d:(k + 1) * d])
        acc = gk * zk if acc is None else acc + gk * zk
    o_ref[...] = h_ref[...] + jnp.dot(acc.astype(BF16), wo_ref[...], preferred_element_type=F32)


def _merge(ya, yb, yc, proj2, gate_b, w_branch, w_out, h, tm):
    m, d = h.shape
    nb = w_branch.shape[0]
    return pl.pallas_call(
        _merge_kernel,
        grid=(m // tm,),
        in_specs=[
            pl.BlockSpec((tm, d), lambda i: (i, 0)),
            pl.BlockSpec((tm, d), lambda i: (i, 0)),
            pl.BlockSpec((tm, d), lambda i: (i, 0)),
            pl.BlockSpec((tm, nb * d), lambda i: (i, _OFF_GATES // (nb * d))),
            pl.BlockSpec((1, nb * d), lambda i: (0, 0)),
            pl.BlockSpec(w_branch.shape, lambda i: (0, 0, 0)),
            pl.BlockSpec(w_out.shape, lambda i: (0, 0)),
            pl.BlockSpec((tm, d), lambda i: (i, 0)),
        ],
        out_specs=pl.BlockSpec((tm, d), lambda i: (i, 0)),
        out_shape=jax.ShapeDtypeStruct((m, d), F32),
        compiler_params=pltpu.CompilerParams(
            dimension_semantics=("parallel",), vmem_limit_bytes=VMEM_LIMIT),
        name="merge",
    )(ya, yb, yc, proj2, gate_b, w_branch, w_out, h)


def _ffn_kernel(h_ref, g_ref, wg_ref, wv_ref, cwg_ref, cwv_ref, cbg_ref, cbv_ref, wd_ref, o_ref,
                xn_ref, gpad_ref, vpad_ref, acc_ref, *, tm):
    t = pl.program_id(1)
    f = pl.program_id(2)
    nf = pl.num_programs(2)

    @pl.when(f == 0)
    def _():
        x = h_ref[0]
        ms = jnp.mean(x * x, axis=-1, keepdims=True)
        xn_ref[...] = (x * lax.rsqrt(ms + EPS) * g_ref[...]).astype(BF16)

    @pl.when(t == 0)
    def _():
        gpad_ref[f, 0:8, :] = jnp.zeros((8, gpad_ref.shape[2]), F32)
        vpad_ref[f, 0:8, :] = jnp.zeros((8, vpad_ref.shape[2]), F32)

    xn = xn_ref[...]
    gpad_ref[f, 8:8 + tm, :] = jnp.dot(xn, wg_ref[...], preferred_element_type=F32)
    vpad_ref[f, 8:8 + tm, :] = jnp.dot(xn, wv_ref[...], preferred_element_type=F32)
    gate = cbg_ref[...]
    val = cbv_ref[...]
    for k in range(FFN_CONV):
        s = FFN_CONV - 1 - k
        gate = gate + cwg_ref[k:k + 1, :] * gpad_ref[f, 8 - s:8 - s + tm, :]
        val = val + cwv_ref[k:k + 1, :] * vpad_ref[f, 8 - s:8 - s + tm, :]
    gpad_ref[f, 0:8, :] = gpad_ref[f, tm:tm + 8, :]
    vpad_ref[f, 0:8, :] = vpad_ref[f, tm:tm + 8, :]
    act = (_silu(gate) * val).astype(BF16)
    part = jnp.dot(act, wd_ref[...], preferred_element_type=F32)

    @pl.when(f == 0)
    def _():
        acc_ref[...] = part

    @pl.when(f > 0)
    def _():
        acc_ref[...] += part

    @pl.when(f == nf - 1)
    def _():
        o_ref[0] = h_ref[0] + acc_ref[...]


def _ffn(h3, g, w_up, cw, cb, w_down, tm, tf):
    bsz, t, d = h3.shape
    dff = w_down.shape[0]
    nf = dff // tf
    return pl.pallas_call(
        functools.partial(_ffn_kernel, tm=tm),
        grid=(bsz, t // tm, nf),
        in_specs=[
            pl.BlockSpec((1, tm, d), lambda b, i, f: (b, i, 0)),
            pl.BlockSpec((1, d), lambda b, i, f: (0, 0)),
            pl.BlockSpec((d, tf), lambda b, i, f: (0, f)),
            pl.BlockSpec((d, tf), lambda b, i, f: (0, nf + f)),
            pl.BlockSpec((FFN_CONV, tf), lambda b, i, f: (0, f)),
            pl.BlockSpec((FFN_CONV, tf), lambda b, i, f: (0, nf + f)),
            pl.BlockSpec((1, tf), lambda b, i, f: (0, f)),
            pl.BlockSpec((1, tf), lambda b, i, f: (0, nf + f)),
            pl.BlockSpec((tf, d), lambda b, i, f: (f, 0)),
        ],
        out_specs=pl.BlockSpec((1, tm, d), lambda b, i, f: (b, i, 0)),
        out_shape=jax.ShapeDtypeStruct((bsz, t, d), F32),
        scratch_shapes=[
            pltpu.VMEM((tm, d), BF16),
            pltpu.VMEM((nf, tm + 8, tf), F32),
            pltpu.VMEM((nf, tm + 8, tf), F32),
            pltpu.VMEM((tm, d), F32),
        ],
        compiler_params=pltpu.CompilerParams(
            dimension_semantics=("parallel", "arbitrary", "arbitrary"), vmem_limit_bytes=VMEM_LIMIT),
        name="conv_ffn",
    )(h3, g, w_up, w_up, cw, cw, cb, cb, w_down)


def _rmsnorm_kernel(x_ref, g_ref, o_ref):
    x = x_ref[...]
    ms = jnp.mean(x * x, axis=-1, keepdims=True)
    o_ref[...] = x * lax.rsqrt(ms + EPS) * g_ref[...]


def _rmsnorm(x, g, tm):
    m, d = x.shape
    return pl.pallas_call(
        _rmsnorm_kernel,
        grid=(m // tm,),
        in_specs=[pl.BlockSpec((tm, d), lambda i: (i, 0)), pl.BlockSpec((1, d), lambda i: (0, 0))],
        out_specs=pl.BlockSpec((tm, d), lambda i: (i, 0)),
        out_shape=jax.ShapeDtypeStruct((m, d), F32),
        compiler_params=pltpu.CompilerParams(dimension_semantics=("parallel",)),
        name="final_rmsnorm",
    )(x, g)


_D = 1024
_OFF_MLSTM = 3 * _D
_OFF_GATES = 6 * _D
_OFF_SZ = 9 * _D
_OFF_SX = 10 * _D
_OFF_SBC = 11 * _D
_OFF_SMALL = 11 * _D + _D // 2
_N_PROJ = 12 * _D


def _pick(n, cands):
    for c in cands:
        if n % c == 0:
            return c
    return n


def kernel(x, norm_mix, w_in, gla_wa, gla_ba, gla_norm, mlstm_conv_w, mlstm_conv_b, mlstm_bi, mlstm_bf,
           mlstm_norm, ssd_conv_w, ssd_conv_b, ssd_dt_bias, ssd_a_log, ssd_d, ssd_norm, gate_b, w_branch,
           w_out, norm_ffn, w_up, ffn_conv_w, ffn_conv_b, w_down, norm_final):
    bsz, t, d = x.shape
    assert d == _D
    depth = w_in.shape[0]
    m = bsz * t
    d_ff = w_down.shape[1]
    qk = d // 2
    rank = gla_wa.shape[1]
    bc_w = 2 * SSD_GROUPS * SSD_STATE
    sizes = (qk, qk, d, rank, d, 2 * qk, d, MLSTM_HEADS, MLSTM_HEADS, d, d, d + bc_w, SSD_HEADS, 3 * d)
    offs = [0]
    for s in sizes:
        offs.append(offs[-1] + s)
    (o_gq, o_gk, o_gv, o_ga, o_gg, o_mqk, o_mv, o_mi, o_mf, o_mo, o_sz, o_sxbc, o_sdt, o_gates, o_end) = offs
    assert o_end == w_in.shape[2]

    def cols(a, lo, hi):
        return a[..., lo:hi]

    w_perm = jnp.concatenate([
        cols(w_in, o_gq, o_ga),
        cols(w_in, o_gg, o_mi),
        cols(w_in, o_mo, o_sz),
        cols(w_in, o_gates, o_end),
        cols(w_in, o_sz, o_sdt),
        cols(w_in, o_ga, o_gg),
        cols(w_in, o_mi, o_mo),
        cols(w_in, o_sdt, o_gates),
        jnp.zeros((depth, d, _N_PROJ - _OFF_SMALL - rank - 2 * MLSTM_HEADS - SSD_HEADS), w_in.dtype),
    ], axis=-1).astype(BF16)
    assert w_perm.shape[-1] == _N_PROJ

    pad_small = lambda a, off: jnp.pad(a, ((0, 0), (off, SM_W - off - a.shape[1])))
    wa_pad = jnp.pad(gla_wa, ((0, 0), (SM_GA, SM_W - SM_GA - rank), (0, 0)))
    m_brow = pad_small(mlstm_bi, SM_MI) + pad_small(mlstm_bf, SM_MF)
    s_brow = pad_small(ssd_dt_bias, SM_DT)
    alog = pad_small(ssd_a_log, SM_DT)
    dexp = jnp.repeat(ssd_d, SSD_HEAD_DIM, axis=-1)
    head_of_col = jnp.arange(d) // SSD_HEAD_DIM
    emat1 = (jnp.arange(SM_W)[:, None] == (SM_DT + head_of_col)[None, :]).astype(BF16)
    emat = jnp.concatenate([emat1, emat1], axis=0)
    w_branch_b = w_branch.astype(BF16)
    w_out_b = w_out.astype(BF16)
    w_up_b = w_up.astype(BF16)
    w_down_b = w_down.astype(BF16)

    tm_proj = _pick(m, (1024, 512, 256, 128, 64))
    tb = _pick(t, (256, 128, 64))
    tm_merge = _pick(m, (512, 256, 128, 64))
    tm_ffn = _pick(t, (512, 256, 128, 64))
    tf = d_ff // 2

    h = x.reshape(m, d)
    for l in range(depth):
        proj2 = _norm_matmul(h, norm_mix[l][None, :], w_perm[l], tm_proj, 1024, F32)
        proj3 = proj2.reshape(bsz, t, _N_PROJ)
        y_gla = _gla(proj3, wa_pad[l], gla_ba[l][None, :], gla_norm[l][None, :], tb)
        y_m = _mlstm(proj3, mlstm_conv_w[l], mlstm_conv_b[l][None, :], m_brow[l][None, :], m_brow[l][:, None],
                     mlstm_norm[l][None, :], tb)
        y_s = _ssd(proj3, ssd_conv_w[l][:, :d], ssd_conv_b[l][None, :d], ssd_conv_w[l][:, d:],
                   ssd_conv_b[l][None, d:], s_brow[l][None, :], s_brow[l][:, None], alog[l][None, :],
                   alog[l][:, None], dexp[l][None, :], ssd_norm[l][None, :], emat, tb)
        h = _merge(y_gla.reshape(m, d), y_m.reshape(m, d), y_s.reshape(m, d), proj2, gate_b[l][None, :],
                   w_branch_b[l], w_out_b[l], h, tm_merge)
        h = _ffn(h.reshape(bsz, t, d), norm_ffn[l][None, :], w_up_b[l], ffn_conv_w[l], ffn_conv_b[l][None, :],
                 w_down_b[l], tm_ffn, tf).reshape(m, d)
    out = _rmsnorm(h, norm_final[None, :], _pick(m, (512, 256, 128, 64)))
    return out.reshape(bsz, t, d)
```

```python
import functools

import jax
import jax.numpy as jnp
from jax import lax
from jax.experimental import pallas as pl
from jax.experimental.pallas import tpu as pltpu

F32 = jnp.float32
BF16 = jnp.bfloat16

EPS = 1e-6
CHUNK = 64
GLA_HEADS = 4
GLA_TAU = 16.0
MLSTM_HEADS = 4
SSD_HEADS = 16
SSD_GROUPS = 2
SSD_STATE = 128
SSD_HEAD_DIM = 64
FFN_CONV = 3
CONV_K = 4
PAD = 16

SM_GA = 0
SM_MI = 16
SM_MF = 20
SM_DT = 24
SM_W = 128

VMEM_LIMIT = 52 * 1024 * 1024


def _dot(a, b):
    return jnp.dot(a.astype(BF16), b.astype(BF16), preferred_element_type=F32)


def _dot_nt(a, b):
    return lax.dot_general(a.astype(BF16), b.astype(BF16), (((1,), (1,)), ((), ())),
                           preferred_element_type=F32)


def _dot_tn(a, b):
    return jnp.dot(a.astype(F32).T.astype(BF16), b.astype(BF16), preferred_element_type=F32)


def _split2(x):
    hi = x.astype(BF16)
    return hi, (x - hi.astype(F32)).astype(BF16)


def _cumsum_rows(tril2, x):
    return jnp.dot(tril2, jnp.concatenate(_split2(x), axis=0), preferred_element_type=F32)


def _cumsum_cols(x, triu2):
    return jnp.dot(jnp.concatenate(_split2(x), axis=1), triu2, preferred_element_type=F32)


def _sigmoid(x):
    return 1.0 / (1.0 + jnp.exp(-x))


def _silu(x):
    return x * _sigmoid(x)


def _softplus(x):
    return jnp.maximum(x, 0.0) + jnp.log1p(jnp.exp(-jnp.abs(x)))


def _log_sigmoid(x):
    return jnp.minimum(x, 0.0) - jnp.log1p(jnp.exp(-jnp.abs(x)))


def _causal():
    row = lax.broadcasted_iota(jnp.int32, (CHUNK, CHUNK), 0)
    col = lax.broadcasted_iota(jnp.int32, (CHUNK, CHUNK), 1)
    return col <= row


def _tril2():
    row = lax.broadcasted_iota(jnp.int32, (CHUNK, 2 * CHUNK), 0)
    col = lax.broadcasted_iota(jnp.int32, (CHUNK, 2 * CHUNK), 1)
    return jnp.where((col <= row) | ((col >= CHUNK) & (col - CHUNK <= row)), 1.0, 0.0).astype(BF16)


def _triu2():
    row = lax.broadcasted_iota(jnp.int32, (2 * CHUNK, CHUNK), 0)
    col = lax.broadcasted_iota(jnp.int32, (2 * CHUNK, CHUNK), 1)
    return jnp.where((row <= col) | ((row >= CHUNK) & (row - CHUNK <= col)), 1.0, 0.0).astype(BF16)


def _shift_mat(taps):
    row = lax.broadcasted_iota(jnp.int32, (taps * CHUNK, PAD + CHUNK), 0)
    col = lax.broadcasted_iota(jnp.int32, (taps * CHUNK, PAD + CHUNK), 1)
    sel = None
    for j in range(taps):
        hit = (row >= j * CHUNK) & (row < (j + 1) * CHUNK) & (col == row - j * CHUNK + PAD - j)
        sel = hit if sel is None else sel | hit
    return jnp.where(sel, 1.0, 0.0).astype(BF16)


def _layer_spec(arr, l):
    zeros = (0,) * (arr.ndim - 1)
    return pl.BlockSpec((None,) + arr.shape[1:], lambda *_: (l,) + zeros)


def _proj_kernel(x_ref, g_ref, w_ref, o_ref, sm_ref, xn_ref, *, sm_tile, sm_col):
    j = pl.program_id(1)

    @pl.when(j == 0)
    def _():
        x = x_ref[...]
        ms = jnp.mean(x * x, axis=-1, keepdims=True)
        xn_ref[...] = (x * lax.rsqrt(ms + EPS) * g_ref[...]).astype(BF16)

    acc = jnp.dot(xn_ref[...], w_ref[...], preferred_element_type=F32)
    o_ref[...] = acc.astype(o_ref.dtype)

    @pl.when(j == sm_tile)
    def _():
        sm_ref[...] = acc[:, sm_col:sm_col + SM_W]


def _proj(x, g, w, l, tm, tn):
    m, d = x.shape
    n = w.shape[2]
    return pl.pallas_call(
        functools.partial(_proj_kernel, sm_tile=_OFF_SMALL // tn, sm_col=_OFF_SMALL % tn),
        grid=(m // tm, n // tn),
        in_specs=[
            pl.BlockSpec((tm, d), lambda i, j: (i, 0)),
            _layer_spec(g, l),
            pl.BlockSpec((None, d, tn), lambda i, j: (l, 0, j)),
        ],
        out_specs=[
            pl.BlockSpec((tm, tn), lambda i, j: (i, j)),
            pl.BlockSpec((tm, SM_W), lambda i, j: (i, 0)),
        ],
        out_shape=[jax.ShapeDtypeStruct((m, n), BF16), jax.ShapeDtypeStruct((m, SM_W), F32)],
        scratch_shapes=[pltpu.VMEM((tm, d), BF16)],
        compiler_params=pltpu.CompilerParams(
            dimension_semantics=("parallel", "arbitrary"), vmem_limit_bytes=VMEM_LIMIT),
        name="in_proj",
    )(x, g, w)


def _gla_kernel(q_ref, k_ref, v_ref, gg_ref, sm_ref, wa_ref, ba_ref, nw_ref, y_ref, st_ref, *, tb):
    dk = q_ref.shape[2] // GLA_HEADS
    dv = v_ref.shape[2] // GLA_HEADS

    @pl.when(pl.program_id(1) == 0)
    def _():
        st_ref[...] = jnp.zeros_like(st_ref)

    chunks = range(tb // CHUNK)
    heads = range(GLA_HEADS)
    causal = _causal()
    tril2 = _tril2()
    scale = dk ** -0.5
    rows = [slice(c * CHUNK, (c + 1) * CHUNK) for c in chunks]

    pre = []
    for c in chunks:
        hi, lo = _split2(sm_ref[0, rows[c], :])
        pre.append(jnp.dot(jnp.concatenate([hi, lo, hi], axis=1), wa_ref[...], preferred_element_type=F32))
    la = [_log_sigmoid(pre[c] + ba_ref[...]) * (1.0 / GLA_TAU) for c in chunks]
    bcum = [_cumsum_rows(tril2, la[c]) for c in chunks]

    for c in chunks:
        r = rows[c]
        kcs = [slice(h * dk, (h + 1) * dk) for h in heads]
        vcs = [slice(h * dv, (h + 1) * dv) for h in heads]
        b = [bcum[c][:, kcs[h]] for h in heads]
        bl = [b[h][CHUNK - 1:CHUNK, :] for h in heads]
        q = [q_ref[0, r, kcs[h]].astype(F32) * scale for h in heads]
        k = [k_ref[0, r, kcs[h]].astype(F32) for h in heads]
        v = [v_ref[0, r, vcs[h]] for h in heads]
        qd = [q[h] * jnp.exp(b[h]) for h in heads]
        ki = [k[h] * jnp.exp(-b[h]) for h in heads]
        kd = [k[h] * jnp.exp(bl[h] - b[h]) for h in heads]
        st = [st_ref[h] for h in heads]
        sc = [_dot_nt(qd[h], ki[h]) for h in heads]
        o2 = [_dot_nt(qd[h], st[h]) for h in heads]
        ds = [_dot_tn(v[h], kd[h]) for h in heads]
        o1 = [_dot(jnp.where(causal, sc[h], 0.0), v[h]) for h in heads]
        for h in heads:
            st_ref[h] = st[h] * jnp.exp(bl[h]) + ds[h]
            o = o1[h] + o2[h]
            ms = jnp.mean(o * o, axis=-1, keepdims=True)
            y = (o * lax.rsqrt(ms + EPS) * nw_ref[:, vcs[h]]) * _silu(gg_ref[0, r, vcs[h]].astype(F32))
            y_ref[0, r, vcs[h]] = y.astype(y_ref.dtype)


def _gla(proj, small, wa3, ba, nw, l, tb):
    bsz, t, _ = proj.shape
    qk_w = ba.shape[2]
    v_w = nw.shape[2]
    dk = qk_w // GLA_HEADS
    dv = v_w // GLA_HEADS
    return pl.pallas_call(
        functools.partial(_gla_kernel, tb=tb),
        grid=(bsz, t // tb),
        in_specs=[
            pl.BlockSpec((1, tb, qk_w), lambda b, i: (b, i, 0)),
            pl.BlockSpec((1, tb, qk_w), lambda b, i: (b, i, 1)),
            pl.BlockSpec((1, tb, v_w), lambda b, i: (b, i, 1)),
            pl.BlockSpec((1, tb, v_w), lambda b, i: (b, i, 2)),
            pl.BlockSpec((1, tb, SM_W), lambda b, i: (b, i, 0)),
            _layer_spec(wa3, l),
            _layer_spec(ba, l),
            _layer_spec(nw, l),
        ],
        out_specs=pl.BlockSpec((1, tb, v_w), lambda b, i: (b, i, 0)),
        out_shape=jax.ShapeDtypeStruct((bsz, t, v_w), BF16),
        scratch_shapes=[pltpu.VMEM((GLA_HEADS, dv, dk), F32)],
        compiler_params=pltpu.CompilerParams(
            dimension_semantics=("parallel", "arbitrary"), vmem_limit_bytes=VMEM_LIMIT),
        name="gla",
    )(proj, proj, proj, proj, small, wa3, ba, nw)


def _mlstm_kernel(qk_ref, v_ref, og_ref, sm_ref, cw_ref, cb_ref, brow_ref, bcol_ref, nw_ref, y_ref,
                  xpad_ref, c_ref, n_ref, m_ref, *, tb):
    qk_w = qk_ref.shape[2] // 2
    dk = qk_w // MLSTM_HEADS
    dv = v_ref.shape[2] // MLSTM_HEADS

    @pl.when(pl.program_id(1) == 0)
    def _():
        xpad_ref[0:PAD, :] = jnp.zeros((PAD, xpad_ref.shape[1]), xpad_ref.dtype)
        c_ref[...] = jnp.zeros_like(c_ref)
        n_ref[...] = jnp.zeros_like(n_ref)
        m_ref[...] = jnp.zeros_like(m_ref)

    xpad_ref[PAD:PAD + tb, :] = qk_ref[0]
    chunks = range(tb // CHUNK)
    heads = range(MLSTM_HEADS)
    pairs = [(c, h) for c in chunks for h in heads]
    causal = _causal()
    tril2 = _tril2()
    triu2 = _triu2()
    shift = _shift_mat(CONV_K)
    kscale = dk ** -0.5
    neg_inf = -jnp.inf
    rows = [slice(c * CHUNK, (c + 1) * CHUNK) for c in chunks]

    sm = [sm_ref[0, rows[c], :] for c in chunks]
    pre_col = [sm[c] + brow_ref[...] for c in chunks]
    pre_row = [(sm[c].T + bcol_ref[...])[SM_MI:SM_MI + 2 * MLSTM_HEADS] for c in chunks]
    b_col = [_cumsum_rows(tril2, _log_sigmoid(pre_col[c])) for c in chunks]
    b_row = [_cumsum_cols(_log_sigmoid(pre_row[c]), triu2) for c in chunks]
    xs = [jnp.dot(shift, xpad_ref[c * CHUNK:c * CHUNK + PAD + CHUNK, :], preferred_element_type=F32)
          for c in chunks]
    act = []
    for c in chunks:
        acc = cb_ref[...]
        for k in range(CONV_K):
            j = CONV_K - 1 - k
            acc = acc + cw_ref[k:k + 1, :] * xs[c][j * CHUNK:(j + 1) * CHUNK]
        act.append(_silu(acc))

    bc, br, ir, bl, m_loc, w = {}, {}, {}, {}, {}, {}
    for c, h in pairs:
        bc[c, h] = b_col[c][:, SM_MF + h:SM_MF + h + 1]
        br[c, h] = b_row[c][MLSTM_HEADS + h:MLSTM_HEADS + h + 1, :]
        ir[c, h] = pre_row[c][h:h + 1, :]
        bl[c, h] = bc[c, h][CHUNK - 1:CHUNK, :]
        a = bl[c, h] - bc[c, h] + pre_col[c][:, SM_MI + h:SM_MI + h + 1]
        m_loc[c, h] = jnp.max(a, axis=0, keepdims=True)
        w[c, h] = jnp.exp(a - m_loc[c, h])
    m_prev, s_old, s_new = {}, {}, {}
    for h in heads:
        m = m_ref[h][0:1, 0:1]
        for c in chunks:
            m_prev[c, h] = m
            m_new = jnp.maximum(bl[c, h] + m, m_loc[c, h])
            s_old[c, h] = jnp.exp(bl[c, h] + m - m_new)
            s_new[c, h] = jnp.exp(m_loc[c, h] - m_new)
            m = m_new
        m_ref[h] = jnp.broadcast_to(m, m_ref.shape[1:])

    q = {(c, h): act[c][:, h * dk:(h + 1) * dk] for c, h in pairs}
    k = {(c, h): act[c][:, qk_w + h * dk:qk_w + (h + 1) * dk] * kscale for c, h in pairs}
    v = {(c, h): v_ref[0, rows[c], h * dv:(h + 1) * dv] for c, h in pairs}
    wk = {p: w[p] * k[p] for p in pairs}

    qk = {p: _dot_nt(q[p], k[p]) for p in pairs}
    dcs = {p: _dot_tn(wk[p], v[p]) for p in pairs}

    c_prev, n_prev = {}, {}
    for h in heads:
        cst = c_ref[h]
        nst = n_ref[h][0:1, :]
        for c in chunks:
            c_prev[c, h] = cst.astype(BF16)
            n_prev[c, h] = nst
            cst = s_old[c, h] * cst + s_new[c, h] * dcs[c, h]
            nst = s_old[c, h] * nst + s_new[c, h] * jnp.sum(wk[c, h], axis=0, keepdims=True)
        c_ref[h] = cst
        n_ref[h] = jnp.broadcast_to(nst, n_ref.shape[1:])

    m_t, wts, s_inter = {}, {}, {}
    for p in pairs:
        log_d = jnp.where(causal, bc[p] - br[p] + ir[p], neg_inf)
        m_inter = bc[p] + m_prev[p]
        m_t[p] = jnp.maximum(m_inter, jnp.max(log_d, axis=-1, keepdims=True))
        wts[p] = qk[p] * jnp.exp(log_d - m_t[p])
        s_inter[p] = jnp.exp(m_inter - m_t[p])

    pv = {p: _dot(wts[p], v[p]) for p in pairs}
    qc = {p: _dot(q[p], c_prev[p]) for p in pairs}

    for c, h in pairs:
        p = (c, h)
        vc = slice(h * dv, (h + 1) * dv)
        num = pv[p] + s_inter[p] * qc[p]
        den = (jnp.sum(wts[p], axis=-1, keepdims=True)
               + s_inter[p] * jnp.sum(q[p] * n_prev[p], axis=-1, keepdims=True))
        hout = num / jnp.maximum(jnp.abs(den), jnp.exp(-m_t[p]))
        ms = jnp.mean(hout * hout, axis=-1, keepdims=True)
        y = (hout * lax.rsqrt(ms + EPS) * nw_ref[:, vc]) * _sigmoid(og_ref[0, rows[c], vc].astype(F32))
        y_ref[0, rows[c], vc] = y.astype(y_ref.dtype)

    xpad_ref[0:PAD, :] = xpad_ref[tb:tb + PAD, :]


def _mlstm(proj, small, cw, cb, brow, bcol, nw, l, tb):
    bsz, t, _ = proj.shape
    qk2 = cw.shape[2]
    v_w = nw.shape[2]
    dk = qk2 // 2 // MLSTM_HEADS
    dv = v_w // MLSTM_HEADS
    base = _OFF_MLSTM // v_w
    return pl.pallas_call(
        functools.partial(_mlstm_kernel, tb=tb),
        grid=(bsz, t // tb),
        in_specs=[
            pl.BlockSpec((1, tb, qk2), lambda b, i: (b, i, base)),
            pl.BlockSpec((1, tb, v_w), lambda b, i: (b, i, base + 1)),
            pl.BlockSpec((1, tb, v_w), lambda b, i: (b, i, base + 2)),
            pl.BlockSpec((1, tb, SM_W), lambda b, i: (b, i, 0)),
            _layer_spec(cw, l),
            _layer_spec(cb, l),
            _layer_spec(brow, l),
            _layer_spec(bcol, l),
            _layer_spec(nw, l),
        ],
        out_specs=pl.BlockSpec((1, tb, v_w), lambda b, i: (b, i, 0)),
        out_shape=jax.ShapeDtypeStruct((bsz, t, v_w), BF16),
        scratch_shapes=[
            pltpu.VMEM((tb + PAD, qk2), BF16),
            pltpu.VMEM((MLSTM_HEADS, dk, dv), F32),
            pltpu.VMEM((MLSTM_HEADS, 8, dk), F32),
            pltpu.VMEM((MLSTM_HEADS, 8, 128), F32),
        ],
        compiler_params=pltpu.CompilerParams(
            dimension_semantics=("parallel", "arbitrary"), vmem_limit_bytes=VMEM_LIMIT),
        name="mlstm",
    )(proj, proj, proj, small, cw, cb, brow, bcol, nw)


def _ssd_kernel(z_ref, x_ref, bc_ref, sm_ref, cwx_ref, cbx_ref, cwbc_ref, cbbc_ref, brow_ref, bcol_ref,
                alog_row_ref, alog_col_ref, dexp_ref, nw_ref, e_ref, y_ref,
                xpad_ref, bcpad_ref, s_ref, ysc_ref, *, tb):
    inner = x_ref.shape[2]
    gw = inner // SSD_GROUPS
    hpg = SSD_HEADS // SSD_GROUPS
    p = SSD_HEAD_DIM
    z = SSD_STATE

    @pl.when(pl.program_id(1) == 0)
    def _():
        xpad_ref[0:PAD, :] = jnp.zeros((PAD, xpad_ref.shape[1]), xpad_ref.dtype)
        bcpad_ref[0:PAD, :] = jnp.zeros((PAD, bcpad_ref.shape[1]), bcpad_ref.dtype)
        s_ref[...] = jnp.zeros_like(s_ref)

    xpad_ref[PAD:PAD + tb, :] = x_ref[0]
    bcpad_ref[PAD:PAD + tb, :] = bc_ref[0]
    chunks = range(tb // CHUNK)
    groups = range(SSD_GROUPS)
    causal = _causal()
    tril2 = _tril2()
    triu2 = _triu2()
    shift = _shift_mat(CONV_K)
    neg_inf = -jnp.inf
    a_row = -jnp.exp(alog_row_ref[...])
    a_col = -jnp.exp(alog_col_ref[...])[SM_DT:SM_DT + SSD_HEADS]
    rows = [slice(c * CHUNK, (c + 1) * CHUNK) for c in chunks]

    def conv(xs, cw, cb):
        acc = cb
        for k in range(CONV_K):
            j = CONV_K - 1 - k
            acc = acc + cw[k:k + 1, :] * xs[j * CHUNK:(j + 1) * CHUNK]
        return _silu(acc)

    sm = [sm_ref[0, rows[c], :] for c in chunks]
    dt_col = [_softplus(sm[c] + brow_ref[...]) for c in chunks]
    dt_row = [_softplus((sm[c].T + bcol_ref[...])[SM_DT:SM_DT + SSD_HEADS]) for c in chunks]
    b_col = [_cumsum_rows(tril2, dt_col[c] * a_row) for c in chunks]
    b_row = [_cumsum_cols(dt_row[c] * a_col, triu2) for c in chunks]
    xsx = [jnp.dot(shift, xpad_ref[c * CHUNK:c * CHUNK + PAD + CHUNK, :], preferred_element_type=F32)
           for c in chunks]
    xsb = [jnp.dot(shift, bcpad_ref[c * CHUNK:c * CHUNK + PAD + CHUNK, :], preferred_element_type=F32)
           for c in chunks]
    expd = []
    for c in chunks:
        bl_row = b_col[c][CHUNK - 1:CHUNK, :]
        w_col = jnp.exp(bl_row - b_col[c]) * dt_col[c]
        eb_col = jnp.exp(b_col[c])
        ebl = jnp.broadcast_to(jnp.exp(bl_row), (8, SM_W))
        hi, lo = _split2(jnp.concatenate([w_col, eb_col, ebl], axis=0))
        expd.append(jnp.dot(jnp.concatenate([hi, lo], axis=1), e_ref[...], preferred_element_type=F32))
    xa = [conv(xsx[c], cwx_ref[...], cbx_ref[...]) for c in chunks]
    bca = [conv(xsb[c], cwbc_ref[...], cbbc_ref[...]) for c in chunks]

    for c in chunks:
        w_x = expd[c][0:CHUNK]
        eb_x = expd[c][CHUNK:2 * CHUNK]
        ebl_x = expd[c][2 * CHUNK:2 * CHUNK + 1]
        xg = [xa[c][:, g * gw:(g + 1) * gw] for g in groups]
        bg = [bca[c][:, g * z:(g + 1) * z] for g in groups]
        cg = [bca[c][:, SSD_GROUPS * z + g * z:SSD_GROUPS * z + (g + 1) * z] for g in groups]
        cb = [_dot_nt(cg[g], bg[g]) for g in groups]
        s_prev = [s_ref[g] for g in groups]
        inter = [_dot(cg[g], s_prev[g]) for g in groups]
        dst = [_dot_tn(bg[g], xg[g] * w_x[:, g * gw:(g + 1) * gw]) for g in groups]
        mixes = []
        for h in range(SSD_HEADS):
            bc = b_col[c][:, SM_DT + h:SM_DT + h + 1]
            br = b_row[c][h:h + 1, :]
            dtr = dt_row[c][h:h + 1, :]
            mixes.append(cb[h // hpg] * jnp.exp(jnp.where(causal, bc - br, neg_inf)) * dtr)
        ys = [_dot(mixes[h], xg[h // hpg][:, (h % hpg) * p:(h % hpg + 1) * p]) for h in range(SSD_HEADS)]
        for h in range(SSD_HEADS):
            ysc_ref[:, h * p:(h + 1) * p] = ys[h]
        for g in groups:
            gc = slice(g * gw, (g + 1) * gw)
            s_ref[g] = s_prev[g] * ebl_x[:, gc] + dst[g]
            y = ysc_ref[:, gc] + inter[g] * eb_x[:, gc] + dexp_ref[:, gc] * xg[g]
            y = y * _silu(z_ref[0, rows[c], gc].astype(F32))
            ms = jnp.mean(y * y, axis=-1, keepdims=True)
            y_ref[0, rows[c], gc] = (y * lax.rsqrt(ms + EPS) * nw_ref[:, gc]).astype(y_ref.dtype)

    xpad_ref[0:PAD, :] = xpad_ref[tb:tb + PAD, :]
    bcpad_ref[0:PAD, :] = bcpad_ref[tb:tb + PAD, :]


def _ssd(proj, small, cwx, cbx, cwbc, cbbc, brow, bcol, alog_row, alog_col, dexp, nw, emat, l, tb):
    bsz, t, _ = proj.shape
    inner = nw.shape[2]
    bcw = cwbc.shape[2]
    return pl.pallas_call(
        functools.partial(_ssd_kernel, tb=tb),
        grid=(bsz, t // tb),
        in_specs=[
            pl.BlockSpec((1, tb, inner), lambda b, i: (b, i, _OFF_SZ // inner)),
            pl.BlockSpec((1, tb, inner), lambda b, i: (b, i, _OFF_SX // inner)),
            pl.BlockSpec((1, tb, bcw), lambda b, i: (b, i, _OFF_SBC // bcw)),
            pl.BlockSpec((1, tb, SM_W), lambda b, i: (b, i, 0)),
            _layer_spec(cwx, l),
            _layer_spec(cbx, l),
            _layer_spec(cwbc, l),
            _layer_spec(cbbc, l),
            _layer_spec(brow, l),
            _layer_spec(bcol, l),
            _layer_spec(alog_row, l),
            _layer_spec(alog_col, l),
            _layer_spec(dexp, l),
            _layer_spec(nw, l),
            pl.BlockSpec(emat.shape, lambda b, i: (0, 0)),
        ],
        out_specs=pl.BlockSpec((1, tb, inner), lambda b, i: (b, i, 0)),
        out_shape=jax.ShapeDtypeStruct((bsz, t, inner), BF16),
        scratch_shapes=[
            pltpu.VMEM((tb + PAD, inner), BF16),
            pltpu.VMEM((tb + PAD, bcw), BF16),
            pltpu.VMEM((SSD_GROUPS, SSD_STATE, inner // SSD_GROUPS), F32),
            pltpu.VMEM((CHUNK, inner), F32),
        ],
        compiler_params=pltpu.CompilerParams(
            dimension_semantics=("parallel", "arbitrary"), vmem_limit_bytes=VMEM_LIMIT),
        name="ssd",
    )(proj, proj, proj, small, cwx, cbx, cwbc, cbbc, brow, bcol, alog_row, alog_col, dexp, nw, emat)


def _merge_kernel(ya_ref, yb_ref, yc_ref, gates_ref, gb_ref, wb_ref, wo_ref, h_ref, o_ref):
    d = h_ref.shape[1]
    acc = None
    for k, y_ref in enumerate((ya_ref, yb_ref, yc_ref)):
        zk = jnp.dot(y_ref[...], wb_ref[k], preferred_element_type=F32)
        gk = _sigmoid(gates_ref[:, k * d:(k + 1) * d].astype(F32) + gb_ref[:, k * d:(k + 1) * d])
        acc = gk * zk if acc is None else acc + gk * zk
    o_ref[...] = h_ref[...] + jnp.dot(acc.astype(BF16), wo_ref[...], preferred_element_type=F32)


def _merge(ya, yb, yc, proj2, gate_b, w_branch, w_out, h, l, tm):
    m, d = h.shape
    nb = w_branch.shape[1]
    return pl.pallas_call(
        _merge_kernel,
        grid=(m // tm,),
        in_specs=[
            pl.BlockSpec((tm, d), lambda i: (i, 0)),
            pl.BlockSpec((tm, d), lambda i: (i, 0)),
            pl.BlockSpec((tm, d), lambda i: (i, 0)),
            pl.BlockSpec((tm, nb * d), lambda i: (i, _OFF_GATES // (nb * d))),
            _layer_spec(gate_b, l),
            _layer_spec(w_branch, l),
            _layer_spec(w_out, l),
            pl.BlockSpec((tm, d), lambda i: (i, 0)),
        ],
        out_specs=pl.BlockSpec((tm, d), lambda i: (i, 0)),
        out_shape=jax.ShapeDtypeStruct((m, d), F32),
        compiler_params=pltpu.CompilerParams(
            dimension_semantics=("parallel",), vmem_limit_bytes=VMEM_LIMIT),
        name="merge",
    )(ya, yb, yc, proj2, gate_b, w_branch, w_out, h)


def _ffn_kernel(h_ref, g_ref, wg_ref, wv_ref, cwg_ref, cwv_ref, cbg_ref, cbv_ref, wd_ref, o_ref,
                xn_ref, gpad_ref, vpad_ref, acc_ref, *, tm):
    t = pl.program_id(1)
    f = pl.program_id(2)
    nf = pl.num_programs(2)

    @pl.when(f == 0)
    def _():
        x = h_ref[0]
        ms = jnp.mean(x * x, axis=-1, keepdims=True)
        xn_ref[...] = (x * lax.rsqrt(ms + EPS) * g_ref[...]).astype(BF16)

    @pl.when(t == 0)
    def _():
        gpad_ref[f, 0:8, :] = jnp.zeros((8, gpad_ref.shape[2]), F32)
        vpad_ref[f, 0:8, :] = jnp.zeros((8, vpad_ref.shape[2]), F32)

    xn = xn_ref[...]
    gpad_ref[f, 8:8 + tm, :] = jnp.dot(xn, wg_ref[...], preferred_element_type=F32)
    vpad_ref[f, 8:8 + tm, :] = jnp.dot(xn, wv_ref[...], preferred_element_type=F32)
    gate = cbg_ref[...]
    val = cbv_ref[...]
    for k in range(FFN_CONV):
        s = FFN_CONV - 1 - k
        gate = gate + cwg_ref[k:k + 1, :] * gpad_ref[f, 8 - s:8 - s + tm, :]
        val = val + cwv_ref[k:k + 1, :] * vpad_ref[f, 8 - s:8 - s + tm, :]
    gpad_ref[f, 0:8, :] = gpad_ref[f, tm:tm + 8, :]
    vpad_ref[f, 0:8, :] = vpad_ref[f, tm:tm + 8, :]
    act = (_silu(gate) * val).astype(BF16)
    part = jnp.dot(act, wd_ref[...], preferred_element_type=F32)

    @pl.when(f == 0)
    def _():
        acc_ref[...] = part

    @pl.when(f > 0)
    def _():
        acc_ref[...] += part

    @pl.when(f == nf - 1)
    def _():
        o_ref[0] = h_ref[0] + acc_ref[...]


def _ffn(h3, g, w_up, cw, cb, w_down, l, tm, tf):
    bsz, t, d = h3.shape
    dff = w_down.shape[1]
    nf = dff // tf
    return pl.pallas_call(
        functools.partial(_ffn_kernel, tm=tm),
        grid=(bsz, t // tm, nf),
        in_specs=[
            pl.BlockSpec((1, tm, d), lambda b, i, f: (b, i, 0)),
            _layer_spec(g, l),
            pl.BlockSpec((None, d, tf), lambda b, i, f: (l, 0, f)),
            pl.BlockSpec((None, d, tf), lambda b, i, f: (l, 0, nf + f)),
            pl.BlockSpec((None, FFN_CONV, tf), lambda b, i, f: (l, 0, f)),
            pl.BlockSpec((None, FFN_CONV, tf), lambda b, i, f: (l, 0, nf + f)),
            pl.BlockSpec((None, 1, tf), lambda b, i, f: (l, 0, f)),
            pl.BlockSpec((None, 1, tf), lambda b, i, f: (l, 0, nf + f)),
            pl.BlockSpec((None, tf, d), lambda b, i, f: (l, f, 0)),
        ],
        out_specs=pl.BlockSpec((1, tm, d), lambda b, i, f: (b, i, 0)),
        out_shape=jax.ShapeDtypeStruct((bsz, t, d), F32),
        scratch_shapes=[
            pltpu.VMEM((tm, d), BF16),
            pltpu.VMEM((nf, tm + 8, tf), F32),
            pltpu.VMEM((nf, tm + 8, tf), F32),
            pltpu.VMEM((tm, d), F32),
        ],
        compiler_params=pltpu.CompilerParams(
            dimension_semantics=("parallel", "arbitrary", "arbitrary"), vmem_limit_bytes=VMEM_LIMIT),
        name="conv_ffn",
    )(h3, g, w_up, w_up, cw, cw, cb, cb, w_down)


def _rmsnorm_kernel(x_ref, g_ref, o_ref):
    x = x_ref[...]
    ms = jnp.mean(x * x, axis=-1, keepdims=True)
    o_ref[...] = x * lax.rsqrt(ms + EPS) * g_ref[...]


def _rmsnorm(x, g, tm):
    m, d = x.shape
    return pl.pallas_call(
        _rmsnorm_kernel,
        grid=(m // tm,),
        in_specs=[pl.BlockSpec((tm, d), lambda i: (i, 0)), pl.BlockSpec((1, d), lambda i: (0, 0))],
        out_specs=pl.BlockSpec((tm, d), lambda i: (i, 0)),
        out_shape=jax.ShapeDtypeStruct((m, d), F32),
        compiler_params=pltpu.CompilerParams(dimension_semantics=("parallel",)),
        name="final_rmsnorm",
    )(x, g)


_D = 1024
_OFF_MLSTM = 3 * _D
_OFF_GATES = 6 * _D
_OFF_SZ = 9 * _D
_OFF_SX = 10 * _D
_OFF_SBC = 11 * _D
_OFF_SMALL = 11 * _D + _D // 2
_N_PROJ = 12 * _D


def _pick(n, cands):
    for c in cands:
        if n % c == 0:
            return c
    return n


def kernel(x, norm_mix, w_in, gla_wa, gla_ba, gla_norm, mlstm_conv_w, mlstm_conv_b, mlstm_bi, mlstm_bf,
           mlstm_norm, ssd_conv_w, ssd_conv_b, ssd_dt_bias, ssd_a_log, ssd_d, ssd_norm, gate_b, w_branch,
           w_out, norm_ffn, w_up, ffn_conv_w, ffn_conv_b, w_down, norm_final):
    bsz, t, d = x.shape
    assert d == _D
    depth = w_in.shape[0]
    m = bsz * t
    d_ff = w_down.shape[1]
    qk = d // 2
    rank = gla_wa.shape[1]
    bc_w = 2 * SSD_GROUPS * SSD_STATE
    sizes = (qk, qk, d, rank, d, 2 * qk, d, MLSTM_HEADS, MLSTM_HEADS, d, d, d + bc_w, SSD_HEADS, 3 * d)
    offs = [0]
    for s in sizes:
        offs.append(offs[-1] + s)
    (o_gq, o_gk, o_gv, o_ga, o_gg, o_mqk, o_mv, o_mi, o_mf, o_mo, o_sz, o_sxbc, o_sdt, o_gates, o_end) = offs
    assert o_end == w_in.shape[2]

    def cols(a, lo, hi):
        return a[..., lo:hi]

    w_perm = jnp.concatenate([
        cols(w_in, o_gq, o_ga),
        cols(w_in, o_gg, o_mi),
        cols(w_in, o_mo, o_sz),
        cols(w_in, o_gates, o_end),
        cols(w_in, o_sz, o_sdt),
        cols(w_in, o_ga, o_gg),
        cols(w_in, o_mi, o_mo),
        cols(w_in, o_sdt, o_gates),
        jnp.zeros((depth, d, _N_PROJ - _OFF_SMALL - rank - 2 * MLSTM_HEADS - SSD_HEADS), w_in.dtype),
    ], axis=-1).astype(BF16)
    assert w_perm.shape[-1] == _N_PROJ

    row3 = lambda a: a[:, None, :]
    col3 = lambda a: a[:, :, None]
    pad_small = lambda a, off: jnp.pad(a, ((0, 0), (off, SM_W - off - a.shape[1])))
    wa_pad = jnp.pad(gla_wa, ((0, 0), (SM_GA, SM_W - SM_GA - rank), (0, 0)))
    wa_hi = wa_pad.astype(BF16)
    wa_lo = (wa_pad - wa_hi.astype(F32)).astype(BF16)
    wa3 = jnp.concatenate([wa_hi, wa_hi, wa_lo], axis=1)
    m_bias = pad_small(mlstm_bi, SM_MI) + pad_small(mlstm_bf, SM_MF)
    s_bias = pad_small(ssd_dt_bias, SM_DT)
    alog = pad_small(ssd_a_log, SM_DT)
    dexp = jnp.repeat(ssd_d, SSD_HEAD_DIM, axis=-1)
    head_of_col = jnp.arange(d) // SSD_HEAD_DIM
    emat1 = (jnp.arange(SM_W)[:, None] == (SM_DT + head_of_col)[None, :]).astype(BF16)
    emat = jnp.concatenate([emat1, emat1], axis=0)
    w_branch_b = w_branch.astype(BF16)
    w_out_b = w_out.astype(BF16)
    w_up_b = w_up.astype(BF16)
    w_down_b = w_down.astype(BF16)
    norm_mix3, norm_ffn3 = row3(norm_mix), row3(norm_ffn)
    gla_ba3, gla_norm3 = row3(gla_ba), row3(gla_norm)
    m_cb3, m_norm3 = row3(mlstm_conv_b), row3(mlstm_norm)
    s_cwx, s_cwbc = ssd_conv_w[:, :, :d], ssd_conv_w[:, :, d:]
    s_cbx, s_cbbc = row3(ssd_conv_b[:, :d]), row3(ssd_conv_b[:, d:])
    s_norm3, dexp3, gate_b3, f_cb3 = row3(ssd_norm), row3(dexp), row3(gate_b), row3(ffn_conv_b)

    tm_proj = _pick(m, (1024, 512, 256, 128, 64))
    tb = _pick(t, (256, 128, 64))
    tm_merge = _pick(m, (512, 256, 128, 64))
    tm_ffn = _pick(t, (512, 256, 128, 64))
    tf = d_ff // 2

    h = x.reshape(m, d)
    for l in range(depth):
        proj2, small2 = _proj(h, norm_mix3, w_perm, l, tm_proj, 2048)
        proj3 = proj2.reshape(bsz, t, _N_PROJ)
        small3 = small2.reshape(bsz, t, SM_W)
        y_gla = _gla(proj3, small3, wa3, gla_ba3, gla_norm3, l, tb)
        y_m = _mlstm(proj3, small3, mlstm_conv_w, m_cb3, row3(m_bias), col3(m_bias), m_norm3, l, tb)
        y_s = _ssd(proj3, small3, s_cwx, s_cbx, s_cwbc, s_cbbc, row3(s_bias), col3(s_bias), row3(alog),
                   col3(alog), dexp3, s_norm3, emat, l, tb)
        h = _merge(y_gla.reshape(m, d), y_m.reshape(m, d), y_s.reshape(m, d), proj2, gate_b3,
                   w_branch_b, w_out_b, h, l, tm_merge)
        h = _ffn(h.reshape(bsz, t, d), norm_ffn3, w_up_b, ffn_conv_w, f_cb3, w_down_b, l, tm_ffn, tf).reshape(m, d)
    out = _rmsnorm(h, norm_final[None, :], _pick(m, (512, 256, 128, 64)))
    return out.reshape(bsz, t, d)
```

```python
import functools

import jax
import jax.numpy as jnp
from jax import lax
from jax.experimental import pallas as pl
from jax.experimental.pallas import tpu as pltpu

F32 = jnp.float32
BF16 = jnp.bfloat16

EPS = 1e-6
CHUNK = 64
GLA_HEADS = 4
GLA_TAU = 16.0
MLSTM_HEADS = 4
SSD_HEADS = 16
SSD_GROUPS = 2
SSD_STATE = 128
SSD_HEAD_DIM = 64
FFN_CONV = 3
CONV_K = 4
PAD = 16

SM_GA = 0
SM_MI = 16
SM_MF = 20
SM_DT = 24
SM_W = 128

VMEM_LIMIT = 52 * 1024 * 1024


def _dot(a, b):
    return jnp.dot(a.astype(BF16), b.astype(BF16), preferred_element_type=F32)


def _dot_nt(a, b):
    return lax.dot_general(a.astype(BF16), b.astype(BF16), (((1,), (1,)), ((), ())),
                           preferred_element_type=F32)


def _dot_tn(a, b):
    return jnp.dot(a.astype(F32).T.astype(BF16), b.astype(BF16), preferred_element_type=F32)


def _split2(x):
    hi = x.astype(BF16)
    return hi, (x - hi.astype(F32)).astype(BF16)


def _cumsum_rows(tril2, x):
    return jnp.dot(tril2, jnp.concatenate(_split2(x), axis=0), preferred_element_type=F32)


def _cumsum_cols(x, triu2):
    return jnp.dot(jnp.concatenate(_split2(x), axis=1), triu2, preferred_element_type=F32)


def _sigmoid(x):
    return 1.0 / (1.0 + jnp.exp(-x))


def _silu(x):
    return x * _sigmoid(x)


def _softplus(x):
    return jnp.maximum(x, 0.0) + jnp.log1p(jnp.exp(-jnp.abs(x)))


def _log_sigmoid(x):
    return jnp.minimum(x, 0.0) - jnp.log1p(jnp.exp(-jnp.abs(x)))


def _causal():
    row = lax.broadcasted_iota(jnp.int32, (CHUNK, CHUNK), 0)
    col = lax.broadcasted_iota(jnp.int32, (CHUNK, CHUNK), 1)
    return col <= row


def _tril2():
    row = lax.broadcasted_iota(jnp.int32, (CHUNK, 2 * CHUNK), 0)
    col = lax.broadcasted_iota(jnp.int32, (CHUNK, 2 * CHUNK), 1)
    return jnp.where((col <= row) | ((col >= CHUNK) & (col - CHUNK <= row)), 1.0, 0.0).astype(BF16)


def _triu2():
    row = lax.broadcasted_iota(jnp.int32, (2 * CHUNK, CHUNK), 0)
    col = lax.broadcasted_iota(jnp.int32, (2 * CHUNK, CHUNK), 1)
    return jnp.where((row <= col) | ((row >= CHUNK) & (row - CHUNK <= col)), 1.0, 0.0).astype(BF16)


def _shift_mat(taps):
    row = lax.broadcasted_iota(jnp.int32, (taps * CHUNK, PAD + CHUNK), 0)
    col = lax.broadcasted_iota(jnp.int32, (taps * CHUNK, PAD + CHUNK), 1)
    sel = None
    for j in range(taps):
        hit = (row >= j * CHUNK) & (row < (j + 1) * CHUNK) & (col == row - j * CHUNK + PAD - j)
        sel = hit if sel is None else sel | hit
    return jnp.where(sel, 1.0, 0.0).astype(BF16)


def _layer_spec(arr, l):
    zeros = (0,) * (arr.ndim - 1)
    return pl.BlockSpec((None,) + arr.shape[1:], lambda *_: (l,) + zeros)


def _history_spec(tb, width, col_block):
    per = tb // PAD
    return pl.BlockSpec((1, PAD, width), lambda b, i: (b, jnp.maximum(i * per - 1, 0), col_block))


def _conv_window(x_ref, hist_ref, c):
    if c > 0:
        return x_ref[0, c * CHUNK - PAD:(c + 1) * CHUNK, :]
    hist = hist_ref[0]
    hist = jnp.where(pl.program_id(1) > 0, hist, jnp.zeros_like(hist))
    return jnp.concatenate([hist, x_ref[0, 0:CHUNK, :]], axis=0)


def _proj_kernel(x_ref, g_ref, w_ref, o_ref, sm_ref, xn_ref, *, sm_tile, sm_col):
    j = pl.program_id(1)

    @pl.when(j == 0)
    def _():
        x = x_ref[...]
        ms = jnp.mean(x * x, axis=-1, keepdims=True)
        xn_ref[...] = (x * lax.rsqrt(ms + EPS) * g_ref[...]).astype(BF16)

    acc = jnp.dot(xn_ref[...], w_ref[...], preferred_element_type=F32)
    o_ref[...] = acc.astype(o_ref.dtype)

    @pl.when(j == sm_tile)
    def _():
        sm_ref[...] = acc[:, sm_col:sm_col + SM_W]


def _proj(x, g, w, l, tm, tn):
    m, d = x.shape
    n = w.shape[2]
    return pl.pallas_call(
        functools.partial(_proj_kernel, sm_tile=_OFF_SMALL // tn, sm_col=_OFF_SMALL % tn),
        grid=(m // tm, n // tn),
        in_specs=[
            pl.BlockSpec((tm, d), lambda i, j: (i, 0)),
            _layer_spec(g, l),
            pl.BlockSpec((None, d, tn), lambda i, j: (l, 0, j)),
        ],
        out_specs=[
            pl.BlockSpec((tm, tn), lambda i, j: (i, j)),
            pl.BlockSpec((tm, SM_W), lambda i, j: (i, 0)),
        ],
        out_shape=[jax.ShapeDtypeStruct((m, n), BF16), jax.ShapeDtypeStruct((m, SM_W), F32)],
        scratch_shapes=[pltpu.VMEM((tm, d), BF16)],
        compiler_params=pltpu.CompilerParams(
            dimension_semantics=("parallel", "arbitrary"), vmem_limit_bytes=VMEM_LIMIT),
        name="in_proj",
    )(x, g, w)


def _gla_kernel(q_ref, k_ref, v_ref, gg_ref, sm_ref, wa_ref, ba_ref, nw_ref, y_ref, st_ref, *, tb):
    dk = q_ref.shape[2] // GLA_HEADS
    dv = v_ref.shape[2] // GLA_HEADS

    @pl.when(pl.program_id(1) == 0)
    def _():
        st_ref[...] = jnp.zeros_like(st_ref)

    chunks = range(tb // CHUNK)
    heads = range(GLA_HEADS)
    causal = _causal()
    tril2 = _tril2()
    scale = dk ** -0.5
    rows = [slice(c * CHUNK, (c + 1) * CHUNK) for c in chunks]

    pre = []
    for c in chunks:
        hi, lo = _split2(sm_ref[0, rows[c], :])
        pre.append(jnp.dot(jnp.concatenate([hi, lo, hi], axis=1), wa_ref[...], preferred_element_type=F32))
    la = [_log_sigmoid(pre[c] + ba_ref[...]) * (1.0 / GLA_TAU) for c in chunks]
    bcum = [_cumsum_rows(tril2, la[c]) for c in chunks]

    for c in chunks:
        r = rows[c]
        kcs = [slice(h * dk, (h + 1) * dk) for h in heads]
        vcs = [slice(h * dv, (h + 1) * dv) for h in heads]
        b = [bcum[c][:, kcs[h]] for h in heads]
        bl = [b[h][CHUNK - 1:CHUNK, :] for h in heads]
        q = [q_ref[0, r, kcs[h]].astype(F32) * scale for h in heads]
        k = [k_ref[0, r, kcs[h]].astype(F32) for h in heads]
        v = [v_ref[0, r, vcs[h]] for h in heads]
        qd = [q[h] * jnp.exp(b[h]) for h in heads]
        ki = [k[h] * jnp.exp(-b[h]) for h in heads]
        kd = [k[h] * jnp.exp(bl[h] - b[h]) for h in heads]
        st = [st_ref[h] for h in heads]
        sc = [_dot_nt(qd[h], ki[h]) for h in heads]
        o2 = [_dot_nt(qd[h], st[h]) for h in heads]
        ds = [_dot_tn(v[h], kd[h]) for h in heads]
        o1 = [_dot(jnp.where(causal, sc[h], 0.0), v[h]) for h in heads]
        for h in heads:
            st_ref[h] = st[h] * jnp.exp(bl[h]) + ds[h]
            o = o1[h] + o2[h]
            ms = jnp.mean(o * o, axis=-1, keepdims=True)
            y = (o * lax.rsqrt(ms + EPS) * nw_ref[:, vcs[h]]) * _silu(gg_ref[0, r, vcs[h]].astype(F32))
            y_ref[0, r, vcs[h]] = y.astype(y_ref.dtype)


def _gla(proj, small, wa3, ba, nw, l, tb):
    bsz, t, _ = proj.shape
    qk_w = ba.shape[2]
    v_w = nw.shape[2]
    dk = qk_w // GLA_HEADS
    dv = v_w // GLA_HEADS
    return pl.pallas_call(
        functools.partial(_gla_kernel, tb=tb),
        grid=(bsz, t // tb),
        in_specs=[
            pl.BlockSpec((1, tb, qk_w), lambda b, i: (b, i, 0)),
            pl.BlockSpec((1, tb, qk_w), lambda b, i: (b, i, 1)),
            pl.BlockSpec((1, tb, v_w), lambda b, i: (b, i, 1)),
            pl.BlockSpec((1, tb, v_w), lambda b, i: (b, i, 2)),
            pl.BlockSpec((1, tb, SM_W), lambda b, i: (b, i, 0)),
            _layer_spec(wa3, l),
            _layer_spec(ba, l),
            _layer_spec(nw, l),
        ],
        out_specs=pl.BlockSpec((1, tb, v_w), lambda b, i: (b, i, 0)),
        out_shape=jax.ShapeDtypeStruct((bsz, t, v_w), BF16),
        scratch_shapes=[pltpu.VMEM((GLA_HEADS, dv, dk), F32)],
        compiler_params=pltpu.CompilerParams(
            dimension_semantics=("parallel", "arbitrary"), vmem_limit_bytes=VMEM_LIMIT),
        name="gla",
    )(proj, proj, proj, proj, small, wa3, ba, nw)


def _mlstm_kernel(qk_ref, qkh_ref, v_ref, og_ref, sm_ref, cw_ref, cb_ref, brow_ref, bcol_ref, nw_ref, y_ref,
                  c_ref, n_ref, m_ref, *, tb):
    qk_w = qk_ref.shape[2] // 2
    dk = qk_w // MLSTM_HEADS
    dv = v_ref.shape[2] // MLSTM_HEADS

    @pl.when(pl.program_id(1) == 0)
    def _():
        c_ref[...] = jnp.zeros_like(c_ref)
        n_ref[...] = jnp.zeros_like(n_ref)
        m_ref[...] = jnp.zeros_like(m_ref)

    chunks = range(tb // CHUNK)
    heads = range(MLSTM_HEADS)
    pairs = [(c, h) for c in chunks for h in heads]
    causal = _causal()
    tril2 = _tril2()
    triu2 = _triu2()
    shift = _shift_mat(CONV_K)
    kscale = dk ** -0.5
    neg_inf = -jnp.inf
    rows = [slice(c * CHUNK, (c + 1) * CHUNK) for c in chunks]

    sm = [sm_ref[0, rows[c], :] for c in chunks]
    pre_col = [sm[c] + brow_ref[...] for c in chunks]
    pre_row = [(sm[c].T + bcol_ref[...])[SM_MI:SM_MI + 2 * MLSTM_HEADS] for c in chunks]
    b_col = [_cumsum_rows(tril2, _log_sigmoid(pre_col[c])) for c in chunks]
    b_row = [_cumsum_cols(_log_sigmoid(pre_row[c]), triu2) for c in chunks]
    xs = [jnp.dot(shift, _conv_window(qk_ref, qkh_ref, c), preferred_element_type=F32) for c in chunks]
    act = []
    for c in chunks:
        acc = cb_ref[...]
        for k in range(CONV_K):
            j = CONV_K - 1 - k
            acc = acc + cw_ref[k:k + 1, :] * xs[c][j * CHUNK:(j + 1) * CHUNK]
        act.append(_silu(acc))

    bc, br, ir, bl, m_loc, w = {}, {}, {}, {}, {}, {}
    for c, h in pairs:
        bc[c, h] = b_col[c][:, SM_MF + h:SM_MF + h + 1]
        br[c, h] = b_row[c][MLSTM_HEADS + h:MLSTM_HEADS + h + 1, :]
        ir[c, h] = pre_row[c][h:h + 1, :]
        bl[c, h] = bc[c, h][CHUNK - 1:CHUNK, :]
        a = bl[c, h] - bc[c, h] + pre_col[c][:, SM_MI + h:SM_MI + h + 1]
        m_loc[c, h] = jnp.max(a, axis=0, keepdims=True)
        w[c, h] = jnp.exp(a - m_loc[c, h])
    m_prev, s_old, s_new = {}, {}, {}
    for h in heads:
        m = m_ref[h][0:1, 0:1]
        for c in chunks:
            m_prev[c, h] = m
            m_new = jnp.maximum(bl[c, h] + m, m_loc[c, h])
            s_old[c, h] = jnp.exp(bl[c, h] + m - m_new)
            s_new[c, h] = jnp.exp(m_loc[c, h] - m_new)
            m = m_new
        m_ref[h] = jnp.broadcast_to(m, m_ref.shape[1:])

    q = {(c, h): act[c][:, h * dk:(h + 1) * dk] for c, h in pairs}
    k = {(c, h): act[c][:, qk_w + h * dk:qk_w + (h + 1) * dk] * kscale for c, h in pairs}
    v = {(c, h): v_ref[0, rows[c], h * dv:(h + 1) * dv] for c, h in pairs}
    wk = {p: w[p] * k[p] for p in pairs}

    qk = {p: _dot_nt(q[p], k[p]) for p in pairs}
    dcs = {p: _dot_tn(wk[p], v[p]) for p in pairs}

    c_prev, n_prev = {}, {}
    for h in heads:
        cst = c_ref[h]
        nst = n_ref[h][0:1, :]
        for c in chunks:
            c_prev[c, h] = cst.astype(BF16)
            n_prev[c, h] = nst
            cst = s_old[c, h] * cst + s_new[c, h] * dcs[c, h]
            nst = s_old[c, h] * nst + s_new[c, h] * jnp.sum(wk[c, h], axis=0, keepdims=True)
        c_ref[h] = cst
        n_ref[h] = jnp.broadcast_to(nst, n_ref.shape[1:])

    log_d = {p: jnp.where(causal, bc[p] - br[p] + ir[p], neg_inf) for p in pairs}
    m_inter = {p: bc[p] + m_prev[p] for p in pairs}
    row_max = {p: jnp.max(log_d[p], axis=-1, keepdims=True) for p in pairs}
    m_t = {p: jnp.maximum(m_inter[p], row_max[p]) for p in pairs}
    wts = {p: qk[p] * jnp.exp(log_d[p] - m_t[p]) for p in pairs}
    s_inter = {p: jnp.exp(m_inter[p] - m_t[p]) for p in pairs}

    pv = {p: _dot(wts[p], v[p]) for p in pairs}
    qc = {p: _dot(q[p], c_prev[p]) for p in pairs}

    row_sum = {p: jnp.sum(wts[p], axis=-1, keepdims=True) for p in pairs}
    qn = {p: jnp.sum(q[p] * n_prev[p], axis=-1, keepdims=True) for p in pairs}
    num = {p: pv[p] + s_inter[p] * qc[p] for p in pairs}
    den = {p: row_sum[p] + s_inter[p] * qn[p] for p in pairs}
    hout = {p: num[p] / jnp.maximum(jnp.abs(den[p]), jnp.exp(-m_t[p])) for p in pairs}
    ms = {p: jnp.mean(hout[p] * hout[p], axis=-1, keepdims=True) for p in pairs}
    for c, h in pairs:
        p = (c, h)
        vc = slice(h * dv, (h + 1) * dv)
        y = (hout[p] * lax.rsqrt(ms[p] + EPS) * nw_ref[:, vc]) * _sigmoid(og_ref[0, rows[c], vc].astype(F32))
        y_ref[0, rows[c], vc] = y.astype(y_ref.dtype)


def _mlstm(proj, small, cw, cb, brow, bcol, nw, l, tb):
    bsz, t, _ = proj.shape
    qk2 = cw.shape[2]
    v_w = nw.shape[2]
    dk = qk2 // 2 // MLSTM_HEADS
    dv = v_w // MLSTM_HEADS
    base = _OFF_MLSTM // v_w
    return pl.pallas_call(
        functools.partial(_mlstm_kernel, tb=tb),
        grid=(bsz, t // tb),
        in_specs=[
            pl.BlockSpec((1, tb, qk2), lambda b, i: (b, i, base)),
            _history_spec(tb, qk2, base),
            pl.BlockSpec((1, tb, v_w), lambda b, i: (b, i, base + 1)),
            pl.BlockSpec((1, tb, v_w), lambda b, i: (b, i, base + 2)),
            pl.BlockSpec((1, tb, SM_W), lambda b, i: (b, i, 0)),
            _layer_spec(cw, l),
            _layer_spec(cb, l),
            _layer_spec(brow, l),
            _layer_spec(bcol, l),
            _layer_spec(nw, l),
        ],
        out_specs=pl.BlockSpec((1, tb, v_w), lambda b, i: (b, i, 0)),
        out_shape=jax.ShapeDtypeStruct((bsz, t, v_w), BF16),
        scratch_shapes=[
            pltpu.VMEM((MLSTM_HEADS, dk, dv), F32),
            pltpu.VMEM((MLSTM_HEADS, 8, dk), F32),
            pltpu.VMEM((MLSTM_HEADS, 8, 128), F32),
        ],
        compiler_params=pltpu.CompilerParams(
            dimension_semantics=("parallel", "arbitrary"), vmem_limit_bytes=VMEM_LIMIT),
        name="mlstm",
    )(proj, proj, proj, proj, small, cw, cb, brow, bcol, nw)


def _ssd_kernel(z_ref, x_ref, xh_ref, bc_ref, bch_ref, sm_ref, cwx_ref, cbx_ref, cwbc_ref, cbbc_ref, brow_ref,
                bcol_ref, alog_row_ref, alog_col_ref, dexp_ref, nw_ref, e_ref, y_ref,
                s_ref, ysc_ref, *, tb):
    inner = x_ref.shape[2]
    gw = inner // SSD_GROUPS
    hpg = SSD_HEADS // SSD_GROUPS
    p = SSD_HEAD_DIM
    z = SSD_STATE

    @pl.when(pl.program_id(1) == 0)
    def _():
        s_ref[...] = jnp.zeros_like(s_ref)

    chunks = range(tb // CHUNK)
    groups = range(SSD_GROUPS)
    causal = _causal()
    tril2 = _tril2()
    triu2 = _triu2()
    shift = _shift_mat(CONV_K)
    neg_inf = -jnp.inf
    a_row = -jnp.exp(alog_row_ref[...])
    a_col = -jnp.exp(alog_col_ref[...])[SM_DT:SM_DT + SSD_HEADS]
    rows = [slice(c * CHUNK, (c + 1) * CHUNK) for c in chunks]

    def conv(xs, cw, cb):
        acc = cb
        for k in range(CONV_K):
            j = CONV_K - 1 - k
            acc = acc + cw[k:k + 1, :] * xs[j * CHUNK:(j + 1) * CHUNK]
        return _silu(acc)

    sm = [sm_ref[0, rows[c], :] for c in chunks]
    dt_col = [_softplus(sm[c] + brow_ref[...]) for c in chunks]
    dt_row = [_softplus((sm[c].T + bcol_ref[...])[SM_DT:SM_DT + SSD_HEADS]) for c in chunks]
    b_col = [_cumsum_rows(tril2, dt_col[c] * a_row) for c in chunks]
    b_row = [_cumsum_cols(dt_row[c] * a_col, triu2) for c in chunks]
    xsx = [jnp.dot(shift, _conv_window(x_ref, xh_ref, c), preferred_element_type=F32) for c in chunks]
    xsb = [jnp.dot(shift, _conv_window(bc_ref, bch_ref, c), preferred_element_type=F32) for c in chunks]
    expd = []
    for c in chunks:
        bl_row = b_col[c][CHUNK - 1:CHUNK, :]
        w_col = jnp.exp(bl_row - b_col[c]) * dt_col[c]
        eb_col = jnp.exp(b_col[c])
        ebl = jnp.broadcast_to(jnp.exp(bl_row), (8, SM_W))
        hi, lo = _split2(jnp.concatenate([w_col, eb_col, ebl], axis=0))
        expd.append(jnp.dot(jnp.concatenate([hi, lo], axis=1), e_ref[...], preferred_element_type=F32))
    xa = [conv(xsx[c], cwx_ref[...], cbx_ref[...]) for c in chunks]
    bca = [conv(xsb[c], cwbc_ref[...], cbbc_ref[...]) for c in chunks]

    for c in chunks:
        w_x = expd[c][0:CHUNK]
        eb_x = expd[c][CHUNK:2 * CHUNK]
        ebl_x = expd[c][2 * CHUNK:2 * CHUNK + 1]
        xg = [xa[c][:, g * gw:(g + 1) * gw] for g in groups]
        bg = [bca[c][:, g * z:(g + 1) * z] for g in groups]
        cg = [bca[c][:, SSD_GROUPS * z + g * z:SSD_GROUPS * z + (g + 1) * z] for g in groups]
        cb = [_dot_nt(cg[g], bg[g]) for g in groups]
        s_prev = [s_ref[g] for g in groups]
        inter = [_dot(cg[g], s_prev[g]) for g in groups]
        dst = [_dot_tn(bg[g], xg[g] * w_x[:, g * gw:(g + 1) * gw]) for g in groups]
        mixes = []
        for h in range(SSD_HEADS):
            bc = b_col[c][:, SM_DT + h:SM_DT + h + 1]
            br = b_row[c][h:h + 1, :]
            dtr = dt_row[c][h:h + 1, :]
            mixes.append(cb[h // hpg] * jnp.exp(jnp.where(causal, bc - br, neg_inf)) * dtr)
        ys = [_dot(mixes[h], xg[h // hpg][:, (h % hpg) * p:(h % hpg + 1) * p]) for h in range(SSD_HEADS)]
        for h in range(SSD_HEADS):
            ysc_ref[:, h * p:(h + 1) * p] = ys[h]
        for g in groups:
            gc = slice(g * gw, (g + 1) * gw)
            s_ref[g] = s_prev[g] * ebl_x[:, gc] + dst[g]
            y = ysc_ref[:, gc] + inter[g] * eb_x[:, gc] + dexp_ref[:, gc] * xg[g]
            y = y * _silu(z_ref[0, rows[c], gc].astype(F32))
            ms = jnp.mean(y * y, axis=-1, keepdims=True)
            y_ref[0, rows[c], gc] = (y * lax.rsqrt(ms + EPS) * nw_ref[:, gc]).astype(y_ref.dtype)


def _ssd(proj, small, cwx, cbx, cwbc, cbbc, brow, bcol, alog_row, alog_col, dexp, nw, emat, l, tb):
    bsz, t, _ = proj.shape
    inner = nw.shape[2]
    bcw = cwbc.shape[2]
    return pl.pallas_call(
        functools.partial(_ssd_kernel, tb=tb),
        grid=(bsz, t // tb),
        in_specs=[
            pl.BlockSpec((1, tb, inner), lambda b, i: (b, i, _OFF_SZ // inner)),
            pl.BlockSpec((1, tb, inner), lambda b, i: (b, i, _OFF_SX // inner)),
            _history_spec(tb, inner, _OFF_SX // inner),
            pl.BlockSpec((1, tb, bcw), lambda b, i: (b, i, _OFF_SBC // bcw)),
            _history_spec(tb, bcw, _OFF_SBC // bcw),
            pl.BlockSpec((1, tb, SM_W), lambda b, i: (b, i, 0)),
            _layer_spec(cwx, l),
            _layer_spec(cbx, l),
            _layer_spec(cwbc, l),
            _layer_spec(cbbc, l),
            _layer_spec(brow, l),
            _layer_spec(bcol, l),
            _layer_spec(alog_row, l),
            _layer_spec(alog_col, l),
            _layer_spec(dexp, l),
            _layer_spec(nw, l),
            pl.BlockSpec(emat.shape, lambda b, i: (0, 0)),
        ],
        out_specs=pl.BlockSpec((1, tb, inner), lambda b, i: (b, i, 0)),
        out_shape=jax.ShapeDtypeStruct((bsz, t, inner), BF16),
        scratch_shapes=[
            pltpu.VMEM((SSD_GROUPS, SSD_STATE, inner // SSD_GROUPS), F32),
            pltpu.VMEM((CHUNK, inner), F32),
        ],
        compiler_params=pltpu.CompilerParams(
            dimension_semantics=("parallel", "arbitrary"), vmem_limit_bytes=VMEM_LIMIT),
        name="ssd",
    )(proj, proj, proj, proj, proj, small, cwx, cbx, cwbc, cbbc, brow, bcol, alog_row, alog_col, dexp, nw, emat)


def _merge_kernel(ya_ref, yb_ref, yc_ref, gates_ref, gb_ref, wb_ref, wo_ref, h_ref, o_ref):
    d = h_ref.shape[1]
    acc = None
    for k, y_ref in enumerate((ya_ref, yb_ref, yc_ref)):
        zk = jnp.dot(y_ref[...], wb_ref[k], preferred_element_type=F32)
        gk = _sigmoid(gates_ref[:, k * d:(k + 1) * d].astype(F32) + gb_ref[:, k * d:(k + 1) * d])
        acc = gk * zk if acc is None else acc + gk * zk
    o_ref[...] = h_ref[...] + jnp.dot(acc.astype(BF16), wo_ref[...], preferred_element_type=F32)


def _merge(ya, yb, yc, proj2, gate_b, w_branch, w_out, h, l, tm):
    m, d = h.shape
    nb = w_branch.shape[1]
    return pl.pallas_call(
        _merge_kernel,
        grid=(m // tm,),
        in_specs=[
            pl.BlockSpec((tm, d), lambda i: (i, 0)),
            pl.BlockSpec((tm, d), lambda i: (i, 0)),
            pl.BlockSpec((tm, d), lambda i: (i, 0)),
            pl.BlockSpec((tm, nb * d), lambda i: (i, _OFF_GATES // (nb * d))),
            _layer_spec(gate_b, l),
            _layer_spec(w_branch, l),
            _layer_spec(w_out, l),
            pl.BlockSpec((tm, d), lambda i: (i, 0)),
        ],
        out_specs=pl.BlockSpec((tm, d), lambda i: (i, 0)),
        out_shape=jax.ShapeDtypeStruct((m, d), F32),
        compiler_params=pltpu.CompilerParams(
            dimension_semantics=("parallel",), vmem_limit_bytes=VMEM_LIMIT),
        name="merge",
    )(ya, yb, yc, proj2, gate_b, w_branch, w_out, h)


def _ffn_kernel(h_ref, g_ref, wg_ref, wv_ref, cwg_ref, cwv_ref, cbg_ref, cbv_ref, wd_ref, o_ref,
                xn_ref, gpad_ref, vpad_ref, acc_ref, *, tm):
    t = pl.program_id(1)
    f = pl.program_id(2)
    nf = pl.num_programs(2)

    @pl.when(f == 0)
    def _():
        x = h_ref[0]
        ms = jnp.mean(x * x, axis=-1, keepdims=True)
        xn_ref[...] = (x * lax.rsqrt(ms + EPS) * g_ref[...]).astype(BF16)

    @pl.when(t == 0)
    def _():
        gpad_ref[f, 0:8, :] = jnp.zeros((8, gpad_ref.shape[2]), F32)
        vpad_ref[f, 0:8, :] = jnp.zeros((8, vpad_ref.shape[2]), F32)

    xn = xn_ref[...]
    gpad_ref[f, 8:8 + tm, :] = jnp.dot(xn, wg_ref[...], preferred_element_type=F32)
    vpad_ref[f, 8:8 + tm, :] = jnp.dot(xn, wv_ref[...], preferred_element_type=F32)
    gate = cbg_ref[...]
    val = cbv_ref[...]
    for k in range(FFN_CONV):
        s = FFN_CONV - 1 - k
        gate = gate + cwg_ref[k:k + 1, :] * gpad_ref[f, 8 - s:8 - s + tm, :]
        val = val + cwv_ref[k:k + 1, :] * vpad_ref[f, 8 - s:8 - s + tm, :]
    gpad_ref[f, 0:8, :] = gpad_ref[f, tm:tm + 8, :]
    vpad_ref[f, 0:8, :] = vpad_ref[f, tm:tm + 8, :]
    act = (_silu(gate) * val).astype(BF16)
    part = jnp.dot(act, wd_ref[...], preferred_element_type=F32)

    @pl.when(f == 0)
    def _():
        acc_ref[...] = part

    @pl.when(f > 0)
    def _():
        acc_ref[...] += part

    @pl.when(f == nf - 1)
    def _():
        o_ref[0] = h_ref[0] + acc_ref[...]


def _ffn(h3, g, w_up, cw, cb, w_down, l, tm, tf):
    bsz, t, d = h3.shape
    dff = w_down.shape[1]
    nf = dff // tf
    return pl.pallas_call(
        functools.partial(_ffn_kernel, tm=tm),
        grid=(bsz, t // tm, nf),
        in_specs=[
            pl.BlockSpec((1, tm, d), lambda b, i, f: (b, i, 0)),
            _layer_spec(g, l),
            pl.BlockSpec((None, d, tf), lambda b, i, f: (l, 0, f)),
            pl.BlockSpec((None, d, tf), lambda b, i, f: (l, 0, nf + f)),
            pl.BlockSpec((None, FFN_CONV, tf), lambda b, i, f: (l, 0, f)),
            pl.BlockSpec((None, FFN_CONV, tf), lambda b, i, f: (l, 0, nf + f)),
            pl.BlockSpec((None, 1, tf), lambda b, i, f: (l, 0, f)),
            pl.BlockSpec((None, 1, tf), lambda b, i, f: (l, 0, nf + f)),
            pl.BlockSpec((None, tf, d), lambda b, i, f: (l, f, 0)),
        ],
        out_specs=pl.BlockSpec((1, tm, d), lambda b, i, f: (b, i, 0)),
        out_shape=jax.ShapeDtypeStruct((bsz, t, d), F32),
        scratch_shapes=[
            pltpu.VMEM((tm, d), BF16),
            pltpu.VMEM((nf, tm + 8, tf), F32),
            pltpu.VMEM((nf, tm + 8, tf), F32),
            pltpu.VMEM((tm, d), F32),
        ],
        compiler_params=pltpu.CompilerParams(
            dimension_semantics=("parallel", "arbitrary", "arbitrary"), vmem_limit_bytes=VMEM_LIMIT),
        name="conv_ffn",
    )(h3, g, w_up, w_up, cw, cw, cb, cb, w_down)


def _rmsnorm_kernel(x_ref, g_ref, o_ref):
    x = x_ref[...]
    ms = jnp.mean(x * x, axis=-1, keepdims=True)
    o_ref[...] = x * lax.rsqrt(ms + EPS) * g_ref[...]


def _rmsnorm(x, g, tm):
    m, d = x.shape
    return pl.pallas_call(
        _rmsnorm_kernel,
        grid=(m // tm,),
        in_specs=[pl.BlockSpec((tm, d), lambda i: (i, 0)), pl.BlockSpec((1, d), lambda i: (0, 0))],
        out_specs=pl.BlockSpec((tm, d), lambda i: (i, 0)),
        out_shape=jax.ShapeDtypeStruct((m, d), F32),
        compiler_params=pltpu.CompilerParams(dimension_semantics=("parallel",)),
        name="final_rmsnorm",
    )(x, g)


_D = 1024
_OFF_MLSTM = 3 * _D
_OFF_GATES = 6 * _D
_OFF_SZ = 9 * _D
_OFF_SX = 10 * _D
_OFF_SBC = 11 * _D
_OFF_SMALL = 11 * _D + _D // 2
_N_PROJ = 12 * _D


def _pick(n, cands):
    for c in cands:
        if n % c == 0:
            return c
    return n


def kernel(x, norm_mix, w_in, gla_wa, gla_ba, gla_norm, mlstm_conv_w, mlstm_conv_b, mlstm_bi, mlstm_bf,
           mlstm_norm, ssd_conv_w, ssd_conv_b, ssd_dt_bias, ssd_a_log, ssd_d, ssd_norm, gate_b, w_branch,
           w_out, norm_ffn, w_up, ffn_conv_w, ffn_conv_b, w_down, norm_final):
    bsz, t, d = x.shape
    assert d == _D
    depth = w_in.shape[0]
    m = bsz * t
    d_ff = w_down.shape[1]
    qk = d // 2
    rank = gla_wa.shape[1]
    bc_w = 2 * SSD_GROUPS * SSD_STATE
    sizes = (qk, qk, d, rank, d, 2 * qk, d, MLSTM_HEADS, MLSTM_HEADS, d, d, d + bc_w, SSD_HEADS, 3 * d)
    offs = [0]
    for s in sizes:
        offs.append(offs[-1] + s)
    (o_gq, o_gk, o_gv, o_ga, o_gg, o_mqk, o_mv, o_mi, o_mf, o_mo, o_sz, o_sxbc, o_sdt, o_gates, o_end) = offs
    assert o_end == w_in.shape[2]

    def cols(a, lo, hi):
        return a[..., lo:hi]

    w_perm = jnp.concatenate([
        cols(w_in, o_gq, o_ga),
        cols(w_in, o_gg, o_mi),
        cols(w_in, o_mo, o_sz),
        cols(w_in, o_gates, o_end),
        cols(w_in, o_sz, o_sdt),
        cols(w_in, o_ga, o_gg),
        cols(w_in, o_mi, o_mo),
        cols(w_in, o_sdt, o_gates),
        jnp.zeros((depth, d, _N_PROJ - _OFF_SMALL - rank - 2 * MLSTM_HEADS - SSD_HEADS), w_in.dtype),
    ], axis=-1).astype(BF16)
    assert w_perm.shape[-1] == _N_PROJ

    row3 = lambda a: a[:, None, :]
    col3 = lambda a: a[:, :, None]
    pad_small = lambda a, off: jnp.pad(a, ((0, 0), (off, SM_W - off - a.shape[1])))
    wa_pad = jnp.pad(gla_wa, ((0, 0), (SM_GA, SM_W - SM_GA - rank), (0, 0)))
    wa_hi = wa_pad.astype(BF16)
    wa_lo = (wa_pad - wa_hi.astype(F32)).astype(BF16)
    wa3 = jnp.concatenate([wa_hi, wa_hi, wa_lo], axis=1)
    m_bias = pad_small(mlstm_bi, SM_MI) + pad_small(mlstm_bf, SM_MF)
    s_bias = pad_small(ssd_dt_bias, SM_DT)
    alog = pad_small(ssd_a_log, SM_DT)
    dexp = jnp.repeat(ssd_d, SSD_HEAD_DIM, axis=-1)
    head_of_col = jnp.arange(d) // SSD_HEAD_DIM
    emat1 = (jnp.arange(SM_W)[:, None] == (SM_DT + head_of_col)[None, :]).astype(BF16)
    emat = jnp.concatenate([emat1, emat1], axis=0)
    w_branch_b = w_branch.astype(BF16)
    w_out_b = w_out.astype(BF16)
    w_up_b = w_up.astype(BF16)
    w_down_b = w_down.astype(BF16)
    norm_mix3, norm_ffn3 = row3(norm_mix), row3(norm_ffn)
    gla_ba3, gla_norm3 = row3(gla_ba), row3(gla_norm)
    m_cb3, m_norm3 = row3(mlstm_conv_b), row3(mlstm_norm)
    s_cwx, s_cwbc = ssd_conv_w[:, :, :d], ssd_conv_w[:, :, d:]
    s_cbx, s_cbbc = row3(ssd_conv_b[:, :d]), row3(ssd_conv_b[:, d:])
    s_norm3, dexp3, gate_b3, f_cb3 = row3(ssd_norm), row3(dexp), row3(gate_b), row3(ffn_conv_b)

    tm_proj = _pick(m, (1024, 512, 256, 128, 64))
    tb = _pick(t, (256, 128, 64))
    tm_merge = _pick(m, (512, 256, 128, 64))
    tm_ffn = _pick(t, (512, 256, 128, 64))
    tf = d_ff // 2

    h = x.reshape(m, d)
    for l in range(depth):
        proj2, small2 = _proj(h, norm_mix3, w_perm, l, tm_proj, 2048)
        proj3 = proj2.reshape(bsz, t, _N_PROJ)
        small3 = small2.reshape(bsz, t, SM_W)
        y_gla = _gla(proj3, small3, wa3, gla_ba3, gla_norm3, l, tb)
        y_m = _mlstm(proj3, small3, mlstm_conv_w, m_cb3, row3(m_bias), col3(m_bias), m_norm3, l, tb)
        y_s = _ssd(proj3, small3, s_cwx, s_cbx, s_cwbc, s_cbbc, row3(s_bias), col3(s_bias), row3(alog),
                   col3(alog), dexp3, s_norm3, emat, l, tb)
        h = _merge(y_gla.reshape(m, d), y_m.reshape(m, d), y_s.reshape(m, d), proj2, gate_b3,
                   w_branch_b, w_out_b, h, l, tm_merge)
        h = _ffn(h.reshape(bsz, t, d), norm_ffn3, w_up_b, ffn_conv_w, f_cb3, w_down_b, l, tm_ffn, tf).reshape(m, d)
    out = _rmsnorm(h, norm_final[None, :], _pick(m, (512, 256, 128, 64)))
    return out.reshape(bsz, t, d)
```

```python
import functools

import jax
import jax.numpy as jnp
from jax import lax
from jax.experimental import pallas as pl
from jax.experimental.pallas import tpu as pltpu

F32 = jnp.float32
BF16 = jnp.bfloat16

EPS = 1e-6
CHUNK = 64
GLA_HEADS = 4
GLA_TAU = 16.0
MLSTM_HEADS = 4
SSD_HEADS = 16
SSD_GROUPS = 2
SSD_STATE = 128
SSD_HEAD_DIM = 64
FFN_CONV = 3
CONV_K = 4
PAD = 16

SM_GA = 0
SM_MI = 16
SM_MF = 20
SM_DT = 24
SM_W = 128

VMEM_LIMIT = 52 * 1024 * 1024


def _dot(a, b):
    return jnp.dot(a.astype(BF16), b.astype(BF16), preferred_element_type=F32)


def _dot_nt(a, b):
    return lax.dot_general(a.astype(BF16), b.astype(BF16), (((1,), (1,)), ((), ())),
                           preferred_element_type=F32)


def _dot_tn(a, b):
    return jnp.dot(a.astype(F32).T.astype(BF16), b.astype(BF16), preferred_element_type=F32)


def _split2(x):
    hi = x.astype(BF16)
    return hi, (x - hi.astype(F32)).astype(BF16)


def _cumsum_rows(tril2, x):
    return jnp.dot(tril2, jnp.concatenate(_split2(x), axis=0), preferred_element_type=F32)


def _cumsum_cols(x, triu2):
    return jnp.dot(jnp.concatenate(_split2(x), axis=1), triu2, preferred_element_type=F32)


def _sigmoid(x):
    return 1.0 / (1.0 + jnp.exp(-x))


def _silu(x):
    return x * _sigmoid(x)


def _softplus(x):
    return jnp.maximum(x, 0.0) + jnp.log1p(jnp.exp(-jnp.abs(x)))


def _log_sigmoid(x):
    return jnp.minimum(x, 0.0) - jnp.log1p(jnp.exp(-jnp.abs(x)))


def _causal():
    row = lax.broadcasted_iota(jnp.int32, (CHUNK, CHUNK), 0)
    col = lax.broadcasted_iota(jnp.int32, (CHUNK, CHUNK), 1)
    return col <= row


def _tril2():
    row = lax.broadcasted_iota(jnp.int32, (CHUNK, 2 * CHUNK), 0)
    col = lax.broadcasted_iota(jnp.int32, (CHUNK, 2 * CHUNK), 1)
    return jnp.where((col <= row) | ((col >= CHUNK) & (col - CHUNK <= row)), 1.0, 0.0).astype(BF16)


def _triu2():
    row = lax.broadcasted_iota(jnp.int32, (2 * CHUNK, CHUNK), 0)
    col = lax.broadcasted_iota(jnp.int32, (2 * CHUNK, CHUNK), 1)
    return jnp.where((row <= col) | ((row >= CHUNK) & (row - CHUNK <= col)), 1.0, 0.0).astype(BF16)


def _shift_mat(taps):
    row = lax.broadcasted_iota(jnp.int32, (taps * CHUNK, PAD + CHUNK), 0)
    col = lax.broadcasted_iota(jnp.int32, (taps * CHUNK, PAD + CHUNK), 1)
    sel = None
    for j in range(taps):
        hit = (row >= j * CHUNK) & (row < (j + 1) * CHUNK) & (col == row - j * CHUNK + PAD - j)
        sel = hit if sel is None else sel | hit
    return jnp.where(sel, 1.0, 0.0).astype(BF16)


def _layer_spec(arr, l):
    zeros = (0,) * (arr.ndim - 1)
    return pl.BlockSpec((None,) + arr.shape[1:], lambda *_: (l,) + zeros)


def _history_spec(tb, width, col_block):
    per = tb // PAD
    return pl.BlockSpec((1, PAD, width), lambda b, i: (b, jnp.maximum(i * per - 1, 0), col_block))


def _conv_window(x_ref, hist_ref, c):
    if c > 0:
        return x_ref[0, c * CHUNK - PAD:(c + 1) * CHUNK, :]
    hist = hist_ref[0]
    hist = jnp.where(pl.program_id(1) > 0, hist, jnp.zeros_like(hist))
    return jnp.concatenate([hist, x_ref[0, 0:CHUNK, :]], axis=0)


def _proj_kernel(x_ref, g_ref, w_ref, o_ref, sm_ref, xn_ref, *, sm_tile, sm_col):
    j = pl.program_id(1)

    @pl.when(j == 0)
    def _():
        x = x_ref[...]
        ms = jnp.mean(x * x, axis=-1, keepdims=True)
        xn_ref[...] = (x * lax.rsqrt(ms + EPS) * g_ref[...]).astype(BF16)

    acc = jnp.dot(xn_ref[...], w_ref[...], preferred_element_type=F32)
    o_ref[...] = acc.astype(o_ref.dtype)

    @pl.when(j == sm_tile)
    def _():
        sm_ref[...] = acc[:, sm_col:sm_col + SM_W]


def _proj(x, g, w, l, tm, tn):
    m, d = x.shape
    n = w.shape[2]
    return pl.pallas_call(
        functools.partial(_proj_kernel, sm_tile=_OFF_SMALL // tn, sm_col=_OFF_SMALL % tn),
        grid=(m // tm, n // tn),
        in_specs=[
            pl.BlockSpec((tm, d), lambda i, j: (i, 0)),
            _layer_spec(g, l),
            pl.BlockSpec((None, d, tn), lambda i, j: (l, 0, j)),
        ],
        out_specs=[
            pl.BlockSpec((tm, tn), lambda i, j: (i, j)),
            pl.BlockSpec((tm, SM_W), lambda i, j: (i, 0)),
        ],
        out_shape=[jax.ShapeDtypeStruct((m, n), BF16), jax.ShapeDtypeStruct((m, SM_W), F32)],
        scratch_shapes=[pltpu.VMEM((tm, d), BF16)],
        compiler_params=pltpu.CompilerParams(
            dimension_semantics=("parallel", "arbitrary"), vmem_limit_bytes=VMEM_LIMIT),
        name="in_proj",
    )(x, g, w)


def _gla_kernel(q_ref, k_ref, v_ref, gg_ref, sm_ref, wa_ref, ba_ref, nw_ref, y_ref, st_ref, *, tb):
    dk = q_ref.shape[2] // GLA_HEADS
    dv = v_ref.shape[2] // GLA_HEADS

    @pl.when(pl.program_id(1) == 0)
    def _():
        st_ref[...] = jnp.zeros_like(st_ref)

    chunks = range(tb // CHUNK)
    heads = range(GLA_HEADS)
    causal = _causal()
    tril2 = _tril2()
    scale = dk ** -0.5
    rows = [slice(c * CHUNK, (c + 1) * CHUNK) for c in chunks]

    pre = []
    for c in chunks:
        hi, lo = _split2(sm_ref[0, rows[c], :])
        pre.append(jnp.dot(jnp.concatenate([hi, lo, hi], axis=1), wa_ref[...], preferred_element_type=F32))
    la = [_log_sigmoid(pre[c] + ba_ref[...]) * (1.0 / GLA_TAU) for c in chunks]
    bcum = [_cumsum_rows(tril2, la[c]) for c in chunks]

    for c in chunks:
        r = rows[c]
        kcs = [slice(h * dk, (h + 1) * dk) for h in heads]
        vcs = [slice(h * dv, (h + 1) * dv) for h in heads]
        b = [bcum[c][:, kcs[h]] for h in heads]
        bl = [b[h][CHUNK - 1:CHUNK, :] for h in heads]
        q = [q_ref[0, r, kcs[h]].astype(F32) * scale for h in heads]
        k = [k_ref[0, r, kcs[h]].astype(F32) for h in heads]
        v = [v_ref[0, r, vcs[h]] for h in heads]
        qd = [q[h] * jnp.exp(b[h]) for h in heads]
        ki = [k[h] * jnp.exp(-b[h]) for h in heads]
        kd = [k[h] * jnp.exp(bl[h] - b[h]) for h in heads]
        st = [st_ref[h] for h in heads]
        sc = [_dot_nt(qd[h], ki[h]) for h in heads]
        o2 = [_dot_nt(qd[h], st[h]) for h in heads]
        ds = [_dot_tn(v[h], kd[h]) for h in heads]
        o1 = [_dot(jnp.where(causal, sc[h], 0.0), v[h]) for h in heads]
        for h in heads:
            st_ref[h] = st[h] * jnp.exp(bl[h]) + ds[h]
            o = o1[h] + o2[h]
            ms = jnp.mean(o * o, axis=-1, keepdims=True)
            y = (o * lax.rsqrt(ms + EPS) * nw_ref[:, vcs[h]]) * _silu(gg_ref[0, r, vcs[h]].astype(F32))
            y_ref[0, r, vcs[h]] = y.astype(y_ref.dtype)


def _gla(proj, small, wa3, ba, nw, l, tb):
    bsz, t, _ = proj.shape
    qk_w = ba.shape[2]
    v_w = nw.shape[2]
    dk = qk_w // GLA_HEADS
    dv = v_w // GLA_HEADS
    return pl.pallas_call(
        functools.partial(_gla_kernel, tb=tb),
        grid=(bsz, t // tb),
        in_specs=[
            pl.BlockSpec((1, tb, qk_w), lambda b, i: (b, i, 0)),
            pl.BlockSpec((1, tb, qk_w), lambda b, i: (b, i, 1)),
            pl.BlockSpec((1, tb, v_w), lambda b, i: (b, i, 1)),
            pl.BlockSpec((1, tb, v_w), lambda b, i: (b, i, 2)),
            pl.BlockSpec((1, tb, SM_W), lambda b, i: (b, i, 0)),
            _layer_spec(wa3, l),
            _layer_spec(ba, l),
            _layer_spec(nw, l),
        ],
        out_specs=pl.BlockSpec((1, tb, v_w), lambda b, i: (b, i, 0)),
        out_shape=jax.ShapeDtypeStruct((bsz, t, v_w), BF16),
        scratch_shapes=[pltpu.VMEM((GLA_HEADS, dv, dk), F32)],
        compiler_params=pltpu.CompilerParams(
            dimension_semantics=("parallel", "arbitrary"), vmem_limit_bytes=VMEM_LIMIT),
        name="gla",
    )(proj, proj, proj, proj, small, wa3, ba, nw)


def _mlstm_kernel(qk_ref, qkh_ref, v_ref, og_ref, sm_ref, cw_ref, cb_ref, brow_ref, bcol_ref, nw_ref, y_ref,
                  c_ref, n_ref, m_ref, *, tb):
    qk_w = qk_ref.shape[2] // 2
    dk = qk_w // MLSTM_HEADS
    dv = v_ref.shape[2] // MLSTM_HEADS

    @pl.when(pl.program_id(1) == 0)
    def _():
        c_ref[...] = jnp.zeros_like(c_ref)
        n_ref[...] = jnp.zeros_like(n_ref)
        m_ref[...] = jnp.zeros_like(m_ref)

    chunks = range(tb // CHUNK)
    heads = range(MLSTM_HEADS)
    pairs = [(c, h) for c in chunks for h in heads]
    causal = _causal()
    tril2 = _tril2()
    triu2 = _triu2()
    shift = _shift_mat(CONV_K)
    kscale = dk ** -0.5
    neg_inf = -jnp.inf
    rows = [slice(c * CHUNK, (c + 1) * CHUNK) for c in chunks]

    sm = [sm_ref[0, rows[c], :] for c in chunks]
    pre_col = [sm[c] + brow_ref[...] for c in chunks]
    pre_row = [(sm[c].T + bcol_ref[...])[SM_MI:SM_MI + 2 * MLSTM_HEADS] for c in chunks]
    b_col = [_cumsum_rows(tril2, _log_sigmoid(pre_col[c])) for c in chunks]
    b_row = [_cumsum_cols(_log_sigmoid(pre_row[c]), triu2) for c in chunks]
    xs = [jnp.dot(shift, _conv_window(qk_ref, qkh_ref, c), preferred_element_type=F32) for c in chunks]
    act = []
    for c in chunks:
        acc = cb_ref[...]
        for k in range(CONV_K):
            j = CONV_K - 1 - k
            acc = acc + cw_ref[k:k + 1, :] * xs[c][j * CHUNK:(j + 1) * CHUNK]
        act.append(_silu(acc))

    bc, br, ir, bl, m_loc, w = {}, {}, {}, {}, {}, {}
    for c, h in pairs:
        bc[c, h] = b_col[c][:, SM_MF + h:SM_MF + h + 1]
        br[c, h] = b_row[c][MLSTM_HEADS + h:MLSTM_HEADS + h + 1, :]
        ir[c, h] = pre_row[c][h:h + 1, :]
        bl[c, h] = bc[c, h][CHUNK - 1:CHUNK, :]
        a = bl[c, h] - bc[c, h] + pre_col[c][:, SM_MI + h:SM_MI + h + 1]
        m_loc[c, h] = jnp.max(a, axis=0, keepdims=True)
        w[c, h] = jnp.exp(a - m_loc[c, h])
    m_prev, s_old, s_new = {}, {}, {}
    for h in heads:
        m = m_ref[h][0:1, 0:1]
        for c in chunks:
            m_prev[c, h] = m
            m_new = jnp.maximum(bl[c, h] + m, m_loc[c, h])
            s_old[c, h] = jnp.exp(bl[c, h] + m - m_new)
            s_new[c, h] = jnp.exp(m_loc[c, h] - m_new)
            m = m_new
        m_ref[h] = jnp.broadcast_to(m, m_ref.shape[1:])

    q = {(c, h): act[c][:, h * dk:(h + 1) * dk] for c, h in pairs}
    k = {(c, h): act[c][:, qk_w + h * dk:qk_w + (h + 1) * dk] * kscale for c, h in pairs}
    v = {(c, h): v_ref[0, rows[c], h * dv:(h + 1) * dv] for c, h in pairs}
    wk = {p: w[p] * k[p] for p in pairs}

    qk = {p: _dot_nt(q[p], k[p]) for p in pairs}
    dcs = {p: _dot_tn(wk[p], v[p]) for p in pairs}

    c_prev, n_prev = {}, {}
    for h in heads:
        cst = c_ref[h]
        nst = n_ref[h][0:1, :]
        for c in chunks:
            c_prev[c, h] = cst.astype(BF16)
            n_prev[c, h] = nst
            cst = s_old[c, h] * cst + s_new[c, h] * dcs[c, h]
            nst = s_old[c, h] * nst + s_new[c, h] * jnp.sum(wk[c, h], axis=0, keepdims=True)
        c_ref[h] = cst
        n_ref[h] = jnp.broadcast_to(nst, n_ref.shape[1:])

    log_d = {p: jnp.where(causal, bc[p] - br[p] + ir[p], neg_inf) for p in pairs}
    m_inter = {p: bc[p] + m_prev[p] for p in pairs}
    row_max = {p: jnp.max(log_d[p], axis=-1, keepdims=True) for p in pairs}
    m_t = {p: jnp.maximum(m_inter[p], row_max[p]) for p in pairs}
    wts = {p: qk[p] * jnp.exp(log_d[p] - m_t[p]) for p in pairs}
    s_inter = {p: jnp.exp(m_inter[p] - m_t[p]) for p in pairs}

    pv = {p: _dot(wts[p], v[p]) for p in pairs}
    qc = {p: _dot(q[p], c_prev[p]) for p in pairs}

    row_sum = {p: jnp.sum(wts[p], axis=-1, keepdims=True) for p in pairs}
    qn = {p: jnp.sum(q[p] * n_prev[p], axis=-1, keepdims=True) for p in pairs}
    num = {p: pv[p] + s_inter[p] * qc[p] for p in pairs}
    den = {p: row_sum[p] + s_inter[p] * qn[p] for p in pairs}
    hout = {p: num[p] / jnp.maximum(jnp.abs(den[p]), jnp.exp(-m_t[p])) for p in pairs}
    ms = {p: jnp.mean(hout[p] * hout[p], axis=-1, keepdims=True) for p in pairs}
    for c, h in pairs:
        p = (c, h)
        vc = slice(h * dv, (h + 1) * dv)
        y = (hout[p] * lax.rsqrt(ms[p] + EPS) * nw_ref[:, vc]) * _sigmoid(og_ref[0, rows[c], vc].astype(F32))
        y_ref[0, rows[c], vc] = y.astype(y_ref.dtype)


def _mlstm(proj, small, cw, cb, brow, bcol, nw, l, tb):
    bsz, t, _ = proj.shape
    qk2 = cw.shape[2]
    v_w = nw.shape[2]
    dk = qk2 // 2 // MLSTM_HEADS
    dv = v_w // MLSTM_HEADS
    base = _OFF_MLSTM // v_w
    return pl.pallas_call(
        functools.partial(_mlstm_kernel, tb=tb),
        grid=(bsz, t // tb),
        in_specs=[
            pl.BlockSpec((1, tb, qk2), lambda b, i: (b, i, base)),
            _history_spec(tb, qk2, base),
            pl.BlockSpec((1, tb, v_w), lambda b, i: (b, i, base + 1)),
            pl.BlockSpec((1, tb, v_w), lambda b, i: (b, i, base + 2)),
            pl.BlockSpec((1, tb, SM_W), lambda b, i: (b, i, 0)),
            _layer_spec(cw, l),
            _layer_spec(cb, l),
            _layer_spec(brow, l),
            _layer_spec(bcol, l),
            _layer_spec(nw, l),
        ],
        out_specs=pl.BlockSpec((1, tb, v_w), lambda b, i: (b, i, 0)),
        out_shape=jax.ShapeDtypeStruct((bsz, t, v_w), BF16),
        scratch_shapes=[
            pltpu.VMEM((MLSTM_HEADS, dk, dv), F32),
            pltpu.VMEM((MLSTM_HEADS, 8, dk), F32),
            pltpu.VMEM((MLSTM_HEADS, 8, 128), F32),
        ],
        compiler_params=pltpu.CompilerParams(
            dimension_semantics=("parallel", "arbitrary"), vmem_limit_bytes=VMEM_LIMIT),
        name="mlstm",
    )(proj, proj, proj, proj, small, cw, cb, brow, bcol, nw)


def _ssd_kernel(z_ref, x_ref, xh_ref, bc_ref, bch_ref, sm_ref, cwx_ref, cbx_ref, cwbc_ref, cbbc_ref, brow_ref,
                bcol_ref, alog_row_ref, alog_col_ref, dexp_ref, nw_ref, e_ref, y_ref,
                s_ref, ysc_ref, *, tb):
    inner = x_ref.shape[2]
    gw = inner // SSD_GROUPS
    hpg = SSD_HEADS // SSD_GROUPS
    p = SSD_HEAD_DIM
    z = SSD_STATE

    @pl.when(pl.program_id(1) == 0)
    def _():
        s_ref[...] = jnp.zeros_like(s_ref)

    chunks = range(tb // CHUNK)
    groups = range(SSD_GROUPS)
    causal = _causal()
    tril2 = _tril2()
    triu2 = _triu2()
    shift = _shift_mat(CONV_K)
    neg_inf = -jnp.inf
    a_row = -jnp.exp(alog_row_ref[...])
    a_col = -jnp.exp(alog_col_ref[...])[SM_DT:SM_DT + SSD_HEADS]
    rows = [slice(c * CHUNK, (c + 1) * CHUNK) for c in chunks]

    def conv(xs, cw, cb):
        acc = cb
        for k in range(CONV_K):
            j = CONV_K - 1 - k
            acc = acc + cw[k:k + 1, :] * xs[j * CHUNK:(j + 1) * CHUNK]
        return _silu(acc)

    sm = [sm_ref[0, rows[c], :] for c in chunks]
    dt_col = [_softplus(sm[c] + brow_ref[...]) for c in chunks]
    dt_row = [_softplus((sm[c].T + bcol_ref[...])[SM_DT:SM_DT + SSD_HEADS]) for c in chunks]
    b_col = [_cumsum_rows(tril2, dt_col[c] * a_row) for c in chunks]
    b_row = [_cumsum_cols(dt_row[c] * a_col, triu2) for c in chunks]
    xsx = [jnp.dot(shift, _conv_window(x_ref, xh_ref, c), preferred_element_type=F32) for c in chunks]
    xsb = [jnp.dot(shift, _conv_window(bc_ref, bch_ref, c), preferred_element_type=F32) for c in chunks]
    expd = []
    for c in chunks:
        bl_row = b_col[c][CHUNK - 1:CHUNK, :]
        w_col = jnp.exp(bl_row - b_col[c]) * dt_col[c]
        eb_col = jnp.exp(b_col[c])
        ebl = jnp.broadcast_to(jnp.exp(bl_row), (8, SM_W))
        hi, lo = _split2(jnp.concatenate([w_col, eb_col, ebl], axis=0))
        expd.append(jnp.dot(jnp.concatenate([hi, lo], axis=1), e_ref[...], preferred_element_type=F32))
    xa = [conv(xsx[c], cwx_ref[...], cbx_ref[...]) for c in chunks]
    bca = [conv(xsb[c], cwbc_ref[...], cbbc_ref[...]) for c in chunks]

    for c in chunks:
        w_x = expd[c][0:CHUNK]
        eb_x = expd[c][CHUNK:2 * CHUNK]
        ebl_x = expd[c][2 * CHUNK:2 * CHUNK + 1]
        xg = [xa[c][:, g * gw:(g + 1) * gw] for g in groups]
        bg = [bca[c][:, g * z:(g + 1) * z] for g in groups]
        cg = [bca[c][:, SSD_GROUPS * z + g * z:SSD_GROUPS * z + (g + 1) * z] for g in groups]
        cb = [_dot_nt(cg[g], bg[g]) for g in groups]
        s_prev = [s_ref[g] for g in groups]
        inter = [_dot(cg[g], s_prev[g]) for g in groups]
        dst = [_dot_tn(bg[g], xg[g] * w_x[:, g * gw:(g + 1) * gw]) for g in groups]
        mixes = []
        for h in range(SSD_HEADS):
            bc = b_col[c][:, SM_DT + h:SM_DT + h + 1]
            br = b_row[c][h:h + 1, :]
            dtr = dt_row[c][h:h + 1, :]
            mixes.append(cb[h // hpg] * jnp.exp(jnp.where(causal, bc - br, neg_inf)) * dtr)
        ys = [_dot(mixes[h], xg[h // hpg][:, (h % hpg) * p:(h % hpg + 1) * p]) for h in range(SSD_HEADS)]
        for h in range(SSD_HEADS):
            ysc_ref[:, h * p:(h + 1) * p] = ys[h]
        for g in groups:
            gc = slice(g * gw, (g + 1) * gw)
            s_ref[g] = s_prev[g] * ebl_x[:, gc] + dst[g]
            y = ysc_ref[:, gc] + inter[g] * eb_x[:, gc] + dexp_ref[:, gc] * xg[g]
            y = y * _silu(z_ref[0, rows[c], gc].astype(F32))
            ms = jnp.mean(y * y, axis=-1, keepdims=True)
            y_ref[0, rows[c], gc] = (y * lax.rsqrt(ms + EPS) * nw_ref[:, gc]).astype(y_ref.dtype)


def _ssd(proj, small, cwx, cbx, cwbc, cbbc, brow, bcol, alog_row, alog_col, dexp, nw, emat, l, tb):
    bsz, t, _ = proj.shape
    inner = nw.shape[2]
    bcw = cwbc.shape[2]
    return pl.pallas_call(
        functools.partial(_ssd_kernel, tb=tb),
        grid=(bsz, t // tb),
        in_specs=[
            pl.BlockSpec((1, tb, inner), lambda b, i: (b, i, _OFF_SZ // inner)),
            pl.BlockSpec((1, tb, inner), lambda b, i: (b, i, _OFF_SX // inner)),
            _history_spec(tb, inner, _OFF_SX // inner),
            pl.BlockSpec((1, tb, bcw), lambda b, i: (b, i, _OFF_SBC // bcw)),
            _history_spec(tb, bcw, _OFF_SBC // bcw),
            pl.BlockSpec((1, tb, SM_W), lambda b, i: (b, i, 0)),
            _layer_spec(cwx, l),
            _layer_spec(cbx, l),
            _layer_spec(cwbc, l),
            _layer_spec(cbbc, l),
            _layer_spec(brow, l),
            _layer_spec(bcol, l),
            _layer_spec(alog_row, l),
            _layer_spec(alog_col, l),
            _layer_spec(dexp, l),
            _layer_spec(nw, l),
            pl.BlockSpec(emat.shape, lambda b, i: (0, 0)),
        ],
        out_specs=pl.BlockSpec((1, tb, inner), lambda b, i: (b, i, 0)),
        out_shape=jax.ShapeDtypeStruct((bsz, t, inner), BF16),
        scratch_shapes=[
            pltpu.VMEM((SSD_GROUPS, SSD_STATE, inner // SSD_GROUPS), F32),
            pltpu.VMEM((CHUNK, inner), F32),
        ],
        compiler_params=pltpu.CompilerParams(
            dimension_semantics=("parallel", "arbitrary"), vmem_limit_bytes=VMEM_LIMIT),
        name="ssd",
    )(proj, proj, proj, proj, proj, small, cwx, cbx, cwbc, cbbc, brow, bcol, alog_row, alog_col, dexp, nw, emat)


def _merge_kernel(ya_ref, yb_ref, yc_ref, gates_ref, gb_ref, wb_ref, wo_ref, h_ref, o_ref):
    d = h_ref.shape[1]
    acc = None
    for k, y_ref in enumerate((ya_ref, yb_ref, yc_ref)):
        zk = jnp.dot(y_ref[...], wb_ref[k], preferred_element_type=F32)
        gk = _sigmoid(gates_ref[:, k * d:(k + 1) * d].astype(F32) + gb_ref[:, k * d:(k + 1) * d])
        acc = gk * zk if acc is None else acc + gk * zk
    o_ref[...] = h_ref[...] + jnp.dot(acc.astype(BF16), wo_ref[...], preferred_element_type=F32)


def _merge(ya, yb, yc, proj2, gate_b, w_branch, w_out, h, l, tm):
    m, d = h.shape
    nb = w_branch.shape[1]
    return pl.pallas_call(
        _merge_kernel,
        grid=(m // tm,),
        in_specs=[
            pl.BlockSpec((tm, d), lambda i: (i, 0)),
            pl.BlockSpec((tm, d), lambda i: (i, 0)),
            pl.BlockSpec((tm, d), lambda i: (i, 0)),
            pl.BlockSpec((tm, nb * d), lambda i: (i, _OFF_GATES // (nb * d))),
            _layer_spec(gate_b, l),
            _layer_spec(w_branch, l),
            _layer_spec(w_out, l),
            pl.BlockSpec((tm, d), lambda i: (i, 0)),
        ],
        out_specs=pl.BlockSpec((tm, d), lambda i: (i, 0)),
        out_shape=jax.ShapeDtypeStruct((m, d), F32),
        compiler_params=pltpu.CompilerParams(
            dimension_semantics=("parallel",), vmem_limit_bytes=VMEM_LIMIT),
        name="merge",
    )(ya, yb, yc, proj2, gate_b, w_branch, w_out, h)


def _ffn_kernel(h_ref, g_ref, wg_ref, wv_ref, cwg_ref, cwv_ref, cbg_ref, cbv_ref, wd_ref, gout_ref, o_ref,
                xn_ref, gpad_ref, vpad_ref, acc_ref, *, tm, norm_out):
    t = pl.program_id(1)
    f = pl.program_id(2)
    nf = pl.num_programs(2)

    @pl.when(f == 0)
    def _():
        x = h_ref[0]
        ms = jnp.mean(x * x, axis=-1, keepdims=True)
        xn_ref[...] = (x * lax.rsqrt(ms + EPS) * g_ref[...]).astype(BF16)

    @pl.when(t == 0)
    def _():
        gpad_ref[f, 0:8, :] = jnp.zeros((8, gpad_ref.shape[2]), F32)
        vpad_ref[f, 0:8, :] = jnp.zeros((8, vpad_ref.shape[2]), F32)

    xn = xn_ref[...]
    gpad_ref[f, 8:8 + tm, :] = jnp.dot(xn, wg_ref[...], preferred_element_type=F32)
    vpad_ref[f, 8:8 + tm, :] = jnp.dot(xn, wv_ref[...], preferred_element_type=F32)
    gate = cbg_ref[...]
    val = cbv_ref[...]
    for k in range(FFN_CONV):
        s = FFN_CONV - 1 - k
        gate = gate + cwg_ref[k:k + 1, :] * gpad_ref[f, 8 - s:8 - s + tm, :]
        val = val + cwv_ref[k:k + 1, :] * vpad_ref[f, 8 - s:8 - s + tm, :]
    gpad_ref[f, 0:8, :] = gpad_ref[f, tm:tm + 8, :]
    vpad_ref[f, 0:8, :] = vpad_ref[f, tm:tm + 8, :]
    act = (_silu(gate) * val).astype(BF16)
    part = jnp.dot(act, wd_ref[...], preferred_element_type=F32)

    @pl.when(f == 0)
    def _():
        acc_ref[...] = part

    @pl.when(f > 0)
    def _():
        acc_ref[...] += part

    @pl.when(f == nf - 1)
    def _():
        r = h_ref[0] + acc_ref[...]
        if norm_out:
            ms = jnp.mean(r * r, axis=-1, keepdims=True)
            r = r * lax.rsqrt(ms + EPS) * gout_ref[...]
        o_ref[0] = r


def _ffn(h3, g, w_up, cw, cb, w_down, g_out, l, tm, tf, norm_out):
    bsz, t, d = h3.shape
    dff = w_down.shape[1]
    nf = dff // tf
    return pl.pallas_call(
        functools.partial(_ffn_kernel, tm=tm, norm_out=norm_out),
        grid=(bsz, t // tm, nf),
        in_specs=[
            pl.BlockSpec((1, tm, d), lambda b, i, f: (b, i, 0)),
            _layer_spec(g, l),
            pl.BlockSpec((None, d, tf), lambda b, i, f: (l, 0, f)),
            pl.BlockSpec((None, d, tf), lambda b, i, f: (l, 0, nf + f)),
            pl.BlockSpec((None, FFN_CONV, tf), lambda b, i, f: (l, 0, f)),
            pl.BlockSpec((None, FFN_CONV, tf), lambda b, i, f: (l, 0, nf + f)),
            pl.BlockSpec((None, 1, tf), lambda b, i, f: (l, 0, f)),
            pl.BlockSpec((None, 1, tf), lambda b, i, f: (l, 0, nf + f)),
            pl.BlockSpec((None, tf, d), lambda b, i, f: (l, f, 0)),
            pl.BlockSpec((1, d), lambda b, i, f: (0, 0)),
        ],
        out_specs=pl.BlockSpec((1, tm, d), lambda b, i, f: (b, i, 0)),
        out_shape=jax.ShapeDtypeStruct((bsz, t, d), F32),
        scratch_shapes=[
            pltpu.VMEM((tm, d), BF16),
            pltpu.VMEM((nf, tm + 8, tf), F32),
            pltpu.VMEM((nf, tm + 8, tf), F32),
            pltpu.VMEM((tm, d), F32),
        ],
        compiler_params=pltpu.CompilerParams(
            dimension_semantics=("parallel", "arbitrary", "arbitrary"), vmem_limit_bytes=VMEM_LIMIT),
        name="conv_ffn",
    )(h3, g, w_up, w_up, cw, cw, cb, cb, w_down, g_out)


_D = 1024
_OFF_MLSTM = 3 * _D
_OFF_GATES = 6 * _D
_OFF_SZ = 9 * _D
_OFF_SX = 10 * _D
_OFF_SBC = 11 * _D
_OFF_SMALL = 11 * _D + _D // 2
_N_PROJ = 12 * _D


def _pick(n, cands):
    for c in cands:
        if n % c == 0:
            return c
    return n


def kernel(x, norm_mix, w_in, gla_wa, gla_ba, gla_norm, mlstm_conv_w, mlstm_conv_b, mlstm_bi, mlstm_bf,
           mlstm_norm, ssd_conv_w, ssd_conv_b, ssd_dt_bias, ssd_a_log, ssd_d, ssd_norm, gate_b, w_branch,
           w_out, norm_ffn, w_up, ffn_conv_w, ffn_conv_b, w_down, norm_final):
    bsz, t, d = x.shape
    assert d == _D
    depth = w_in.shape[0]
    m = bsz * t
    d_ff = w_down.shape[1]
    qk = d // 2
    rank = gla_wa.shape[1]
    bc_w = 2 * SSD_GROUPS * SSD_STATE
    sizes = (qk, qk, d, rank, d, 2 * qk, d, MLSTM_HEADS, MLSTM_HEADS, d, d, d + bc_w, SSD_HEADS, 3 * d)
    offs = [0]
    for s in sizes:
        offs.append(offs[-1] + s)
    (o_gq, o_gk, o_gv, o_ga, o_gg, o_mqk, o_mv, o_mi, o_mf, o_mo, o_sz, o_sxbc, o_sdt, o_gates, o_end) = offs
    assert o_end == w_in.shape[2]

    def cols(a, lo, hi):
        return a[..., lo:hi]

    w_perm = jnp.concatenate([
        cols(w_in, o_gq, o_ga),
        cols(w_in, o_gg, o_mi),
        cols(w_in, o_mo, o_sz),
        cols(w_in, o_gates, o_end),
        cols(w_in, o_sz, o_sdt),
        cols(w_in, o_ga, o_gg),
        cols(w_in, o_mi, o_mo),
        cols(w_in, o_sdt, o_gates),
        jnp.zeros((depth, d, _N_PROJ - _OFF_SMALL - rank - 2 * MLSTM_HEADS - SSD_HEADS), w_in.dtype),
    ], axis=-1).astype(BF16)
    assert w_perm.shape[-1] == _N_PROJ

    row3 = lambda a: a[:, None, :]
    col3 = lambda a: a[:, :, None]
    pad_small = lambda a, off: jnp.pad(a, ((0, 0), (off, SM_W - off - a.shape[1])))
    wa_pad = jnp.pad(gla_wa, ((0, 0), (SM_GA, SM_W - SM_GA - rank), (0, 0)))
    wa_hi = wa_pad.astype(BF16)
    wa_lo = (wa_pad - wa_hi.astype(F32)).astype(BF16)
    wa3 = jnp.concatenate([wa_hi, wa_hi, wa_lo], axis=1)
    m_bias = pad_small(mlstm_bi, SM_MI) + pad_small(mlstm_bf, SM_MF)
    s_bias = pad_small(ssd_dt_bias, SM_DT)
    alog = pad_small(ssd_a_log, SM_DT)
    dexp = jnp.repeat(ssd_d, SSD_HEAD_DIM, axis=-1)
    head_of_col = jnp.arange(d) // SSD_HEAD_DIM
    emat1 = (jnp.arange(SM_W)[:, None] == (SM_DT + head_of_col)[None, :]).astype(BF16)
    emat = jnp.concatenate([emat1, emat1], axis=0)
    w_branch_b = w_branch.astype(BF16)
    w_out_b = w_out.astype(BF16)
    w_up_b = w_up.astype(BF16)
    w_down_b = w_down.astype(BF16)
    norm_mix3, norm_ffn3 = row3(norm_mix), row3(norm_ffn)
    gla_ba3, gla_norm3 = row3(gla_ba), row3(gla_norm)
    m_cb3, m_norm3 = row3(mlstm_conv_b), row3(mlstm_norm)
    s_cwx, s_cwbc = ssd_conv_w[:, :, :d], ssd_conv_w[:, :, d:]
    s_cbx, s_cbbc = row3(ssd_conv_b[:, :d]), row3(ssd_conv_b[:, d:])
    s_norm3, dexp3, gate_b3, f_cb3 = row3(ssd_norm), row3(dexp), row3(gate_b), row3(ffn_conv_b)

    tm_proj = _pick(m, (1024, 512, 256, 128, 64))
    tn_proj = _pick(_N_PROJ, (3072, 2048, 1024))
    tb = _pick(t, (512, 256, 128, 64))
    tm_merge = _pick(m, (512, 256, 128, 64))
    tm_ffn = _pick(t, (512, 256, 128, 64))
    tf = d_ff // 2

    h = x.reshape(m, d)
    for l in range(depth):
        proj2, small2 = _proj(h, norm_mix3, w_perm, l, tm_proj, tn_proj)
        proj3 = proj2.reshape(bsz, t, _N_PROJ)
        small3 = small2.reshape(bsz, t, SM_W)
        y_gla = _gla(proj3, small3, wa3, gla_ba3, gla_norm3, l, tb)
        y_m = _mlstm(proj3, small3, mlstm_conv_w, m_cb3, row3(m_bias), col3(m_bias), m_norm3, l, tb)
        y_s = _ssd(proj3, small3, s_cwx, s_cbx, s_cwbc, s_cbbc, row3(s_bias), col3(s_bias), row3(alog),
                   col3(alog), dexp3, s_norm3, emat, l, tb)
        h = _merge(y_gla.reshape(m, d), y_m.reshape(m, d), y_s.reshape(m, d), proj2, gate_b3,
                   w_branch_b, w_out_b, h, l, tm_merge)
        h = _ffn(h.reshape(bsz, t, d), norm_ffn3, w_up_b, ffn_conv_w, f_cb3, w_down_b, norm_final[None, :], l,
                 tm_ffn, tf, norm_out=(l == depth - 1)).reshape(m, d)
    return h.reshape(bsz, t, d)
```

```python
import functools

import jax
import jax.numpy as jnp
from jax import lax
from jax.experimental import pallas as pl
from jax.experimental.pallas import tpu as pltpu

F32 = jnp.float32
BF16 = jnp.bfloat16

EPS = 1e-6
CHUNK = 64
GLA_HEADS = 4
GLA_TAU = 16.0
MLSTM_HEADS = 4
SSD_HEADS = 16
SSD_GROUPS = 2
SSD_STATE = 128
SSD_HEAD_DIM = 64
FFN_CONV = 3
CONV_K = 4
PAD = 16

SM_GA = 0
SM_MI = 16
SM_MF = 20
SM_DT = 24
SM_W = 128

VMEM_LIMIT = 52 * 1024 * 1024


def _dot(a, b):
    return jnp.dot(a.astype(BF16), b.astype(BF16), preferred_element_type=F32)


def _dot_nt(a, b):
    return lax.dot_general(a.astype(BF16), b.astype(BF16), (((1,), (1,)), ((), ())),
                           preferred_element_type=F32)


def _dot_tn(a, b):
    return jnp.dot(a.astype(F32).T.astype(BF16), b.astype(BF16), preferred_element_type=F32)


def _split2(x):
    hi = x.astype(BF16)
    return hi, (x - hi.astype(F32)).astype(BF16)


def _cumsum_rows(tril2, x):
    return jnp.dot(tril2, jnp.concatenate(_split2(x), axis=0), preferred_element_type=F32)


def _cumsum_cols(x, triu2):
    return jnp.dot(jnp.concatenate(_split2(x), axis=1), triu2, preferred_element_type=F32)


def _sigmoid(x):
    return 1.0 / (1.0 + jnp.exp(-x))


def _silu(x):
    return x * _sigmoid(x)


def _softplus(x):
    return jnp.maximum(x, 0.0) + jnp.log1p(jnp.exp(-jnp.abs(x)))


def _log_sigmoid(x):
    return jnp.minimum(x, 0.0) - jnp.log1p(jnp.exp(-jnp.abs(x)))


def _causal():
    row = lax.broadcasted_iota(jnp.int32, (CHUNK, CHUNK), 0)
    col = lax.broadcasted_iota(jnp.int32, (CHUNK, CHUNK), 1)
    return col <= row


def _tril2():
    row = lax.broadcasted_iota(jnp.int32, (CHUNK, 2 * CHUNK), 0)
    col = lax.broadcasted_iota(jnp.int32, (CHUNK, 2 * CHUNK), 1)
    return jnp.where((col <= row) | ((col >= CHUNK) & (col - CHUNK <= row)), 1.0, 0.0).astype(BF16)


def _triu2():
    row = lax.broadcasted_iota(jnp.int32, (2 * CHUNK, CHUNK), 0)
    col = lax.broadcasted_iota(jnp.int32, (2 * CHUNK, CHUNK), 1)
    return jnp.where((row <= col) | ((row >= CHUNK) & (row - CHUNK <= col)), 1.0, 0.0).astype(BF16)


def _shift_mat(taps):
    row = lax.broadcasted_iota(jnp.int32, (taps * CHUNK, PAD + CHUNK), 0)
    col = lax.broadcasted_iota(jnp.int32, (taps * CHUNK, PAD + CHUNK), 1)
    sel = None
    for j in range(taps):
        hit = (row >= j * CHUNK) & (row < (j + 1) * CHUNK) & (col == row - j * CHUNK + PAD - j)
        sel = hit if sel is None else sel | hit
    return jnp.where(sel, 1.0, 0.0).astype(BF16)


def _layer_spec(arr, l):
    zeros = (0,) * (arr.ndim - 1)
    return pl.BlockSpec((None,) + arr.shape[1:], lambda *_: (l,) + zeros)


def _history_spec(tb, width, col_block):
    per = tb // PAD
    return pl.BlockSpec((1, PAD, width), lambda b, i: (b, jnp.maximum(i * per - 1, 0), col_block))


def _conv_window(x_ref, hist_ref, c):
    if c > 0:
        return x_ref[0, c * CHUNK - PAD:(c + 1) * CHUNK, :]
    hist = hist_ref[0]
    hist = jnp.where(pl.program_id(1) > 0, hist, jnp.zeros_like(hist))
    return jnp.concatenate([hist, x_ref[0, 0:CHUNK, :]], axis=0)


def _proj_kernel(x_ref, g_ref, w_ref, o_ref, sm_ref, xn_ref, *, sm_tile, sm_col):
    j = pl.program_id(1)

    @pl.when(j == 0)
    def _():
        x = x_ref[...]
        ms = jnp.mean(x * x, axis=-1, keepdims=True)
        xn_ref[...] = (x * lax.rsqrt(ms + EPS) * g_ref[...]).astype(BF16)

    acc = jnp.dot(xn_ref[...], w_ref[...], preferred_element_type=F32)
    o_ref[...] = acc.astype(o_ref.dtype)

    @pl.when(j == sm_tile)
    def _():
        sm_ref[...] = acc[:, sm_col:sm_col + SM_W]


def _proj(x, g, w, l, tm, tn):
    m, d = x.shape
    n = w.shape[2]
    return pl.pallas_call(
        functools.partial(_proj_kernel, sm_tile=_OFF_SMALL // tn, sm_col=_OFF_SMALL % tn),
        grid=(m // tm, n // tn),
        in_specs=[
            pl.BlockSpec((tm, d), lambda i, j: (i, 0)),
            _layer_spec(g, l),
            pl.BlockSpec((None, d, tn), lambda i, j: (l, 0, j)),
        ],
        out_specs=[
            pl.BlockSpec((tm, tn), lambda i, j: (i, j)),
            pl.BlockSpec((tm, SM_W), lambda i, j: (i, 0)),
        ],
        out_shape=[jax.ShapeDtypeStruct((m, n), BF16), jax.ShapeDtypeStruct((m, SM_W), F32)],
        scratch_shapes=[pltpu.VMEM((tm, d), BF16)],
        compiler_params=pltpu.CompilerParams(
            dimension_semantics=("parallel", "arbitrary"), vmem_limit_bytes=VMEM_LIMIT),
        name="in_proj",
    )(x, g, w)


def _gla_kernel(q_ref, k_ref, v_ref, gg_ref, sm_ref, wa_ref, ba_ref, nw_ref, y_ref, st_ref, *, tb):
    dk = q_ref.shape[2] // GLA_HEADS
    dv = v_ref.shape[2] // GLA_HEADS

    @pl.when(pl.program_id(1) == 0)
    def _():
        st_ref[...] = jnp.zeros_like(st_ref)

    chunks = range(tb // CHUNK)
    heads = range(GLA_HEADS)
    causal = _causal()
    tril2 = _tril2()
    scale = dk ** -0.5
    rows = [slice(c * CHUNK, (c + 1) * CHUNK) for c in chunks]

    pre = []
    for c in chunks:
        hi, lo = _split2(sm_ref[0, rows[c], :])
        pre.append(jnp.dot(jnp.concatenate([hi, lo, hi], axis=1), wa_ref[...], preferred_element_type=F32))
    la = [_log_sigmoid(pre[c] + ba_ref[...]) * (1.0 / GLA_TAU) for c in chunks]
    bcum = [_cumsum_rows(tril2, la[c]) for c in chunks]

    for c in chunks:
        r = rows[c]
        kcs = [slice(h * dk, (h + 1) * dk) for h in heads]
        vcs = [slice(h * dv, (h + 1) * dv) for h in heads]
        b = [bcum[c][:, kcs[h]] for h in heads]
        bl = [b[h][CHUNK - 1:CHUNK, :] for h in heads]
        q = [q_ref[0, r, kcs[h]].astype(F32) * scale for h in heads]
        k = [k_ref[0, r, kcs[h]].astype(F32) for h in heads]
        v = [v_ref[0, r, vcs[h]] for h in heads]
        qd = [q[h] * jnp.exp(b[h]) for h in heads]
        ki = [k[h] * jnp.exp(-b[h]) for h in heads]
        kd = [k[h] * jnp.exp(bl[h] - b[h]) for h in heads]
        st = [st_ref[h] for h in heads]
        sc = [_dot_nt(qd[h], ki[h]) for h in heads]
        o2 = [_dot_nt(qd[h], st[h]) for h in heads]
        ds = [_dot_tn(v[h], kd[h]) for h in heads]
        o1 = [_dot(jnp.where(causal, sc[h], 0.0), v[h]) for h in heads]
        for h in heads:
            st_ref[h] = st[h] * jnp.exp(bl[h]) + ds[h]
            o = o1[h] + o2[h]
            ms = jnp.mean(o * o, axis=-1, keepdims=True)
            y = (o * lax.rsqrt(ms + EPS) * nw_ref[:, vcs[h]]) * _silu(gg_ref[0, r, vcs[h]].astype(F32))
            y_ref[0, r, vcs[h]] = y.astype(y_ref.dtype)


def _gla(proj, small, wa3, ba, nw, l, tb):
    bsz, t, _ = proj.shape
    qk_w = ba.shape[2]
    v_w = nw.shape[2]
    dk = qk_w // GLA_HEADS
    dv = v_w // GLA_HEADS
    return pl.pallas_call(
        functools.partial(_gla_kernel, tb=tb),
        grid=(bsz, t // tb),
        in_specs=[
            pl.BlockSpec((1, tb, qk_w), lambda b, i: (b, i, 0)),
            pl.BlockSpec((1, tb, qk_w), lambda b, i: (b, i, 1)),
            pl.BlockSpec((1, tb, v_w), lambda b, i: (b, i, 1)),
            pl.BlockSpec((1, tb, v_w), lambda b, i: (b, i, 2)),
            pl.BlockSpec((1, tb, SM_W), lambda b, i: (b, i, 0)),
            _layer_spec(wa3, l),
            _layer_spec(ba, l),
            _layer_spec(nw, l),
        ],
        out_specs=pl.BlockSpec((1, tb, v_w), lambda b, i: (b, i, 0)),
        out_shape=jax.ShapeDtypeStruct((bsz, t, v_w), BF16),
        scratch_shapes=[pltpu.VMEM((GLA_HEADS, dv, dk), F32)],
        compiler_params=pltpu.CompilerParams(
            dimension_semantics=("parallel", "arbitrary"), vmem_limit_bytes=VMEM_LIMIT),
        name="gla",
    )(proj, proj, proj, proj, small, wa3, ba, nw)


def _mlstm_kernel(qk_ref, qkh_ref, v_ref, og_ref, sm_ref, cw_ref, cb_ref, brow_ref, bcol_ref, nw_ref, y_ref,
                  c_ref, n_ref, m_ref, *, tb):
    qk_w = qk_ref.shape[2] // 2
    dk = qk_w // MLSTM_HEADS
    dv = v_ref.shape[2] // MLSTM_HEADS

    @pl.when(pl.program_id(1) == 0)
    def _():
        c_ref[...] = jnp.zeros_like(c_ref)
        n_ref[...] = jnp.zeros_like(n_ref)
        m_ref[...] = jnp.zeros_like(m_ref)

    chunks = range(tb // CHUNK)
    heads = range(MLSTM_HEADS)
    pairs = [(c, h) for c in chunks for h in heads]
    causal = _causal()
    tril2 = _tril2()
    triu2 = _triu2()
    shift = _shift_mat(CONV_K)
    kscale = dk ** -0.5
    neg_inf = -jnp.inf
    rows = [slice(c * CHUNK, (c + 1) * CHUNK) for c in chunks]

    sm = [sm_ref[0, rows[c], :] for c in chunks]
    pre_col = [sm[c] + brow_ref[...] for c in chunks]
    pre_row = [(sm[c].T + bcol_ref[...])[SM_MI:SM_MI + 2 * MLSTM_HEADS] for c in chunks]
    b_col = [_cumsum_rows(tril2, _log_sigmoid(pre_col[c])) for c in chunks]
    b_row = [_cumsum_cols(_log_sigmoid(pre_row[c]), triu2) for c in chunks]
    xs = [jnp.dot(shift, _conv_window(qk_ref, qkh_ref, c), preferred_element_type=F32) for c in chunks]
    act = []
    for c in chunks:
        acc = cb_ref[...]
        for k in range(CONV_K):
            j = CONV_K - 1 - k
            acc = acc + cw_ref[k:k + 1, :] * xs[c][j * CHUNK:(j + 1) * CHUNK]
        act.append(_silu(acc))

    bc, br, ir, bl, m_loc, w = {}, {}, {}, {}, {}, {}
    for c, h in pairs:
        bc[c, h] = b_col[c][:, SM_MF + h:SM_MF + h + 1]
        br[c, h] = b_row[c][MLSTM_HEADS + h:MLSTM_HEADS + h + 1, :]
        ir[c, h] = pre_row[c][h:h + 1, :]
        bl[c, h] = bc[c, h][CHUNK - 1:CHUNK, :]
        a = bl[c, h] - bc[c, h] + pre_col[c][:, SM_MI + h:SM_MI + h + 1]
        m_loc[c, h] = jnp.max(a, axis=0, keepdims=True)
        w[c, h] = jnp.exp(a - m_loc[c, h])
    m_prev, s_old, s_new = {}, {}, {}
    for h in heads:
        m = m_ref[h][0:1, 0:1]
        for c in chunks:
            m_prev[c, h] = m
            m_new = jnp.maximum(bl[c, h] + m, m_loc[c, h])
            s_old[c, h] = jnp.exp(bl[c, h] + m - m_new)
            s_new[c, h] = jnp.exp(m_loc[c, h] - m_new)
            m = m_new
        m_ref[h] = jnp.broadcast_to(m, m_ref.shape[1:])

    q = {(c, h): act[c][:, h * dk:(h + 1) * dk] for c, h in pairs}
    k = {(c, h): act[c][:, qk_w + h * dk:qk_w + (h + 1) * dk] * kscale for c, h in pairs}
    v = {(c, h): v_ref[0, rows[c], h * dv:(h + 1) * dv] for c, h in pairs}
    wk = {p: w[p] * k[p] for p in pairs}

    qk = {p: _dot_nt(q[p], k[p]) for p in pairs}
    dcs = {p: _dot_tn(wk[p], v[p]) for p in pairs}

    c_prev, n_prev = {}, {}
    for h in heads:
        cst = c_ref[h]
        nst = n_ref[h][0:1, :]
        for c in chunks:
            c_prev[c, h] = cst.astype(BF16)
            n_prev[c, h] = nst
            cst = s_old[c, h] * cst + s_new[c, h] * dcs[c, h]
            nst = s_old[c, h] * nst + s_new[c, h] * jnp.sum(wk[c, h], axis=0, keepdims=True)
        c_ref[h] = cst
        n_ref[h] = jnp.broadcast_to(nst, n_ref.shape[1:])

    log_d = {p: jnp.where(causal, bc[p] - br[p] + ir[p], neg_inf) for p in pairs}
    m_inter = {p: bc[p] + m_prev[p] for p in pairs}
    row_max = {p: jnp.max(log_d[p], axis=-1, keepdims=True) for p in pairs}
    m_t = {p: jnp.maximum(m_inter[p], row_max[p]) for p in pairs}
    wts = {p: qk[p] * jnp.exp(log_d[p] - m_t[p]) for p in pairs}
    s_inter = {p: jnp.exp(m_inter[p] - m_t[p]) for p in pairs}

    pv = {p: _dot(wts[p], v[p]) for p in pairs}
    qc = {p: _dot(q[p], c_prev[p]) for p in pairs}

    row_sum = {p: jnp.sum(wts[p], axis=-1, keepdims=True) for p in pairs}
    qn = {p: jnp.sum(q[p] * n_prev[p], axis=-1, keepdims=True) for p in pairs}
    num = {p: pv[p] + s_inter[p] * qc[p] for p in pairs}
    den = {p: row_sum[p] + s_inter[p] * qn[p] for p in pairs}
    hout = {p: num[p] / jnp.maximum(jnp.abs(den[p]), jnp.exp(-m_t[p])) for p in pairs}
    ms = {p: jnp.mean(hout[p] * hout[p], axis=-1, keepdims=True) for p in pairs}
    for c, h in pairs:
        p = (c, h)
        vc = slice(h * dv, (h + 1) * dv)
        y = (hout[p] * lax.rsqrt(ms[p] + EPS) * nw_ref[:, vc]) * _sigmoid(og_ref[0, rows[c], vc].astype(F32))
        y_ref[0, rows[c], vc] = y.astype(y_ref.dtype)


def _mlstm(proj, small, cw, cb, brow, bcol, nw, l, tb):
    bsz, t, _ = proj.shape
    qk2 = cw.shape[2]
    v_w = nw.shape[2]
    dk = qk2 // 2 // MLSTM_HEADS
    dv = v_w // MLSTM_HEADS
    base = _OFF_MLSTM // v_w
    return pl.pallas_call(
        functools.partial(_mlstm_kernel, tb=tb),
        grid=(bsz, t // tb),
        in_specs=[
            pl.BlockSpec((1, tb, qk2), lambda b, i: (b, i, base)),
            _history_spec(tb, qk2, base),
            pl.BlockSpec((1, tb, v_w), lambda b, i: (b, i, base + 1)),
            pl.BlockSpec((1, tb, v_w), lambda b, i: (b, i, base + 2)),
            pl.BlockSpec((1, tb, SM_W), lambda b, i: (b, i, 0)),
            _layer_spec(cw, l),
            _layer_spec(cb, l),
            _layer_spec(brow, l),
            _layer_spec(bcol, l),
            _layer_spec(nw, l),
        ],
        out_specs=pl.BlockSpec((1, tb, v_w), lambda b, i: (b, i, 0)),
        out_shape=jax.ShapeDtypeStruct((bsz, t, v_w), BF16),
        scratch_shapes=[
            pltpu.VMEM((MLSTM_HEADS, dk, dv), F32),
            pltpu.VMEM((MLSTM_HEADS, 8, dk), F32),
            pltpu.VMEM((MLSTM_HEADS, 8, 128), F32),
        ],
        compiler_params=pltpu.CompilerParams(
            dimension_semantics=("parallel", "arbitrary"), vmem_limit_bytes=VMEM_LIMIT),
        name="mlstm",
    )(proj, proj, proj, proj, small, cw, cb, brow, bcol, nw)


def _ssd_kernel(z_ref, x_ref, xh_ref, bc_ref, bch_ref, sm_ref, cwx_ref, cbx_ref, cwbc_ref, cbbc_ref, brow_ref,
                bcol_ref, alog_row_ref, alog_col_ref, dexp_ref, nw_ref, e_ref, y_ref,
                s_ref, ysc_ref, *, tb):
    inner = x_ref.shape[2]
    gw = inner // SSD_GROUPS
    hpg = SSD_HEADS // SSD_GROUPS
    p = SSD_HEAD_DIM
    z = SSD_STATE

    @pl.when(pl.program_id(1) == 0)
    def _():
        s_ref[...] = jnp.zeros_like(s_ref)

    chunks = range(tb // CHUNK)
    groups = range(SSD_GROUPS)
    causal = _causal()
    tril2 = _tril2()
    triu2 = _triu2()
    shift = _shift_mat(CONV_K)
    neg_inf = -jnp.inf
    a_row = -jnp.exp(alog_row_ref[...])
    a_col = -jnp.exp(alog_col_ref[...])[SM_DT:SM_DT + SSD_HEADS]
    rows = [slice(c * CHUNK, (c + 1) * CHUNK) for c in chunks]

    def conv(xs, cw, cb):
        acc = cb
        for k in range(CONV_K):
            j = CONV_K - 1 - k
            acc = acc + cw[k:k + 1, :] * xs[j * CHUNK:(j + 1) * CHUNK]
        return _silu(acc)

    sm = [sm_ref[0, rows[c], :] for c in chunks]
    dt_col = [_softplus(sm[c] + brow_ref[...]) for c in chunks]
    dt_row = [_softplus((sm[c].T + bcol_ref[...])[SM_DT:SM_DT + SSD_HEADS]) for c in chunks]
    b_col = [_cumsum_rows(tril2, dt_col[c] * a_row) for c in chunks]
    b_row = [_cumsum_cols(dt_row[c] * a_col, triu2) for c in chunks]
    xsx = [jnp.dot(shift, _conv_window(x_ref, xh_ref, c), preferred_element_type=F32) for c in chunks]
    xsb = [jnp.dot(shift, _conv_window(bc_ref, bch_ref, c), preferred_element_type=F32) for c in chunks]
    expd = []
    for c in chunks:
        bl_row = b_col[c][CHUNK - 1:CHUNK, :]
        w_col = jnp.exp(bl_row - b_col[c]) * dt_col[c]
        eb_col = jnp.exp(b_col[c])
        ebl = jnp.broadcast_to(jnp.exp(bl_row), (8, SM_W))
        hi, lo = _split2(jnp.concatenate([w_col, eb_col, ebl], axis=0))
        expd.append(jnp.dot(jnp.concatenate([hi, lo], axis=1), e_ref[...], preferred_element_type=F32))
    xa = [conv(xsx[c], cwx_ref[...], cbx_ref[...]) for c in chunks]
    bca = [conv(xsb[c], cwbc_ref[...], cbbc_ref[...]) for c in chunks]

    for c in chunks:
        w_x = expd[c][0:CHUNK]
        eb_x = expd[c][CHUNK:2 * CHUNK]
        ebl_x = expd[c][2 * CHUNK:2 * CHUNK + 1]
        xg = [xa[c][:, g * gw:(g + 1) * gw] for g in groups]
        bg = [bca[c][:, g * z:(g + 1) * z] for g in groups]
        cg = [bca[c][:, SSD_GROUPS * z + g * z:SSD_GROUPS * z + (g + 1) * z] for g in groups]
        cb = [_dot_nt(cg[g], bg[g]) for g in groups]
        s_prev = [s_ref[g] for g in groups]
        inter = [_dot(cg[g], s_prev[g]) for g in groups]
        dst = [_dot_tn(bg[g], xg[g] * w_x[:, g * gw:(g + 1) * gw]) for g in groups]
        mixes = []
        for h in range(SSD_HEADS):
            bc = b_col[c][:, SM_DT + h:SM_DT + h + 1]
            br = b_row[c][h:h + 1, :]
            dtr = dt_row[c][h:h + 1, :]
            mixes.append(cb[h // hpg] * jnp.exp(jnp.where(causal, bc - br, neg_inf)) * dtr)
        ys = [_dot(mixes[h], xg[h // hpg][:, (h % hpg) * p:(h % hpg + 1) * p]) for h in range(SSD_HEADS)]
        for h in range(SSD_HEADS):
            ysc_ref[:, h * p:(h + 1) * p] = ys[h]
        for g in groups:
            gc = slice(g * gw, (g + 1) * gw)
            s_ref[g] = s_prev[g] * ebl_x[:, gc] + dst[g]
            y = ysc_ref[:, gc] + inter[g] * eb_x[:, gc] + dexp_ref[:, gc] * xg[g]
            y = y * _silu(z_ref[0, rows[c], gc].astype(F32))
            ms = jnp.mean(y * y, axis=-1, keepdims=True)
            y_ref[0, rows[c], gc] = (y * lax.rsqrt(ms + EPS) * nw_ref[:, gc]).astype(y_ref.dtype)


def _ssd(proj, small, cwx, cbx, cwbc, cbbc, brow, bcol, alog_row, alog_col, dexp, nw, emat, l, tb):
    bsz, t, _ = proj.shape
    inner = nw.shape[2]
    bcw = cwbc.shape[2]
    return pl.pallas_call(
        functools.partial(_ssd_kernel, tb=tb),
        grid=(bsz, t // tb),
        in_specs=[
            pl.BlockSpec((1, tb, inner), lambda b, i: (b, i, _OFF_SZ // inner)),
            pl.BlockSpec((1, tb, inner), lambda b, i: (b, i, _OFF_SX // inner)),
            _history_spec(tb, inner, _OFF_SX // inner),
            pl.BlockSpec((1, tb, bcw), lambda b, i: (b, i, _OFF_SBC // bcw)),
            _history_spec(tb, bcw, _OFF_SBC // bcw),
            pl.BlockSpec((1, tb, SM_W), lambda b, i: (b, i, 0)),
            _layer_spec(cwx, l),
            _layer_spec(cbx, l),
            _layer_spec(cwbc, l),
            _layer_spec(cbbc, l),
            _layer_spec(brow, l),
            _layer_spec(bcol, l),
            _layer_spec(alog_row, l),
            _layer_spec(alog_col, l),
            _layer_spec(dexp, l),
            _layer_spec(nw, l),
            pl.BlockSpec(emat.shape, lambda b, i: (0, 0)),
        ],
        out_specs=pl.BlockSpec((1, tb, inner), lambda b, i: (b, i, 0)),
        out_shape=jax.ShapeDtypeStruct((bsz, t, inner), BF16),
        scratch_shapes=[
            pltpu.VMEM((SSD_GROUPS, SSD_STATE, inner // SSD_GROUPS), F32),
            pltpu.VMEM((CHUNK, inner), F32),
        ],
        compiler_params=pltpu.CompilerParams(
            dimension_semantics=("parallel", "arbitrary"), vmem_limit_bytes=VMEM_LIMIT),
        name="ssd",
    )(proj, proj, proj, proj, proj, small, cwx, cbx, cwbc, cbbc, brow, bcol, alog_row, alog_col, dexp, nw, emat)


def _merge_kernel(ya_ref, yb_ref, yc_ref, gates_ref, gb_ref, wb_ref, wo_ref, h_ref, o_ref):
    d = h_ref.shape[1]
    acc = None
    for k, y_ref in enumerate((ya_ref, yb_ref, yc_ref)):
        zk = jnp.dot(y_ref[...], wb_ref[k], preferred_element_type=F32)
        gk = _sigmoid(gates_ref[:, k * d:(k + 1) * d].astype(F32) + gb_ref[:, k * d:(k + 1) * d])
        acc = gk * zk if acc is None else acc + gk * zk
    o_ref[...] = h_ref[...] + jnp.dot(acc.astype(BF16), wo_ref[...], preferred_element_type=F32)


def _merge(ya, yb, yc, proj2, gate_b, w_branch, w_out, h, l, tm):
    m, d = h.shape
    nb = w_branch.shape[1]
    return pl.pallas_call(
        _merge_kernel,
        grid=(m // tm,),
        in_specs=[
            pl.BlockSpec((tm, d), lambda i: (i, 0)),
            pl.BlockSpec((tm, d), lambda i: (i, 0)),
            pl.BlockSpec((tm, d), lambda i: (i, 0)),
            pl.BlockSpec((tm, nb * d), lambda i: (i, _OFF_GATES // (nb * d))),
            _layer_spec(gate_b, l),
            _layer_spec(w_branch, l),
            _layer_spec(w_out, l),
            pl.BlockSpec((tm, d), lambda i: (i, 0)),
        ],
        out_specs=pl.BlockSpec((tm, d), lambda i: (i, 0)),
        out_shape=jax.ShapeDtypeStruct((m, d), F32),
        compiler_params=pltpu.CompilerParams(
            dimension_semantics=("parallel",), vmem_limit_bytes=VMEM_LIMIT),
        name="merge",
    )(ya, yb, yc, proj2, gate_b, w_branch, w_out, h)


def _ffn_kernel(h_ref, g_ref, wg_ref, wv_ref, cwg_ref, cwv_ref, cbg_ref, cbv_ref, wd_ref, gout_ref, o_ref,
                gpad_ref, vpad_ref, *, tm, norm_out):
    @pl.when(pl.program_id(1) == 0)
    def _():
        gpad_ref[0:8, :] = jnp.zeros((8, gpad_ref.shape[1]), F32)
        vpad_ref[0:8, :] = jnp.zeros((8, vpad_ref.shape[1]), F32)

    x = h_ref[0]
    ms = jnp.mean(x * x, axis=-1, keepdims=True)
    xn = (x * lax.rsqrt(ms + EPS) * g_ref[...]).astype(BF16)
    gpad_ref[8:8 + tm, :] = jnp.dot(xn, wg_ref[...], preferred_element_type=F32)
    vpad_ref[8:8 + tm, :] = jnp.dot(xn, wv_ref[...], preferred_element_type=F32)
    gate = cbg_ref[...]
    val = cbv_ref[...]
    for k in range(FFN_CONV):
        s = FFN_CONV - 1 - k
        gate = gate + cwg_ref[k:k + 1, :] * gpad_ref[8 - s:8 - s + tm, :]
        val = val + cwv_ref[k:k + 1, :] * vpad_ref[8 - s:8 - s + tm, :]
    gpad_ref[0:8, :] = gpad_ref[tm:tm + 8, :]
    vpad_ref[0:8, :] = vpad_ref[tm:tm + 8, :]
    act = (_silu(gate) * val).astype(BF16)
    r = x + jnp.dot(act, wd_ref[...], preferred_element_type=F32)
    if norm_out:
        ms = jnp.mean(r * r, axis=-1, keepdims=True)
        r = r * lax.rsqrt(ms + EPS) * gout_ref[...]
    o_ref[0] = r


def _ffn(h3, g, w_up, cw, cb, w_down, g_out, l, tm, norm_out):
    bsz, t, d = h3.shape
    dff = w_down.shape[1]
    resident = dict(pipeline_mode=pl.Buffered(1))
    return pl.pallas_call(
        functools.partial(_ffn_kernel, tm=tm, norm_out=norm_out),
        grid=(bsz, t // tm),
        in_specs=[
            pl.BlockSpec((1, tm, d), lambda b, i: (b, i, 0)),
            _layer_spec(g, l),
            pl.BlockSpec((None, d, dff), lambda b, i: (l, 0, 0), **resident),
            pl.BlockSpec((None, d, dff), lambda b, i: (l, 0, 1), **resident),
            pl.BlockSpec((None, FFN_CONV, dff), lambda b, i: (l, 0, 0)),
            pl.BlockSpec((None, FFN_CONV, dff), lambda b, i: (l, 0, 1)),
            pl.BlockSpec((None, 1, dff), lambda b, i: (l, 0, 0)),
            pl.BlockSpec((None, 1, dff), lambda b, i: (l, 0, 1)),
            pl.BlockSpec((None, dff, d), lambda b, i: (l, 0, 0), **resident),
            pl.BlockSpec((1, d), lambda b, i: (0, 0)),
        ],
        out_specs=pl.BlockSpec((1, tm, d), lambda b, i: (b, i, 0)),
        out_shape=jax.ShapeDtypeStruct((bsz, t, d), F32),
        scratch_shapes=[
            pltpu.VMEM((tm + 8, dff), F32),
            pltpu.VMEM((tm + 8, dff), F32),
        ],
        compiler_params=pltpu.CompilerParams(
            dimension_semantics=("parallel", "arbitrary"), vmem_limit_bytes=VMEM_LIMIT),
        name="conv_ffn",
    )(h3, g, w_up, w_up, cw, cw, cb, cb, w_down, g_out)


_D = 1024
_OFF_MLSTM = 3 * _D
_OFF_GATES = 6 * _D
_OFF_SZ = 9 * _D
_OFF_SX = 10 * _D
_OFF_SBC = 11 * _D
_OFF_SMALL = 11 * _D + _D // 2
_N_PROJ = 12 * _D


def _pick(n, cands):
    for c in cands:
        if n % c == 0:
            return c
    return n


def kernel(x, norm_mix, w_in, gla_wa, gla_ba, gla_norm, mlstm_conv_w, mlstm_conv_b, mlstm_bi, mlstm_bf,
           mlstm_norm, ssd_conv_w, ssd_conv_b, ssd_dt_bias, ssd_a_log, ssd_d, ssd_norm, gate_b, w_branch,
           w_out, norm_ffn, w_up, ffn_conv_w, ffn_conv_b, w_down, norm_final):
    bsz, t, d = x.shape
    assert d == _D
    depth = w_in.shape[0]
    m = bsz * t
    qk = d // 2
    rank = gla_wa.shape[1]
    bc_w = 2 * SSD_GROUPS * SSD_STATE
    sizes = (qk, qk, d, rank, d, 2 * qk, d, MLSTM_HEADS, MLSTM_HEADS, d, d, d + bc_w, SSD_HEADS, 3 * d)
    offs = [0]
    for s in sizes:
        offs.append(offs[-1] + s)
    (o_gq, o_gk, o_gv, o_ga, o_gg, o_mqk, o_mv, o_mi, o_mf, o_mo, o_sz, o_sxbc, o_sdt, o_gates, o_end) = offs
    assert o_end == w_in.shape[2]

    def cols(a, lo, hi):
        return a[..., lo:hi]

    w_perm = jnp.concatenate([
        cols(w_in, o_gq, o_ga),
        cols(w_in, o_gg, o_mi),
        cols(w_in, o_mo, o_sz),
        cols(w_in, o_gates, o_end),
        cols(w_in, o_sz, o_sdt),
        cols(w_in, o_ga, o_gg),
        cols(w_in, o_mi, o_mo),
        cols(w_in, o_sdt, o_gates),
        jnp.zeros((depth, d, _N_PROJ - _OFF_SMALL - rank - 2 * MLSTM_HEADS - SSD_HEADS), w_in.dtype),
    ], axis=-1).astype(BF16)
    assert w_perm.shape[-1] == _N_PROJ

    row3 = lambda a: a[:, None, :]
    col3 = lambda a: a[:, :, None]
    pad_small = lambda a, off: jnp.pad(a, ((0, 0), (off, SM_W - off - a.shape[1])))
    wa_pad = jnp.pad(gla_wa, ((0, 0), (SM_GA, SM_W - SM_GA - rank), (0, 0)))
    wa_hi = wa_pad.astype(BF16)
    wa_lo = (wa_pad - wa_hi.astype(F32)).astype(BF16)
    wa3 = jnp.concatenate([wa_hi, wa_hi, wa_lo], axis=1)
    m_bias = pad_small(mlstm_bi, SM_MI) + pad_small(mlstm_bf, SM_MF)
    s_bias = pad_small(ssd_dt_bias, SM_DT)
    alog = pad_small(ssd_a_log, SM_DT)
    dexp = jnp.repeat(ssd_d, SSD_HEAD_DIM, axis=-1)
    head_of_col = jnp.arange(d) // SSD_HEAD_DIM
    emat1 = (jnp.arange(SM_W)[:, None] == (SM_DT + head_of_col)[None, :]).astype(BF16)
    emat = jnp.concatenate([emat1, emat1], axis=0)
    w_branch_b = w_branch.astype(BF16)
    w_out_b = w_out.astype(BF16)
    w_up_b = w_up.astype(BF16)
    w_down_b = w_down.astype(BF16)
    norm_mix3, norm_ffn3 = row3(norm_mix), row3(norm_ffn)
    gla_ba3, gla_norm3 = row3(gla_ba), row3(gla_norm)
    m_cb3, m_norm3 = row3(mlstm_conv_b), row3(mlstm_norm)
    s_cwx, s_cwbc = ssd_conv_w[:, :, :d], ssd_conv_w[:, :, d:]
    s_cbx, s_cbbc = row3(ssd_conv_b[:, :d]), row3(ssd_conv_b[:, d:])
    s_norm3, dexp3, gate_b3, f_cb3 = row3(ssd_norm), row3(dexp), row3(gate_b), row3(ffn_conv_b)

    tm_proj = _pick(m, (1024, 512, 256, 128, 64))
    tn_proj = _pick(_N_PROJ, (3072, 2048, 1024))
    tb = _pick(t, (512, 256, 128, 64))
    tm_merge = _pick(m, (512, 256, 128, 64))
    tm_ffn = _pick(t, (512, 256, 128, 64))

    h = x.reshape(m, d)
    for l in range(depth):
        proj2, small2 = _proj(h, norm_mix3, w_perm, l, tm_proj, tn_proj)
        proj3 = proj2.reshape(bsz, t, _N_PROJ)
        small3 = small2.reshape(bsz, t, SM_W)
        y_gla = _gla(proj3, small3, wa3, gla_ba3, gla_norm3, l, tb)
        y_m = _mlstm(proj3, small3, mlstm_conv_w, m_cb3, row3(m_bias), col3(m_bias), m_norm3, l, tb)
        y_s = _ssd(proj3, small3, s_cwx, s_cbx, s_cwbc, s_cbbc, row3(s_bias), col3(s_bias), row3(alog),
                   col3(alog), dexp3, s_norm3, emat, l, tb)
        h = _merge(y_gla.reshape(m, d), y_m.reshape(m, d), y_s.reshape(m, d), proj2, gate_b3,
                   w_branch_b, w_out_b, h, l, tm_merge)
        h = _ffn(h.reshape(bsz, t, d), norm_ffn3, w_up_b, ffn_conv_w, f_cb3, w_down_b, norm_final[None, :], l,
                 tm_ffn, norm_out=(l == depth - 1)).reshape(m, d)
    return h.reshape(bsz, t, d)
```

```python
import functools

import jax
import jax.numpy as jnp
from jax import lax
from jax.experimental import pallas as pl
from jax.experimental.pallas import tpu as pltpu

F32 = jnp.float32
BF16 = jnp.bfloat16

EPS = 1e-6
CHUNK = 64
GLA_HEADS = 4
GLA_TAU = 16.0
MLSTM_HEADS = 4
SSD_HEADS = 16
SSD_GROUPS = 2
SSD_STATE = 128
SSD_HEAD_DIM = 64
FFN_CONV = 3
CONV_K = 4
PAD = 16

SM_GA = 0
SM_MI = 16
SM_MF = 20
SM_DT = 24
SM_W = 128

VMEM_LIMIT = 52 * 1024 * 1024


def _dot(a, b):
    return jnp.dot(a.astype(BF16), b.astype(BF16), preferred_element_type=F32)


def _dot_nt(a, b):
    return lax.dot_general(a.astype(BF16), b.astype(BF16), (((1,), (1,)), ((), ())),
                           preferred_element_type=F32)


def _dot_tn(a, b):
    return jnp.dot(a.astype(F32).T.astype(BF16), b.astype(BF16), preferred_element_type=F32)


def _split2(x):
    hi = x.astype(BF16)
    return hi, (x - hi.astype(F32)).astype(BF16)


def _cumsum_rows(tril2, x):
    return jnp.dot(tril2, jnp.concatenate(_split2(x), axis=0), preferred_element_type=F32)


def _cumsum_cols(x, triu2):
    return jnp.dot(jnp.concatenate(_split2(x), axis=1), triu2, preferred_element_type=F32)


def _sigmoid(x):
    return 1.0 / (1.0 + jnp.exp(-x))


def _silu(x):
    return x * _sigmoid(x)


def _softplus(x):
    return jnp.maximum(x, 0.0) + jnp.log1p(jnp.exp(-jnp.abs(x)))


def _log_sigmoid(x):
    return jnp.minimum(x, 0.0) - jnp.log1p(jnp.exp(-jnp.abs(x)))


def _causal():
    row = lax.broadcasted_iota(jnp.int32, (CHUNK, CHUNK), 0)
    col = lax.broadcasted_iota(jnp.int32, (CHUNK, CHUNK), 1)
    return col <= row


def _tril2():
    row = lax.broadcasted_iota(jnp.int32, (CHUNK, 2 * CHUNK), 0)
    col = lax.broadcasted_iota(jnp.int32, (CHUNK, 2 * CHUNK), 1)
    return jnp.where((col <= row) | ((col >= CHUNK) & (col - CHUNK <= row)), 1.0, 0.0).astype(BF16)


def _triu2():
    row = lax.broadcasted_iota(jnp.int32, (2 * CHUNK, CHUNK), 0)
    col = lax.broadcasted_iota(jnp.int32, (2 * CHUNK, CHUNK), 1)
    return jnp.where((row <= col) | ((row >= CHUNK) & (row - CHUNK <= col)), 1.0, 0.0).astype(BF16)


def _shift_mat(taps):
    row = lax.broadcasted_iota(jnp.int32, (taps * CHUNK, PAD + CHUNK), 0)
    col = lax.broadcasted_iota(jnp.int32, (taps * CHUNK, PAD + CHUNK), 1)
    sel = None
    for j in range(taps):
        hit = (row >= j * CHUNK) & (row < (j + 1) * CHUNK) & (col == row - j * CHUNK + PAD - j)
        sel = hit if sel is None else sel | hit
    return jnp.where(sel, 1.0, 0.0).astype(BF16)


def _layer_spec(arr, l):
    zeros = (0,) * (arr.ndim - 1)
    return pl.BlockSpec((None,) + arr.shape[1:], lambda *_: (l,) + zeros)


def _history_spec(tb, width, col_block):
    per = tb // PAD
    return pl.BlockSpec((1, PAD, width), lambda b, i: (b, jnp.maximum(i * per - 1, 0), col_block))


def _conv_window(x_ref, hist_ref, c):
    if c > 0:
        return x_ref[0, c * CHUNK - PAD:(c + 1) * CHUNK, :]
    hist = hist_ref[0]
    hist = jnp.where(pl.program_id(1) > 0, hist, jnp.zeros_like(hist))
    return jnp.concatenate([hist, x_ref[0, 0:CHUNK, :]], axis=0)


def _proj_kernel(x_ref, g_ref, w_ref, o_ref, sm_ref, xn_ref, *, sm_tile, sm_col):
    j = pl.program_id(1)

    @pl.when(j == 0)
    def _():
        x = x_ref[...]
        ms = jnp.mean(x * x, axis=-1, keepdims=True)
        xn_ref[...] = (x * lax.rsqrt(ms + EPS) * g_ref[...]).astype(BF16)

    acc = jnp.dot(xn_ref[...], w_ref[...], preferred_element_type=F32)
    o_ref[...] = acc.astype(o_ref.dtype)

    @pl.when(j == sm_tile)
    def _():
        sm_ref[...] = acc[:, sm_col:sm_col + SM_W]


def _proj(x, g, w, l, tm, tn):
    m, d = x.shape
    n = w.shape[2]
    return pl.pallas_call(
        functools.partial(_proj_kernel, sm_tile=_OFF_SMALL // tn, sm_col=_OFF_SMALL % tn),
        grid=(m // tm, n // tn),
        in_specs=[
            pl.BlockSpec((tm, d), lambda i, j: (i, 0)),
            _layer_spec(g, l),
            pl.BlockSpec((None, d, tn), lambda i, j: (l, 0, j)),
        ],
        out_specs=[
            pl.BlockSpec((tm, tn), lambda i, j: (i, j)),
            pl.BlockSpec((tm, SM_W), lambda i, j: (i, 0)),
        ],
        out_shape=[jax.ShapeDtypeStruct((m, n), BF16), jax.ShapeDtypeStruct((m, SM_W), F32)],
        scratch_shapes=[pltpu.VMEM((tm, d), BF16)],
        compiler_params=pltpu.CompilerParams(
            dimension_semantics=("parallel", "arbitrary"), vmem_limit_bytes=VMEM_LIMIT),
        name="in_proj",
    )(x, g, w)


def _gla_kernel(q_ref, k_ref, v_ref, gg_ref, sm_ref, wa_ref, ba_ref, nw_ref, y_ref, st_ref, *, tb):
    dk = q_ref.shape[2] // GLA_HEADS
    dv = v_ref.shape[2] // GLA_HEADS

    @pl.when(pl.program_id(1) == 0)
    def _():
        st_ref[...] = jnp.zeros_like(st_ref)

    chunks = range(tb // CHUNK)
    heads = range(GLA_HEADS)
    causal = _causal()
    tril2 = _tril2()
    scale = dk ** -0.5
    rows = [slice(c * CHUNK, (c + 1) * CHUNK) for c in chunks]

    pre = []
    for c in chunks:
        hi, lo = _split2(sm_ref[0, rows[c], :])
        pre.append(jnp.dot(jnp.concatenate([hi, lo, hi], axis=1), wa_ref[...], preferred_element_type=F32))
    la = [_log_sigmoid(pre[c] + ba_ref[...]) * (1.0 / GLA_TAU) for c in chunks]
    bcum = [_cumsum_rows(tril2, la[c]) for c in chunks]

    for c in chunks:
        r = rows[c]
        kcs = [slice(h * dk, (h + 1) * dk) for h in heads]
        vcs = [slice(h * dv, (h + 1) * dv) for h in heads]
        b = [bcum[c][:, kcs[h]] for h in heads]
        bl = [b[h][CHUNK - 1:CHUNK, :] for h in heads]
        q = [q_ref[0, r, kcs[h]].astype(F32) * scale for h in heads]
        k = [k_ref[0, r, kcs[h]].astype(F32) for h in heads]
        v = [v_ref[0, r, vcs[h]] for h in heads]
        qd = [q[h] * jnp.exp(b[h]) for h in heads]
        ki = [k[h] * jnp.exp(-b[h]) for h in heads]
        kd = [k[h] * jnp.exp(bl[h] - b[h]) for h in heads]
        st = [st_ref[h] for h in heads]
        sc = [_dot_nt(qd[h], ki[h]) for h in heads]
        o2 = [_dot_nt(qd[h], st[h]) for h in heads]
        ds = [_dot_tn(v[h], kd[h]) for h in heads]
        o1 = [_dot(jnp.where(causal, sc[h], 0.0), v[h]) for h in heads]
        for h in heads:
            st_ref[h] = st[h] * jnp.exp(bl[h]) + ds[h]
            o = o1[h] + o2[h]
            ms = jnp.mean(o * o, axis=-1, keepdims=True)
            y = (o * lax.rsqrt(ms + EPS) * nw_ref[:, vcs[h]]) * _silu(gg_ref[0, r, vcs[h]].astype(F32))
            y_ref[0, r, vcs[h]] = y.astype(y_ref.dtype)


def _gla(proj, small, wa3, ba, nw, l, tb):
    bsz, t, _ = proj.shape
    qk_w = ba.shape[2]
    v_w = nw.shape[2]
    dk = qk_w // GLA_HEADS
    dv = v_w // GLA_HEADS
    return pl.pallas_call(
        functools.partial(_gla_kernel, tb=tb),
        grid=(bsz, t // tb),
        in_specs=[
            pl.BlockSpec((1, tb, qk_w), lambda b, i: (b, i, 0)),
            pl.BlockSpec((1, tb, qk_w), lambda b, i: (b, i, 1)),
            pl.BlockSpec((1, tb, v_w), lambda b, i: (b, i, 1)),
            pl.BlockSpec((1, tb, v_w), lambda b, i: (b, i, 2)),
            pl.BlockSpec((1, tb, SM_W), lambda b, i: (b, i, 0)),
            _layer_spec(wa3, l),
            _layer_spec(ba, l),
            _layer_spec(nw, l),
        ],
        out_specs=pl.BlockSpec((1, tb, v_w), lambda b, i: (b, i, 0)),
        out_shape=jax.ShapeDtypeStruct((bsz, t, v_w), BF16),
        scratch_shapes=[pltpu.VMEM((GLA_HEADS, dv, dk), F32)],
        compiler_params=pltpu.CompilerParams(
            dimension_semantics=("parallel", "arbitrary"), vmem_limit_bytes=VMEM_LIMIT),
        name="gla",
    )(proj, proj, proj, proj, small, wa3, ba, nw)


def _mlstm_kernel(qk_ref, qkh_ref, v_ref, og_ref, sm_ref, cw_ref, cb_ref, brow_ref, bcol_ref, nw_ref, y_ref,
                  c_ref, n_ref, m_ref, *, tb):
    qk_w = qk_ref.shape[2] // 2
    dk = qk_w // MLSTM_HEADS
    dv = v_ref.shape[2] // MLSTM_HEADS

    @pl.when(pl.program_id(1) == 0)
    def _():
        c_ref[...] = jnp.zeros_like(c_ref)
        n_ref[...] = jnp.zeros_like(n_ref)
        m_ref[...] = jnp.zeros_like(m_ref)

    chunks = range(tb // CHUNK)
    heads = range(MLSTM_HEADS)
    pairs = [(c, h) for c in chunks for h in heads]
    causal = _causal()
    tril2 = _tril2()
    triu2 = _triu2()
    shift = _shift_mat(CONV_K)
    kscale = dk ** -0.5
    neg_inf = -jnp.inf
    rows = [slice(c * CHUNK, (c + 1) * CHUNK) for c in chunks]

    sm = [sm_ref[0, rows[c], :] for c in chunks]
    pre_col = [sm[c] + brow_ref[...] for c in chunks]
    pre_row = [(sm[c].T + bcol_ref[...])[SM_MI:SM_MI + 2 * MLSTM_HEADS] for c in chunks]
    b_col = [_cumsum_rows(tril2, _log_sigmoid(pre_col[c])) for c in chunks]
    b_row = [_cumsum_cols(_log_sigmoid(pre_row[c]), triu2) for c in chunks]
    xs = [jnp.dot(shift, _conv_window(qk_ref, qkh_ref, c), preferred_element_type=F32) for c in chunks]
    act = []
    for c in chunks:
        acc = cb_ref[...]
        for k in range(CONV_K):
            j = CONV_K - 1 - k
            acc = acc + cw_ref[k:k + 1, :] * xs[c][j * CHUNK:(j + 1) * CHUNK]
        act.append(_silu(acc))

    bc, br, ir, bl, m_loc, w = {}, {}, {}, {}, {}, {}
    for c, h in pairs:
        bc[c, h] = b_col[c][:, SM_MF + h:SM_MF + h + 1]
        br[c, h] = b_row[c][MLSTM_HEADS + h:MLSTM_HEADS + h + 1, :]
        ir[c, h] = pre_row[c][h:h + 1, :]
        bl[c, h] = bc[c, h][CHUNK - 1:CHUNK, :]
        a = bl[c, h] - bc[c, h] + pre_col[c][:, SM_MI + h:SM_MI + h + 1]
        m_loc[c, h] = jnp.max(a, axis=0, keepdims=True)
        w[c, h] = jnp.exp(a - m_loc[c, h])
    m_prev, s_old, s_new = {}, {}, {}
    for h in heads:
        m = m_ref[h][0:1, 0:1]
        for c in chunks:
            m_prev[c, h] = m
            m_new = jnp.maximum(bl[c, h] + m, m_loc[c, h])
            s_old[c, h] = jnp.exp(bl[c, h] + m - m_new)
            s_new[c, h] = jnp.exp(m_loc[c, h] - m_new)
            m = m_new
        m_ref[h] = jnp.broadcast_to(m, m_ref.shape[1:])

    q = {(c, h): act[c][:, h * dk:(h + 1) * dk] for c, h in pairs}
    k = {(c, h): act[c][:, qk_w + h * dk:qk_w + (h + 1) * dk] * kscale for c, h in pairs}
    v = {(c, h): v_ref[0, rows[c], h * dv:(h + 1) * dv] for c, h in pairs}
    wk = {p: w[p] * k[p] for p in pairs}

    qk = {p: _dot_nt(q[p], k[p]) for p in pairs}
    dcs = {p: _dot_tn(wk[p], v[p]) for p in pairs}

    c_prev, n_prev = {}, {}
    for h in heads:
        cst = c_ref[h]
        nst = n_ref[h][0:1, :]
        for c in chunks:
            c_prev[c, h] = cst.astype(BF16)
            n_prev[c, h] = nst
            cst = s_old[c, h] * cst + s_new[c, h] * dcs[c, h]
            nst = s_old[c, h] * nst + s_new[c, h] * jnp.sum(wk[c, h], axis=0, keepdims=True)
        c_ref[h] = cst
        n_ref[h] = jnp.broadcast_to(nst, n_ref.shape[1:])

    log_d = {p: jnp.where(causal, bc[p] - br[p] + ir[p], neg_inf) for p in pairs}
    m_inter = {p: bc[p] + m_prev[p] for p in pairs}
    row_max = {p: jnp.max(log_d[p], axis=-1, keepdims=True) for p in pairs}
    m_t = {p: jnp.maximum(m_inter[p], row_max[p]) for p in pairs}
    wts = {p: qk[p] * jnp.exp(log_d[p] - m_t[p]) for p in pairs}
    s_inter = {p: jnp.exp(m_inter[p] - m_t[p]) for p in pairs}

    pv = {p: _dot(wts[p], v[p]) for p in pairs}
    qc = {p: _dot(q[p], c_prev[p]) for p in pairs}

    row_sum = {p: jnp.sum(wts[p], axis=-1, keepdims=True) for p in pairs}
    qn = {p: jnp.sum(q[p] * n_prev[p], axis=-1, keepdims=True) for p in pairs}
    num = {p: pv[p] + s_inter[p] * qc[p] for p in pairs}
    den = {p: row_sum[p] + s_inter[p] * qn[p] for p in pairs}
    hout = {p: num[p] / jnp.maximum(jnp.abs(den[p]), jnp.exp(-m_t[p])) for p in pairs}
    ms = {p: jnp.mean(hout[p] * hout[p], axis=-1, keepdims=True) for p in pairs}
    for c, h in pairs:
        p = (c, h)
        vc = slice(h * dv, (h + 1) * dv)
        y = (hout[p] * lax.rsqrt(ms[p] + EPS) * nw_ref[:, vc]) * _sigmoid(og_ref[0, rows[c], vc].astype(F32))
        y_ref[0, rows[c], vc] = y.astype(y_ref.dtype)


def _mlstm(proj, small, cw, cb, brow, bcol, nw, l, tb):
    bsz, t, _ = proj.shape
    qk2 = cw.shape[2]
    v_w = nw.shape[2]
    dk = qk2 // 2 // MLSTM_HEADS
    dv = v_w // MLSTM_HEADS
    base = _OFF_MLSTM // v_w
    return pl.pallas_call(
        functools.partial(_mlstm_kernel, tb=tb),
        grid=(bsz, t // tb),
        in_specs=[
            pl.BlockSpec((1, tb, qk2), lambda b, i: (b, i, base)),
            _history_spec(tb, qk2, base),
            pl.BlockSpec((1, tb, v_w), lambda b, i: (b, i, base + 1)),
            pl.BlockSpec((1, tb, v_w), lambda b, i: (b, i, base + 2)),
            pl.BlockSpec((1, tb, SM_W), lambda b, i: (b, i, 0)),
            _layer_spec(cw, l),
            _layer_spec(cb, l),
            _layer_spec(brow, l),
            _layer_spec(bcol, l),
            _layer_spec(nw, l),
        ],
        out_specs=pl.BlockSpec((1, tb, v_w), lambda b, i: (b, i, 0)),
        out_shape=jax.ShapeDtypeStruct((bsz, t, v_w), BF16),
        scratch_shapes=[
            pltpu.VMEM((MLSTM_HEADS, dk, dv), F32),
            pltpu.VMEM((MLSTM_HEADS, 8, dk), F32),
            pltpu.VMEM((MLSTM_HEADS, 8, 128), F32),
        ],
        compiler_params=pltpu.CompilerParams(
            dimension_semantics=("parallel", "arbitrary"), vmem_limit_bytes=VMEM_LIMIT),
        name="mlstm",
    )(proj, proj, proj, proj, small, cw, cb, brow, bcol, nw)


def _ssd_kernel(z_ref, x_ref, xh_ref, bc_ref, bch_ref, sm_ref, cwx_ref, cbx_ref, cwbc_ref, cbbc_ref, brow_ref,
                bcol_ref, alog_row_ref, alog_col_ref, dexp_ref, nw_ref, e_ref, y_ref,
                s_ref, ysc_ref, *, tb):
    inner = x_ref.shape[2]
    gw = inner // SSD_GROUPS
    hpg = SSD_HEADS // SSD_GROUPS
    p = SSD_HEAD_DIM
    z = SSD_STATE

    @pl.when(pl.program_id(1) == 0)
    def _():
        s_ref[...] = jnp.zeros_like(s_ref)

    chunks = range(tb // CHUNK)
    groups = range(SSD_GROUPS)
    causal = _causal()
    tril2 = _tril2()
    triu2 = _triu2()
    shift = _shift_mat(CONV_K)
    neg_inf = -jnp.inf
    a_row = -jnp.exp(alog_row_ref[...])
    a_col = -jnp.exp(alog_col_ref[...])[SM_DT:SM_DT + SSD_HEADS]
    rows = [slice(c * CHUNK, (c + 1) * CHUNK) for c in chunks]

    def conv(xs, cw, cb):
        acc = cb
        for k in range(CONV_K):
            j = CONV_K - 1 - k
            acc = acc + cw[k:k + 1, :] * xs[j * CHUNK:(j + 1) * CHUNK]
        return _silu(acc)

    sm = [sm_ref[0, rows[c], :] for c in chunks]
    dt_col = [_softplus(sm[c] + brow_ref[...]) for c in chunks]
    dt_row = [_softplus((sm[c].T + bcol_ref[...])[SM_DT:SM_DT + SSD_HEADS]) for c in chunks]
    b_col = [_cumsum_rows(tril2, dt_col[c] * a_row) for c in chunks]
    b_row = [_cumsum_cols(dt_row[c] * a_col, triu2) for c in chunks]
    xsx = [jnp.dot(shift, _conv_window(x_ref, xh_ref, c), preferred_element_type=F32) for c in chunks]
    xsb = [jnp.dot(shift, _conv_window(bc_ref, bch_ref, c), preferred_element_type=F32) for c in chunks]
    expd = []
    for c in chunks:
        bl_row = b_col[c][CHUNK - 1:CHUNK, :]
        w_col = jnp.exp(bl_row - b_col[c]) * dt_col[c]
        eb_col = jnp.exp(b_col[c])
        ebl = jnp.broadcast_to(jnp.exp(bl_row), (8, SM_W))
        hi, lo = _split2(jnp.concatenate([w_col, eb_col, ebl], axis=0))
        expd.append(jnp.dot(jnp.concatenate([hi, lo], axis=1), e_ref[...], preferred_element_type=F32))
    xa = [conv(xsx[c], cwx_ref[...], cbx_ref[...]) for c in chunks]
    bca = [conv(xsb[c], cwbc_ref[...], cbbc_ref[...]) for c in chunks]

    for c in chunks:
        w_x = expd[c][0:CHUNK]
        eb_x = expd[c][CHUNK:2 * CHUNK]
        ebl_x = expd[c][2 * CHUNK:2 * CHUNK + 1]
        xg = [xa[c][:, g * gw:(g + 1) * gw] for g in groups]
        bg = [bca[c][:, g * z:(g + 1) * z] for g in groups]
        cg = [bca[c][:, SSD_GROUPS * z + g * z:SSD_GROUPS * z + (g + 1) * z] for g in groups]
        cb = [_dot_nt(cg[g], bg[g]) for g in groups]
        s_prev = [s_ref[g] for g in groups]
        inter = [_dot(cg[g], s_prev[g]) for g in groups]
        dst = [_dot_tn(bg[g], xg[g] * w_x[:, g * gw:(g + 1) * gw]) for g in groups]
        mixes = []
        for h in range(SSD_HEADS):
            bc = b_col[c][:, SM_DT + h:SM_DT + h + 1]
            br = b_row[c][h:h + 1, :]
            dtr = dt_row[c][h:h + 1, :]
            mixes.append(cb[h // hpg] * jnp.exp(jnp.where(causal, bc - br, neg_inf)) * dtr)
        ys = [_dot(mixes[h], xg[h // hpg][:, (h % hpg) * p:(h % hpg + 1) * p]) for h in range(SSD_HEADS)]
        for h in range(SSD_HEADS):
            ysc_ref[:, h * p:(h + 1) * p] = ys[h]
        for g in groups:
            gc = slice(g * gw, (g + 1) * gw)
            s_ref[g] = s_prev[g] * ebl_x[:, gc] + dst[g]
            y = ysc_ref[:, gc] + inter[g] * eb_x[:, gc] + dexp_ref[:, gc] * xg[g]
            y = y * _silu(z_ref[0, rows[c], gc].astype(F32))
            ms = jnp.mean(y * y, axis=-1, keepdims=True)
            y_ref[0, rows[c], gc] = (y * lax.rsqrt(ms + EPS) * nw_ref[:, gc]).astype(y_ref.dtype)


def _ssd(proj, small, cwx, cbx, cwbc, cbbc, brow, bcol, alog_row, alog_col, dexp, nw, emat, l, tb):
    bsz, t, _ = proj.shape
    inner = nw.shape[2]
    bcw = cwbc.shape[2]
    return pl.pallas_call(
        functools.partial(_ssd_kernel, tb=tb),
        grid=(bsz, t // tb),
        in_specs=[
            pl.BlockSpec((1, tb, inner), lambda b, i: (b, i, _OFF_SZ // inner)),
            pl.BlockSpec((1, tb, inner), lambda b, i: (b, i, _OFF_SX // inner)),
            _history_spec(tb, inner, _OFF_SX // inner),
            pl.BlockSpec((1, tb, bcw), lambda b, i: (b, i, _OFF_SBC // bcw)),
            _history_spec(tb, bcw, _OFF_SBC // bcw),
            pl.BlockSpec((1, tb, SM_W), lambda b, i: (b, i, 0)),
            _layer_spec(cwx, l),
            _layer_spec(cbx, l),
            _layer_spec(cwbc, l),
            _layer_spec(cbbc, l),
            _layer_spec(brow, l),
            _layer_spec(bcol, l),
            _layer_spec(alog_row, l),
            _layer_spec(alog_col, l),
            _layer_spec(dexp, l),
            _layer_spec(nw, l),
            pl.BlockSpec(emat.shape, lambda b, i: (0, 0)),
        ],
        out_specs=pl.BlockSpec((1, tb, inner), lambda b, i: (b, i, 0)),
        out_shape=jax.ShapeDtypeStruct((bsz, t, inner), BF16),
        scratch_shapes=[
            pltpu.VMEM((SSD_GROUPS, SSD_STATE, inner // SSD_GROUPS), F32),
            pltpu.VMEM((CHUNK, inner), F32),
        ],
        compiler_params=pltpu.CompilerParams(
            dimension_semantics=("parallel", "arbitrary"), vmem_limit_bytes=VMEM_LIMIT),
        name="ssd",
    )(proj, proj, proj, proj, proj, small, cwx, cbx, cwbc, cbbc, brow, bcol, alog_row, alog_col, dexp, nw, emat)


def _merge_kernel(ya_ref, yb_ref, yc_ref, gates_ref, gb_ref, wb_ref, wo_ref, h_ref, o_ref):
    d = h_ref.shape[1]
    acc = None
    for k, y_ref in enumerate((ya_ref, yb_ref, yc_ref)):
        zk = jnp.dot(y_ref[...], wb_ref[k], preferred_element_type=F32)
        gk = _sigmoid(gates_ref[:, k * d:(k + 1) * d].astype(F32) + gb_ref[:, k * d:(k + 1) * d])
        acc = gk * zk if acc is None else acc + gk * zk
    o_ref[...] = h_ref[...] + jnp.dot(acc.astype(BF16), wo_ref[...], preferred_element_type=F32)


def _merge(ya, yb, yc, proj2, gate_b, w_branch, w_out, h, l, tm):
    m, d = h.shape
    nb = w_branch.shape[1]
    return pl.pallas_call(
        _merge_kernel,
        grid=(m // tm,),
        in_specs=[
            pl.BlockSpec((tm, d), lambda i: (i, 0)),
            pl.BlockSpec((tm, d), lambda i: (i, 0)),
            pl.BlockSpec((tm, d), lambda i: (i, 0)),
            pl.BlockSpec((tm, nb * d), lambda i: (i, _OFF_GATES // (nb * d))),
            _layer_spec(gate_b, l),
            _layer_spec(w_branch, l),
            _layer_spec(w_out, l),
            pl.BlockSpec((tm, d), lambda i: (i, 0)),
        ],
        out_specs=pl.BlockSpec((tm, d), lambda i: (i, 0)),
        out_shape=jax.ShapeDtypeStruct((m, d), F32),
        compiler_params=pltpu.CompilerParams(
            dimension_semantics=("parallel",), vmem_limit_bytes=VMEM_LIMIT),
        name="merge",
    )(ya, yb, yc, proj2, gate_b, w_branch, w_out, h)


def _ffn_kernel(h_ref, g_ref, wg_ref, wv_ref, cwg_ref, cwv_ref, cbg_ref, cbv_ref, wd_ref, gout_ref, o_ref,
                gpad_ref, vpad_ref, *, tm, norm_out):
    @pl.when(pl.program_id(1) == 0)
    def _():
        gpad_ref[0:8, :] = jnp.zeros((8, gpad_ref.shape[1]), F32)
        vpad_ref[0:8, :] = jnp.zeros((8, vpad_ref.shape[1]), F32)

    x = h_ref[0]
    ms = jnp.mean(x * x, axis=-1, keepdims=True)
    xn = (x * lax.rsqrt(ms + EPS) * g_ref[...]).astype(BF16)
    gpad_ref[8:8 + tm, :] = jnp.dot(xn, wg_ref[...], preferred_element_type=F32)
    vpad_ref[8:8 + tm, :] = jnp.dot(xn, wv_ref[...], preferred_element_type=F32)
    gate = cbg_ref[...]
    val = cbv_ref[...]
    for k in range(FFN_CONV):
        s = FFN_CONV - 1 - k
        gate = gate + cwg_ref[k:k + 1, :] * gpad_ref[8 - s:8 - s + tm, :]
        val = val + cwv_ref[k:k + 1, :] * vpad_ref[8 - s:8 - s + tm, :]
    gpad_ref[0:8, :] = gpad_ref[tm:tm + 8, :]
    vpad_ref[0:8, :] = vpad_ref[tm:tm + 8, :]
    act = (_silu(gate) * val).astype(BF16)
    r = x + jnp.dot(act, wd_ref[...], preferred_element_type=F32)
    if norm_out:
        ms = jnp.mean(r * r, axis=-1, keepdims=True)
        r = r * lax.rsqrt(ms + EPS) * gout_ref[...]
    o_ref[0] = r


def _ffn(h3, g, w_up, cw, cb, w_down, g_out, l, tm, norm_out):
    bsz, t, d = h3.shape
    dff = w_down.shape[1]
    resident = dict(pipeline_mode=pl.Buffered(1))
    return pl.pallas_call(
        functools.partial(_ffn_kernel, tm=tm, norm_out=norm_out),
        grid=(bsz, t // tm),
        in_specs=[
            pl.BlockSpec((1, tm, d), lambda b, i: (b, i, 0)),
            _layer_spec(g, l),
            pl.BlockSpec((None, d, dff), lambda b, i: (l, 0, 0), **resident),
            pl.BlockSpec((None, d, dff), lambda b, i: (l, 0, 1), **resident),
            pl.BlockSpec((None, FFN_CONV, dff), lambda b, i: (l, 0, 0)),
            pl.BlockSpec((None, FFN_CONV, dff), lambda b, i: (l, 0, 1)),
            pl.BlockSpec((None, 1, dff), lambda b, i: (l, 0, 0)),
            pl.BlockSpec((None, 1, dff), lambda b, i: (l, 0, 1)),
            pl.BlockSpec((None, dff, d), lambda b, i: (l, 0, 0), **resident),
            pl.BlockSpec((1, d), lambda b, i: (0, 0)),
        ],
        out_specs=pl.BlockSpec((1, tm, d), lambda b, i: (b, i, 0)),
        out_shape=jax.ShapeDtypeStruct((bsz, t, d), F32),
        scratch_shapes=[
            pltpu.VMEM((tm + 8, dff), F32),
            pltpu.VMEM((tm + 8, dff), F32),
        ],
        compiler_params=pltpu.CompilerParams(
            dimension_semantics=("parallel", "arbitrary"), vmem_limit_bytes=VMEM_LIMIT),
        name="conv_ffn",
    )(h3, g, w_up, w_up, cw, cw, cb, cb, w_down, g_out)


_RL_TILE = 512


def _relayout_kernel(*refs, shift):
    main_ref, next_ref, o_ref = refs[0], refs[1], refs[-1]
    if shift == 0:
        o_ref[...] = main_ref[...].astype(BF16)
    else:
        x = jnp.concatenate([main_ref[...], next_ref[...]], axis=1)
        o_ref[...] = x[:, shift:shift + _RL_TILE].astype(BF16)


def _relayout_small_kernel(ga_ref, mg_ref, dt_ref, prev_ref, o_ref, *, bounds):
    lane = lax.broadcasted_iota(jnp.int32, ga_ref.shape, 1)
    b0, b1, b2 = bounds
    sm = jnp.where(lane < b0, ga_ref[...], jnp.where(lane < b1, mg_ref[...], jnp.where(lane < b2, dt_ref[...], 0.0)))
    o_ref[:, 0:SM_W] = sm.astype(BF16)
    o_ref[:, SM_W:] = jnp.zeros((o_ref.shape[0], o_ref.shape[1] - SM_W), BF16)


def _relayout(w_in, segments, small_srcs, small_bounds):
    depth, d, _ = w_in.shape
    out = None
    for dst, src, length in segments:
        shift = src % 128
        base = src - shift
        assert length % _RL_TILE == 0 and base % _RL_TILE == 0 and dst % _RL_TILE == 0
        nt = length // _RL_TILE
        in_specs = [
            pl.BlockSpec((None, d, _RL_TILE), lambda l, k, b=base: (l, 0, b // _RL_TILE + k)),
            pl.BlockSpec((None, d, 128), lambda l, k, b=base: (l, 0, (b + _RL_TILE) // 128 + k * (_RL_TILE // 128))),
        ]
        args = [w_in, w_in]
        aliases = {}
        if out is not None:
            in_specs.append(pl.BlockSpec(memory_space=pl.ANY))
            args.append(out)
            aliases = {2: 0}
        out = pl.pallas_call(
            functools.partial(_relayout_kernel, shift=shift),
            grid=(depth, nt),
            in_specs=in_specs,
            out_specs=pl.BlockSpec((None, d, _RL_TILE), lambda l, k, o=dst: (l, 0, o // _RL_TILE + k)),
            out_shape=jax.ShapeDtypeStruct((depth, d, _N_PROJ), BF16),
            input_output_aliases=aliases,
            compiler_params=pltpu.CompilerParams(dimension_semantics=("parallel", "parallel")),
            name="w_relayout",
        )(*args)
    blk = lambda src: pl.BlockSpec((None, d, 128), lambda l, s=src: (l, 0, s // 128))
    return pl.pallas_call(
        functools.partial(_relayout_small_kernel, bounds=small_bounds),
        grid=(depth,),
        in_specs=[blk(small_srcs[0]), blk(small_srcs[1]), blk(small_srcs[2]), pl.BlockSpec(memory_space=pl.ANY)],
        out_specs=pl.BlockSpec((None, d, _RL_TILE), lambda l: (l, 0, _OFF_SMALL // _RL_TILE)),
        out_shape=jax.ShapeDtypeStruct((depth, d, _N_PROJ), BF16),
        input_output_aliases={3: 0},
        compiler_params=pltpu.CompilerParams(dimension_semantics=("parallel",)),
        name="w_relayout_small",
    )(w_in, w_in, w_in, out)


_D = 1024
_OFF_MLSTM = 3 * _D
_OFF_GATES = 6 * _D
_OFF_SZ = 9 * _D
_OFF_SX = 10 * _D
_OFF_SBC = 11 * _D
_OFF_SMALL = 11 * _D + _D // 2
_N_PROJ = 12 * _D


def _pick(n, cands):
    for c in cands:
        if n % c == 0:
            return c
    return n


def kernel(x, norm_mix, w_in, gla_wa, gla_ba, gla_norm, mlstm_conv_w, mlstm_conv_b, mlstm_bi, mlstm_bf,
           mlstm_norm, ssd_conv_w, ssd_conv_b, ssd_dt_bias, ssd_a_log, ssd_d, ssd_norm, gate_b, w_branch,
           w_out, norm_ffn, w_up, ffn_conv_w, ffn_conv_b, w_down, norm_final):
    bsz, t, d = x.shape
    assert d == _D
    depth = w_in.shape[0]
    m = bsz * t
    qk = d // 2
    rank = gla_wa.shape[1]
    bc_w = 2 * SSD_GROUPS * SSD_STATE
    sizes = (qk, qk, d, rank, d, 2 * qk, d, MLSTM_HEADS, MLSTM_HEADS, d, d, d + bc_w, SSD_HEADS, 3 * d)
    offs = [0]
    for s in sizes:
        offs.append(offs[-1] + s)
    (o_gq, o_gk, o_gv, o_ga, o_gg, o_mqk, o_mv, o_mi, o_mf, o_mo, o_sz, o_sxbc, o_sdt, o_gates, o_end) = offs
    assert o_end == w_in.shape[2]

    segments = ((0, o_gq, o_ga - o_gq), (o_ga - o_gq, o_gg, o_mi - o_gg), (_OFF_MLSTM + 2 * d, o_mo, o_sz - o_mo),
                (_OFF_GATES, o_gates, o_end - o_gates), (_OFF_SZ, o_sz, o_sdt - o_sz))
    assert o_ga % 128 == SM_GA and o_mi % 128 == SM_MI and o_sdt % 128 == SM_DT
    w_perm = _relayout(w_in, segments, (o_ga, o_mi, o_sdt), (SM_MI, SM_DT, SM_DT + SSD_HEADS))

    row3 = lambda a: a[:, None, :]
    col3 = lambda a: a[:, :, None]
    pad_small = lambda a, off: jnp.pad(a, ((0, 0), (off, SM_W - off - a.shape[1])))
    wa_pad = jnp.pad(gla_wa, ((0, 0), (SM_GA, SM_W - SM_GA - rank), (0, 0)))
    wa_hi = wa_pad.astype(BF16)
    wa_lo = (wa_pad - wa_hi.astype(F32)).astype(BF16)
    wa3 = jnp.concatenate([wa_hi, wa_hi, wa_lo], axis=1)
    m_bias = pad_small(mlstm_bi, SM_MI) + pad_small(mlstm_bf, SM_MF)
    s_bias = pad_small(ssd_dt_bias, SM_DT)
    alog = pad_small(ssd_a_log, SM_DT)
    dexp = jnp.repeat(ssd_d, SSD_HEAD_DIM, axis=-1)
    head_of_col = jnp.arange(d) // SSD_HEAD_DIM
    emat1 = (jnp.arange(SM_W)[:, None] == (SM_DT + head_of_col)[None, :]).astype(BF16)
    emat = jnp.concatenate([emat1, emat1], axis=0)
    w_branch_b = w_branch.astype(BF16)
    w_out_b = w_out.astype(BF16)
    w_up_b = w_up.astype(BF16)
    w_down_b = w_down.astype(BF16)
    norm_mix3, norm_ffn3 = row3(norm_mix), row3(norm_ffn)
    gla_ba3, gla_norm3 = row3(gla_ba), row3(gla_norm)
    m_cb3, m_norm3 = row3(mlstm_conv_b), row3(mlstm_norm)
    s_cwx, s_cwbc = ssd_conv_w[:, :, :d], ssd_conv_w[:, :, d:]
    s_cbx, s_cbbc = row3(ssd_conv_b[:, :d]), row3(ssd_conv_b[:, d:])
    s_norm3, dexp3, gate_b3, f_cb3 = row3(ssd_norm), row3(dexp), row3(gate_b), row3(ffn_conv_b)

    tm_proj = _pick(m, (1024, 512, 256, 128, 64))
    tn_proj = _pick(_N_PROJ, (3072, 2048, 1024))
    tb = _pick(t, (512, 256, 128, 64))
    tm_merge = _pick(m, (512, 256, 128, 64))
    tm_ffn = _pick(t, (512, 256, 128, 64))

    h = x.reshape(m, d)
    for l in range(depth):
        proj2, small2 = _proj(h, norm_mix3, w_perm, l, tm_proj, tn_proj)
        proj3 = proj2.reshape(bsz, t, _N_PROJ)
        small3 = small2.reshape(bsz, t, SM_W)
        y_gla = _gla(proj3, small3, wa3, gla_ba3, gla_norm3, l, tb)
        y_m = _mlstm(proj3, small3, mlstm_conv_w, m_cb3, row3(m_bias), col3(m_bias), m_norm3, l, tb)
        y_s = _ssd(proj3, small3, s_cwx, s_cbx, s_cwbc, s_cbbc, row3(s_bias), col3(s_bias), row3(alog),
                   col3(alog), dexp3, s_norm3, emat, l, tb)
        h = _merge(y_gla.reshape(m, d), y_m.reshape(m, d), y_s.reshape(m, d), proj2, gate_b3,
                   w_branch_b, w_out_b, h, l, tm_merge)
        h = _ffn(h.reshape(bsz, t, d), norm_ffn3, w_up_b, ffn_conv_w, f_cb3, w_down_b, norm_final[None, :], l,
                 tm_ffn, norm_out=(l == depth - 1)).reshape(m, d)
    return h.reshape(bsz, t, d)
```

```python
import functools

import jax
import jax.numpy as jnp
from jax import lax
from jax.experimental import pallas as pl
from jax.experimental.pallas import tpu as pltpu

F32 = jnp.float32
BF16 = jnp.bfloat16

EPS = 1e-6
CHUNK = 64
GLA_HEADS = 4
GLA_TAU = 16.0
MLSTM_HEADS = 4
SSD_HEADS = 16
SSD_GROUPS = 2
SSD_STATE = 128
SSD_HEAD_DIM = 64
FFN_CONV = 3
CONV_K = 4
PAD = 16

SM_GA = 0
SM_MI = 16
SM_MF = 20
SM_DT = 24
SM_W = 128

VMEM_LIMIT = 52 * 1024 * 1024


def _dot(a, b):
    return jnp.dot(a.astype(BF16), b.astype(BF16), preferred_element_type=F32)


def _dot_nt(a, b):
    return lax.dot_general(a.astype(BF16), b.astype(BF16), (((1,), (1,)), ((), ())),
                           preferred_element_type=F32)


def _dot_tn(a, b):
    return jnp.dot(a.astype(F32).T.astype(BF16), b.astype(BF16), preferred_element_type=F32)


def _split2(x):
    hi = x.astype(BF16)
    return hi, (x - hi.astype(F32)).astype(BF16)


def _cumsum_rows(tril2, x):
    return jnp.dot(tril2, jnp.concatenate(_split2(x), axis=0), preferred_element_type=F32)


def _cumsum_cols(x, triu2):
    return jnp.dot(jnp.concatenate(_split2(x), axis=1), triu2, preferred_element_type=F32)


def _sigmoid(x):
    return 1.0 / (1.0 + jnp.exp(-x))


def _silu(x):
    return x * _sigmoid(x)


def _softplus(x):
    return jnp.maximum(x, 0.0) + jnp.log1p(jnp.exp(-jnp.abs(x)))


def _log_sigmoid(x):
    return jnp.minimum(x, 0.0) - jnp.log1p(jnp.exp(-jnp.abs(x)))


def _causal():
    row = lax.broadcasted_iota(jnp.int32, (CHUNK, CHUNK), 0)
    col = lax.broadcasted_iota(jnp.int32, (CHUNK, CHUNK), 1)
    return col <= row


def _tril2():
    row = lax.broadcasted_iota(jnp.int32, (CHUNK, 2 * CHUNK), 0)
    col = lax.broadcasted_iota(jnp.int32, (CHUNK, 2 * CHUNK), 1)
    return jnp.where((col <= row) | ((col >= CHUNK) & (col - CHUNK <= row)), 1.0, 0.0).astype(BF16)


def _triu2():
    row = lax.broadcasted_iota(jnp.int32, (2 * CHUNK, CHUNK), 0)
    col = lax.broadcasted_iota(jnp.int32, (2 * CHUNK, CHUNK), 1)
    return jnp.where((row <= col) | ((row >= CHUNK) & (row - CHUNK <= col)), 1.0, 0.0).astype(BF16)


def _shift_mat(taps):
    row = lax.broadcasted_iota(jnp.int32, (taps * CHUNK, PAD + CHUNK), 0)
    col = lax.broadcasted_iota(jnp.int32, (taps * CHUNK, PAD + CHUNK), 1)
    sel = None
    for j in range(taps):
        hit = (row >= j * CHUNK) & (row < (j + 1) * CHUNK) & (col == row - j * CHUNK + PAD - j)
        sel = hit if sel is None else sel | hit
    return jnp.where(sel, 1.0, 0.0).astype(BF16)


def _layer_spec(arr, l):
    zeros = (0,) * (arr.ndim - 1)
    return pl.BlockSpec((None,) + arr.shape[1:], lambda *_: (l,) + zeros)


def _history_spec(tb, width, col_block):
    per = tb // PAD
    return pl.BlockSpec((1, PAD, width), lambda b, i: (b, jnp.maximum(i * per - 1, 0), col_block))


def _conv_window(x_ref, hist_ref, c):
    if c > 0:
        return x_ref[0, c * CHUNK - PAD:(c + 1) * CHUNK, :]
    hist = hist_ref[0]
    hist = jnp.where(pl.program_id(1) > 0, hist, jnp.zeros_like(hist))
    return jnp.concatenate([hist, x_ref[0, 0:CHUNK, :]], axis=0)


def _proj_kernel(x_ref, g_ref, w_ref, o_ref, sm_ref, xn_ref, *, sm_tile, sm_col):
    j = pl.program_id(1)

    @pl.when(j == 0)
    def _():
        x = x_ref[...]
        ms = jnp.mean(x * x, axis=-1, keepdims=True)
        xn_ref[...] = (x * lax.rsqrt(ms + EPS) * g_ref[...]).astype(BF16)

    acc = lax.dot_general(xn_ref[...], w_ref[...], (((1,), (1,)), ((), ())), preferred_element_type=F32)
    o_ref[...] = acc.astype(o_ref.dtype)

    @pl.when(j == sm_tile)
    def _():
        sm_ref[...] = acc[:, sm_col:sm_col + SM_W]


def _proj(x, g, w, l, tm, tn):
    m, d = x.shape
    n = w.shape[1]
    return pl.pallas_call(
        functools.partial(_proj_kernel, sm_tile=_OFF_SMALL // tn, sm_col=_OFF_SMALL % tn),
        grid=(m // tm, n // tn),
        in_specs=[
            pl.BlockSpec((tm, d), lambda i, j: (i, 0)),
            _layer_spec(g, l),
            pl.BlockSpec((None, tn, d), lambda i, j: (l, j, 0)),
        ],
        out_specs=[
            pl.BlockSpec((tm, tn), lambda i, j: (i, j)),
            pl.BlockSpec((tm, SM_W), lambda i, j: (i, 0)),
        ],
        out_shape=[jax.ShapeDtypeStruct((m, n), BF16), jax.ShapeDtypeStruct((m, SM_W), F32)],
        scratch_shapes=[pltpu.VMEM((tm, d), BF16)],
        compiler_params=pltpu.CompilerParams(
            dimension_semantics=("parallel", "arbitrary"), vmem_limit_bytes=VMEM_LIMIT),
        name="in_proj",
    )(x, g, w)


def _gla_kernel(q_ref, k_ref, v_ref, gg_ref, sm_ref, wa_ref, ba_ref, nw_ref, y_ref, st_ref, *, tb):
    dk = q_ref.shape[2] // GLA_HEADS
    dv = v_ref.shape[2] // GLA_HEADS

    @pl.when(pl.program_id(1) == 0)
    def _():
        st_ref[...] = jnp.zeros_like(st_ref)

    chunks = range(tb // CHUNK)
    heads = range(GLA_HEADS)
    causal = _causal()
    tril2 = _tril2()
    scale = dk ** -0.5
    rows = [slice(c * CHUNK, (c + 1) * CHUNK) for c in chunks]

    pre = []
    for c in chunks:
        hi, lo = _split2(sm_ref[0, rows[c], :])
        pre.append(jnp.dot(jnp.concatenate([hi, lo, hi], axis=1), wa_ref[...], preferred_element_type=F32))
    la = [_log_sigmoid(pre[c] + ba_ref[...]) * (1.0 / GLA_TAU) for c in chunks]
    bcum = [_cumsum_rows(tril2, la[c]) for c in chunks]

    for c in chunks:
        r = rows[c]
        kcs = [slice(h * dk, (h + 1) * dk) for h in heads]
        vcs = [slice(h * dv, (h + 1) * dv) for h in heads]
        b = [bcum[c][:, kcs[h]] for h in heads]
        bl = [b[h][CHUNK - 1:CHUNK, :] for h in heads]
        q = [q_ref[0, r, kcs[h]].astype(F32) * scale for h in heads]
        k = [k_ref[0, r, kcs[h]].astype(F32) for h in heads]
        v = [v_ref[0, r, vcs[h]] for h in heads]
        qd = [q[h] * jnp.exp(b[h]) for h in heads]
        ki = [k[h] * jnp.exp(-b[h]) for h in heads]
        kd = [k[h] * jnp.exp(bl[h] - b[h]) for h in heads]
        st = [st_ref[h] for h in heads]
        sc = [_dot_nt(qd[h], ki[h]) for h in heads]
        o2 = [_dot_nt(qd[h], st[h]) for h in heads]
        ds = [_dot_tn(v[h], kd[h]) for h in heads]
        o1 = [_dot(jnp.where(causal, sc[h], 0.0), v[h]) for h in heads]
        for h in heads:
            st_ref[h] = st[h] * jnp.exp(bl[h]) + ds[h]
            o = o1[h] + o2[h]
            ms = jnp.mean(o * o, axis=-1, keepdims=True)
            y = (o * lax.rsqrt(ms + EPS) * nw_ref[:, vcs[h]]) * _silu(gg_ref[0, r, vcs[h]].astype(F32))
            y_ref[0, r, vcs[h]] = y.astype(y_ref.dtype)


def _gla(proj, small, wa3, ba, nw, l, tb):
    bsz, t, _ = proj.shape
    qk_w = ba.shape[2]
    v_w = nw.shape[2]
    dk = qk_w // GLA_HEADS
    dv = v_w // GLA_HEADS
    return pl.pallas_call(
        functools.partial(_gla_kernel, tb=tb),
        grid=(bsz, t // tb),
        in_specs=[
            pl.BlockSpec((1, tb, qk_w), lambda b, i: (b, i, 0)),
            pl.BlockSpec((1, tb, qk_w), lambda b, i: (b, i, 1)),
            pl.BlockSpec((1, tb, v_w), lambda b, i: (b, i, 1)),
            pl.BlockSpec((1, tb, v_w), lambda b, i: (b, i, 2)),
            pl.BlockSpec((1, tb, SM_W), lambda b, i: (b, i, 0)),
            _layer_spec(wa3, l),
            _layer_spec(ba, l),
            _layer_spec(nw, l),
        ],
        out_specs=pl.BlockSpec((1, tb, v_w), lambda b, i: (b, i, 0)),
        out_shape=jax.ShapeDtypeStruct((bsz, t, v_w), BF16),
        scratch_shapes=[pltpu.VMEM((GLA_HEADS, dv, dk), F32)],
        compiler_params=pltpu.CompilerParams(
            dimension_semantics=("parallel", "arbitrary"), vmem_limit_bytes=VMEM_LIMIT),
        name="gla",
    )(proj, proj, proj, proj, small, wa3, ba, nw)


def _mlstm_kernel(qk_ref, qkh_ref, v_ref, og_ref, sm_ref, cw_ref, cb_ref, brow_ref, bcol_ref, nw_ref, y_ref,
                  c_ref, n_ref, m_ref, *, tb):
    qk_w = qk_ref.shape[2] // 2
    dk = qk_w // MLSTM_HEADS
    dv = v_ref.shape[2] // MLSTM_HEADS

    @pl.when(pl.program_id(1) == 0)
    def _():
        c_ref[...] = jnp.zeros_like(c_ref)
        n_ref[...] = jnp.zeros_like(n_ref)
        m_ref[...] = jnp.zeros_like(m_ref)

    chunks = range(tb // CHUNK)
    heads = range(MLSTM_HEADS)
    pairs = [(c, h) for c in chunks for h in heads]
    causal = _causal()
    tril2 = _tril2()
    triu2 = _triu2()
    shift = _shift_mat(CONV_K)
    kscale = dk ** -0.5
    neg_inf = -jnp.inf
    rows = [slice(c * CHUNK, (c + 1) * CHUNK) for c in chunks]

    sm = [sm_ref[0, rows[c], :] for c in chunks]
    pre_col = [sm[c] + brow_ref[...] for c in chunks]
    pre_row = [(sm[c].T + bcol_ref[...])[SM_MI:SM_MI + 2 * MLSTM_HEADS] for c in chunks]
    b_col = [_cumsum_rows(tril2, _log_sigmoid(pre_col[c])) for c in chunks]
    b_row = [_cumsum_cols(_log_sigmoid(pre_row[c]), triu2) for c in chunks]
    xs = [jnp.dot(shift, _conv_window(qk_ref, qkh_ref, c), preferred_element_type=F32) for c in chunks]
    act = []
    for c in chunks:
        acc = cb_ref[...]
        for k in range(CONV_K):
            j = CONV_K - 1 - k
            acc = acc + cw_ref[k:k + 1, :] * xs[c][j * CHUNK:(j + 1) * CHUNK]
        act.append(_silu(acc))

    bc, br, ir, bl, m_loc, w = {}, {}, {}, {}, {}, {}
    for c, h in pairs:
        bc[c, h] = b_col[c][:, SM_MF + h:SM_MF + h + 1]
        br[c, h] = b_row[c][MLSTM_HEADS + h:MLSTM_HEADS + h + 1, :]
        ir[c, h] = pre_row[c][h:h + 1, :]
        bl[c, h] = bc[c, h][CHUNK - 1:CHUNK, :]
        a = bl[c, h] - bc[c, h] + pre_col[c][:, SM_MI + h:SM_MI + h + 1]
        m_loc[c, h] = jnp.max(a, axis=0, keepdims=True)
        w[c, h] = jnp.exp(a - m_loc[c, h])
    m_prev, s_old, s_new = {}, {}, {}
    for h in heads:
        m = m_ref[h][0:1, 0:1]
        for c in chunks:
            m_prev[c, h] = m
            m_new = jnp.maximum(bl[c, h] + m, m_loc[c, h])
            s_old[c, h] = jnp.exp(bl[c, h] + m - m_new)
            s_new[c, h] = jnp.exp(m_loc[c, h] - m_new)
            m = m_new
        m_ref[h] = jnp.broadcast_to(m, m_ref.shape[1:])

    q = {(c, h): act[c][:, h * dk:(h + 1) * dk] for c, h in pairs}
    k = {(c, h): act[c][:, qk_w + h * dk:qk_w + (h + 1) * dk] * kscale for c, h in pairs}
    v = {(c, h): v_ref[0, rows[c], h * dv:(h + 1) * dv] for c, h in pairs}
    wk = {p: w[p] * k[p] for p in pairs}

    qk = {p: _dot_nt(q[p], k[p]) for p in pairs}
    dcs = {p: _dot_tn(wk[p], v[p]) for p in pairs}

    c_prev, n_prev = {}, {}
    for h in heads:
        cst = c_ref[h]
        nst = n_ref[h][0:1, :]
        for c in chunks:
            c_prev[c, h] = cst.astype(BF16)
            n_prev[c, h] = nst
            cst = s_old[c, h] * cst + s_new[c, h] * dcs[c, h]
            nst = s_old[c, h] * nst + s_new[c, h] * jnp.sum(wk[c, h], axis=0, keepdims=True)
        c_ref[h] = cst
        n_ref[h] = jnp.broadcast_to(nst, n_ref.shape[1:])

    log_d = {p: jnp.where(causal, bc[p] - br[p] + ir[p], neg_inf) for p in pairs}
    m_inter = {p: bc[p] + m_prev[p] for p in pairs}
    row_max = {p: jnp.max(log_d[p], axis=-1, keepdims=True) for p in pairs}
    m_t = {p: jnp.maximum(m_inter[p], row_max[p]) for p in pairs}
    wts = {p: qk[p] * jnp.exp(log_d[p] - m_t[p]) for p in pairs}
    s_inter = {p: jnp.exp(m_inter[p] - m_t[p]) for p in pairs}

    pv = {p: _dot(wts[p], v[p]) for p in pairs}
    qc = {p: _dot(q[p], c_prev[p]) for p in pairs}

    row_sum = {p: jnp.sum(wts[p], axis=-1, keepdims=True) for p in pairs}
    qn = {p: jnp.sum(q[p] * n_prev[p], axis=-1, keepdims=True) for p in pairs}
    num = {p: pv[p] + s_inter[p] * qc[p] for p in pairs}
    den = {p: row_sum[p] + s_inter[p] * qn[p] for p in pairs}
    hout = {p: num[p] / jnp.maximum(jnp.abs(den[p]), jnp.exp(-m_t[p])) for p in pairs}
    ms = {p: jnp.mean(hout[p] * hout[p], axis=-1, keepdims=True) for p in pairs}
    for c, h in pairs:
        p = (c, h)
        vc = slice(h * dv, (h + 1) * dv)
        y = (hout[p] * lax.rsqrt(ms[p] + EPS) * nw_ref[:, vc]) * _sigmoid(og_ref[0, rows[c], vc].astype(F32))
        y_ref[0, rows[c], vc] = y.astype(y_ref.dtype)


def _mlstm(proj, small, cw, cb, brow, bcol, nw, l, tb):
    bsz, t, _ = proj.shape
    qk2 = cw.shape[2]
    v_w = nw.shape[2]
    dk = qk2 // 2 // MLSTM_HEADS
    dv = v_w // MLSTM_HEADS
    base = _OFF_MLSTM // v_w
    return pl.pallas_call(
        functools.partial(_mlstm_kernel, tb=tb),
        grid=(bsz, t // tb),
        in_specs=[
            pl.BlockSpec((1, tb, qk2), lambda b, i: (b, i, base)),
            _history_spec(tb, qk2, base),
            pl.BlockSpec((1, tb, v_w), lambda b, i: (b, i, base + 1)),
            pl.BlockSpec((1, tb, v_w), lambda b, i: (b, i, base + 2)),
            pl.BlockSpec((1, tb, SM_W), lambda b, i: (b, i, 0)),
            _layer_spec(cw, l),
            _layer_spec(cb, l),
            _layer_spec(brow, l),
            _layer_spec(bcol, l),
            _layer_spec(nw, l),
        ],
        out_specs=pl.BlockSpec((1, tb, v_w), lambda b, i: (b, i, 0)),
        out_shape=jax.ShapeDtypeStruct((bsz, t, v_w), BF16),
        scratch_shapes=[
            pltpu.VMEM((MLSTM_HEADS, dk, dv), F32),
            pltpu.VMEM((MLSTM_HEADS, 8, dk), F32),
            pltpu.VMEM((MLSTM_HEADS, 8, 128), F32),
        ],
        compiler_params=pltpu.CompilerParams(
            dimension_semantics=("parallel", "arbitrary"), vmem_limit_bytes=VMEM_LIMIT),
        name="mlstm",
    )(proj, proj, proj, proj, small, cw, cb, brow, bcol, nw)


def _ssd_kernel(z_ref, x_ref, xh_ref, bc_ref, bch_ref, sm_ref, cwx_ref, cbx_ref, cwbc_ref, cbbc_ref, brow_ref,
                bcol_ref, alog_row_ref, alog_col_ref, dexp_ref, nw_ref, e_ref, y_ref,
                s_ref, ysc_ref, *, tb):
    inner = x_ref.shape[2]
    gw = inner // SSD_GROUPS
    hpg = SSD_HEADS // SSD_GROUPS
    p = SSD_HEAD_DIM
    z = SSD_STATE

    @pl.when(pl.program_id(1) == 0)
    def _():
        s_ref[...] = jnp.zeros_like(s_ref)

    chunks = range(tb // CHUNK)
    groups = range(SSD_GROUPS)
    causal = _causal()
    tril2 = _tril2()
    triu2 = _triu2()
    shift = _shift_mat(CONV_K)
    neg_inf = -jnp.inf
    a_row = -jnp.exp(alog_row_ref[...])
    a_col = -jnp.exp(alog_col_ref[...])[SM_DT:SM_DT + SSD_HEADS]
    rows = [slice(c * CHUNK, (c + 1) * CHUNK) for c in chunks]

    def conv(xs, cw, cb):
        acc = cb
        for k in range(CONV_K):
            j = CONV_K - 1 - k
            acc = acc + cw[k:k + 1, :] * xs[j * CHUNK:(j + 1) * CHUNK]
        return _silu(acc)

    sm = [sm_ref[0, rows[c], :] for c in chunks]
    dt_col = [_softplus(sm[c] + brow_ref[...]) for c in chunks]
    dt_row = [_softplus((sm[c].T + bcol_ref[...])[SM_DT:SM_DT + SSD_HEADS]) for c in chunks]
    b_col = [_cumsum_rows(tril2, dt_col[c] * a_row) for c in chunks]
    b_row = [_cumsum_cols(dt_row[c] * a_col, triu2) for c in chunks]
    xsx = [jnp.dot(shift, _conv_window(x_ref, xh_ref, c), preferred_element_type=F32) for c in chunks]
    xsb = [jnp.dot(shift, _conv_window(bc_ref, bch_ref, c), preferred_element_type=F32) for c in chunks]
    expd = []
    for c in chunks:
        bl_row = b_col[c][CHUNK - 1:CHUNK, :]
        w_col = jnp.exp(bl_row - b_col[c]) * dt_col[c]
        eb_col = jnp.exp(b_col[c])
        ebl = jnp.broadcast_to(jnp.exp(bl_row), (8, SM_W))
        hi, lo = _split2(jnp.concatenate([w_col, eb_col, ebl], axis=0))
        expd.append(jnp.dot(jnp.concatenate([hi, lo], axis=1), e_ref[...], preferred_element_type=F32))
    xa = [conv(xsx[c], cwx_ref[...], cbx_ref[...]) for c in chunks]
    bca = [conv(xsb[c], cwbc_ref[...], cbbc_ref[...]) for c in chunks]

    for c in chunks:
        w_x = expd[c][0:CHUNK]
        eb_x = expd[c][CHUNK:2 * CHUNK]
        ebl_x = expd[c][2 * CHUNK:2 * CHUNK + 1]
        xg = [xa[c][:, g * gw:(g + 1) * gw] for g in groups]
        bg = [bca[c][:, g * z:(g + 1) * z] for g in groups]
        cg = [bca[c][:, SSD_GROUPS * z + g * z:SSD_GROUPS * z + (g + 1) * z] for g in groups]
        cb = [_dot_nt(cg[g], bg[g]) for g in groups]
        s_prev = [s_ref[g] for g in groups]
        inter = [_dot(cg[g], s_prev[g]) for g in groups]
        dst = [_dot_tn(bg[g], xg[g] * w_x[:, g * gw:(g + 1) * gw]) for g in groups]
        mixes = []
        for h in range(SSD_HEADS):
            bc = b_col[c][:, SM_DT + h:SM_DT + h + 1]
            br = b_row[c][h:h + 1, :]
            dtr = dt_row[c][h:h + 1, :]
            mixes.append(cb[h // hpg] * jnp.exp(jnp.where(causal, bc - br, neg_inf)) * dtr)
        ys = [_dot(mixes[h], xg[h // hpg][:, (h % hpg) * p:(h % hpg + 1) * p]) for h in range(SSD_HEADS)]
        for h in range(SSD_HEADS):
            ysc_ref[:, h * p:(h + 1) * p] = ys[h]
        for g in groups:
            gc = slice(g * gw, (g + 1) * gw)
            s_ref[g] = s_prev[g] * ebl_x[:, gc] + dst[g]
            y = ysc_ref[:, gc] + inter[g] * eb_x[:, gc] + dexp_ref[:, gc] * xg[g]
            y = y * _silu(z_ref[0, rows[c], gc].astype(F32))
            ms = jnp.mean(y * y, axis=-1, keepdims=True)
            y_ref[0, rows[c], gc] = (y * lax.rsqrt(ms + EPS) * nw_ref[:, gc]).astype(y_ref.dtype)


def _ssd(proj, small, cwx, cbx, cwbc, cbbc, brow, bcol, alog_row, alog_col, dexp, nw, emat, l, tb):
    bsz, t, _ = proj.shape
    inner = nw.shape[2]
    bcw = cwbc.shape[2]
    return pl.pallas_call(
        functools.partial(_ssd_kernel, tb=tb),
        grid=(bsz, t // tb),
        in_specs=[
            pl.BlockSpec((1, tb, inner), lambda b, i: (b, i, _OFF_SZ // inner)),
            pl.BlockSpec((1, tb, inner), lambda b, i: (b, i, _OFF_SX // inner)),
            _history_spec(tb, inner, _OFF_SX // inner),
            pl.BlockSpec((1, tb, bcw), lambda b, i: (b, i, _OFF_SBC // bcw)),
            _history_spec(tb, bcw, _OFF_SBC // bcw),
            pl.BlockSpec((1, tb, SM_W), lambda b, i: (b, i, 0)),
            _layer_spec(cwx, l),
            _layer_spec(cbx, l),
            _layer_spec(cwbc, l),
            _layer_spec(cbbc, l),
            _layer_spec(brow, l),
            _layer_spec(bcol, l),
            _layer_spec(alog_row, l),
            _layer_spec(alog_col, l),
            _layer_spec(dexp, l),
            _layer_spec(nw, l),
            pl.BlockSpec(emat.shape, lambda b, i: (0, 0)),
        ],
        out_specs=pl.BlockSpec((1, tb, inner), lambda b, i: (b, i, 0)),
        out_shape=jax.ShapeDtypeStruct((bsz, t, inner), BF16),
        scratch_shapes=[
            pltpu.VMEM((SSD_GROUPS, SSD_STATE, inner // SSD_GROUPS), F32),
            pltpu.VMEM((CHUNK, inner), F32),
        ],
        compiler_params=pltpu.CompilerParams(
            dimension_semantics=("parallel", "arbitrary"), vmem_limit_bytes=VMEM_LIMIT),
        name="ssd",
    )(proj, proj, proj, proj, proj, small, cwx, cbx, cwbc, cbbc, brow, bcol, alog_row, alog_col, dexp, nw, emat)


def _merge_kernel(ya_ref, yb_ref, yc_ref, gates_ref, gb_ref, wb_ref, wo_ref, h_ref, o_ref):
    d = h_ref.shape[1]
    acc = None
    for k, y_ref in enumerate((ya_ref, yb_ref, yc_ref)):
        zk = jnp.dot(y_ref[...], wb_ref[k], preferred_element_type=F32)
        gk = _sigmoid(gates_ref[:, k * d:(k + 1) * d].astype(F32) + gb_ref[:, k * d:(k + 1) * d])
        acc = gk * zk if acc is None else acc + gk * zk
    o_ref[...] = h_ref[...] + jnp.dot(acc.astype(BF16), wo_ref[...], preferred_element_type=F32)


def _merge(ya, yb, yc, proj2, gate_b, w_branch, w_out, h, l, tm):
    m, d = h.shape
    nb = w_branch.shape[1]
    return pl.pallas_call(
        _merge_kernel,
        grid=(m // tm,),
        in_specs=[
            pl.BlockSpec((tm, d), lambda i: (i, 0)),
            pl.BlockSpec((tm, d), lambda i: (i, 0)),
            pl.BlockSpec((tm, d), lambda i: (i, 0)),
            pl.BlockSpec((tm, nb * d), lambda i: (i, _OFF_GATES // (nb * d))),
            _layer_spec(gate_b, l),
            _layer_spec(w_branch, l),
            _layer_spec(w_out, l),
            pl.BlockSpec((tm, d), lambda i: (i, 0)),
        ],
        out_specs=pl.BlockSpec((tm, d), lambda i: (i, 0)),
        out_shape=jax.ShapeDtypeStruct((m, d), F32),
        compiler_params=pltpu.CompilerParams(
            dimension_semantics=("parallel",), vmem_limit_bytes=VMEM_LIMIT),
        name="merge",
    )(ya, yb, yc, proj2, gate_b, w_branch, w_out, h)


def _ffn_kernel(h_ref, g_ref, wg_ref, wv_ref, cwg_ref, cwv_ref, cbg_ref, cbv_ref, wd_ref, gout_ref, o_ref,
                gpad_ref, vpad_ref, *, tm, norm_out):
    @pl.when(pl.program_id(1) == 0)
    def _():
        gpad_ref[0:8, :] = jnp.zeros((8, gpad_ref.shape[1]), F32)
        vpad_ref[0:8, :] = jnp.zeros((8, vpad_ref.shape[1]), F32)

    x = h_ref[0]
    ms = jnp.mean(x * x, axis=-1, keepdims=True)
    xn = (x * lax.rsqrt(ms + EPS) * g_ref[...]).astype(BF16)
    gpad_ref[8:8 + tm, :] = jnp.dot(xn, wg_ref[...], preferred_element_type=F32)
    vpad_ref[8:8 + tm, :] = jnp.dot(xn, wv_ref[...], preferred_element_type=F32)
    gate = cbg_ref[...]
    val = cbv_ref[...]
    for k in range(FFN_CONV):
        s = FFN_CONV - 1 - k
        gate = gate + cwg_ref[k:k + 1, :] * gpad_ref[8 - s:8 - s + tm, :]
        val = val + cwv_ref[k:k + 1, :] * vpad_ref[8 - s:8 - s + tm, :]
    gpad_ref[0:8, :] = gpad_ref[tm:tm + 8, :]
    vpad_ref[0:8, :] = vpad_ref[tm:tm + 8, :]
    act = (_silu(gate) * val).astype(BF16)
    r = x + jnp.dot(act, wd_ref[...], preferred_element_type=F32)
    if norm_out:
        ms = jnp.mean(r * r, axis=-1, keepdims=True)
        r = r * lax.rsqrt(ms + EPS) * gout_ref[...]
    o_ref[0] = r


def _ffn(h3, g, w_up, cw, cb, w_down, g_out, l, tm, norm_out):
    bsz, t, d = h3.shape
    dff = w_down.shape[1]
    resident = dict(pipeline_mode=pl.Buffered(1))
    return pl.pallas_call(
        functools.partial(_ffn_kernel, tm=tm, norm_out=norm_out),
        grid=(bsz, t // tm),
        in_specs=[
            pl.BlockSpec((1, tm, d), lambda b, i: (b, i, 0)),
            _layer_spec(g, l),
            pl.BlockSpec((None, d, dff), lambda b, i: (l, 0, 0), **resident),
            pl.BlockSpec((None, d, dff), lambda b, i: (l, 0, 1), **resident),
            pl.BlockSpec((None, FFN_CONV, dff), lambda b, i: (l, 0, 0)),
            pl.BlockSpec((None, FFN_CONV, dff), lambda b, i: (l, 0, 1)),
            pl.BlockSpec((None, 1, dff), lambda b, i: (l, 0, 0)),
            pl.BlockSpec((None, 1, dff), lambda b, i: (l, 0, 1)),
            pl.BlockSpec((None, dff, d), lambda b, i: (l, 0, 0), **resident),
            pl.BlockSpec((1, d), lambda b, i: (0, 0)),
        ],
        out_specs=pl.BlockSpec((1, tm, d), lambda b, i: (b, i, 0)),
        out_shape=jax.ShapeDtypeStruct((bsz, t, d), F32),
        scratch_shapes=[
            pltpu.VMEM((tm + 8, dff), F32),
            pltpu.VMEM((tm + 8, dff), F32),
        ],
        compiler_params=pltpu.CompilerParams(
            dimension_semantics=("parallel", "arbitrary"), vmem_limit_bytes=VMEM_LIMIT),
        name="conv_ffn",
    )(h3, g, w_up, w_up, cw, cw, cb, cb, w_down, g_out)


_RL_TILE = 512


def _relayout_kernel(*refs):
    refs[-1][...] = refs[0][0].astype(BF16)


def _relayout_small_kernel(ga_ref, mg_ref, dt_ref, prev_ref, o_ref):
    rows = ga_ref.shape[1] + mg_ref.shape[1] + dt_ref.shape[1]
    sm = jnp.concatenate([ga_ref[0], mg_ref[0], dt_ref[0],
                          jnp.zeros((SM_W - rows, o_ref.shape[1]), F32)], axis=0)
    o_ref[0:SM_W, :] = sm.astype(BF16)
    o_ref[SM_W:, :] = jnp.zeros((o_ref.shape[0] - SM_W, o_ref.shape[1]), BF16)


def _relayout(w_t, segments, small_srcs):
    depth, _, d = w_t.shape
    out = None
    for dst, src, length in segments:
        assert length % _RL_TILE == 0 and dst % _RL_TILE == 0 and src % 8 == 0
        in_specs = [pl.BlockSpec((pl.Element(1), pl.Element(_RL_TILE), pl.Element(d)),
                                 lambda l, k, s=src: (l, pl.multiple_of(s + k * _RL_TILE, 8), 0))]
        args = [w_t]
        aliases = {}
        if out is not None:
            in_specs.append(pl.BlockSpec(memory_space=pl.ANY))
            args.append(out)
            aliases = {1: 0}
        out = pl.pallas_call(
            _relayout_kernel,
            grid=(depth, length // _RL_TILE),
            in_specs=in_specs,
            out_specs=pl.BlockSpec((None, _RL_TILE, d), lambda l, k, o=dst: (l, o // _RL_TILE + k, 0)),
            out_shape=jax.ShapeDtypeStruct((depth, _N_PROJ, d), BF16),
            input_output_aliases=aliases,
            compiler_params=pltpu.CompilerParams(dimension_semantics=("parallel", "parallel")),
            name="w_relayout",
        )(*args)
    blk = lambda src, n: pl.BlockSpec((pl.Element(1), pl.Element(n), pl.Element(d)), lambda l, s=src: (l, s, 0))
    return pl.pallas_call(
        _relayout_small_kernel,
        grid=(depth,),
        in_specs=[blk(*small_srcs[0]), blk(*small_srcs[1]), blk(*small_srcs[2]), pl.BlockSpec(memory_space=pl.ANY)],
        out_specs=pl.BlockSpec((None, _RL_TILE, d), lambda l: (l, _OFF_SMALL // _RL_TILE, 0)),
        out_shape=jax.ShapeDtypeStruct((depth, _N_PROJ, d), BF16),
        input_output_aliases={3: 0},
        compiler_params=pltpu.CompilerParams(dimension_semantics=("parallel",)),
        name="w_relayout_small",
    )(w_t, w_t, w_t, out)


_D = 1024
_OFF_MLSTM = 3 * _D
_OFF_GATES = 6 * _D
_OFF_SZ = 9 * _D
_OFF_SX = 10 * _D
_OFF_SBC = 11 * _D
_OFF_SMALL = 11 * _D + _D // 2
_N_PROJ = 12 * _D


def _pick(n, cands):
    for c in cands:
        if n % c == 0:
            return c
    return n


def kernel(x, norm_mix, w_in, gla_wa, gla_ba, gla_norm, mlstm_conv_w, mlstm_conv_b, mlstm_bi, mlstm_bf,
           mlstm_norm, ssd_conv_w, ssd_conv_b, ssd_dt_bias, ssd_a_log, ssd_d, ssd_norm, gate_b, w_branch,
           w_out, norm_ffn, w_up, ffn_conv_w, ffn_conv_b, w_down, norm_final):
    bsz, t, d = x.shape
    assert d == _D
    depth = w_in.shape[0]
    m = bsz * t
    qk = d // 2
    rank = gla_wa.shape[1]
    bc_w = 2 * SSD_GROUPS * SSD_STATE
    sizes = (qk, qk, d, rank, d, 2 * qk, d, MLSTM_HEADS, MLSTM_HEADS, d, d, d + bc_w, SSD_HEADS, 3 * d)
    offs = [0]
    for s in sizes:
        offs.append(offs[-1] + s)
    (o_gq, o_gk, o_gv, o_ga, o_gg, o_mqk, o_mv, o_mi, o_mf, o_mo, o_sz, o_sxbc, o_sdt, o_gates, o_end) = offs
    assert o_end == w_in.shape[2]

    segments = ((0, o_gq, o_ga - o_gq), (o_ga - o_gq, o_gg, o_mi - o_gg), (_OFF_MLSTM + 2 * d, o_mo, o_sz - o_mo),
                (_OFF_GATES, o_gates, o_end - o_gates), (_OFF_SZ, o_sz, o_sdt - o_sz))
    assert (SM_GA, SM_MI, SM_DT) == (0, rank, rank + 2 * MLSTM_HEADS)
    w_perm = _relayout(jnp.swapaxes(w_in, 1, 2), segments,
                       ((o_ga, rank), (o_mi, 2 * MLSTM_HEADS), (o_sdt, SSD_HEADS)))

    row3 = lambda a: a[:, None, :]
    col3 = lambda a: a[:, :, None]
    pad_small = lambda a, off: jnp.pad(a, ((0, 0), (off, SM_W - off - a.shape[1])))
    wa_pad = jnp.pad(gla_wa, ((0, 0), (SM_GA, SM_W - SM_GA - rank), (0, 0)))
    wa_hi = wa_pad.astype(BF16)
    wa_lo = (wa_pad - wa_hi.astype(F32)).astype(BF16)
    wa3 = jnp.concatenate([wa_hi, wa_hi, wa_lo], axis=1)
    m_bias = pad_small(mlstm_bi, SM_MI) + pad_small(mlstm_bf, SM_MF)
    s_bias = pad_small(ssd_dt_bias, SM_DT)
    alog = pad_small(ssd_a_log, SM_DT)
    dexp = jnp.repeat(ssd_d, SSD_HEAD_DIM, axis=-1)
    head_of_col = jnp.arange(d) // SSD_HEAD_DIM
    emat1 = (jnp.arange(SM_W)[:, None] == (SM_DT + head_of_col)[None, :]).astype(BF16)
    emat = jnp.concatenate([emat1, emat1], axis=0)
    w_branch_b = w_branch.astype(BF16)
    w_out_b = w_out.astype(BF16)
    w_up_b = w_up.astype(BF16)
    w_down_b = w_down.astype(BF16)
    norm_mix3, norm_ffn3 = row3(norm_mix), row3(norm_ffn)
    gla_ba3, gla_norm3 = row3(gla_ba), row3(gla_norm)
    m_cb3, m_norm3 = row3(mlstm_conv_b), row3(mlstm_norm)
    s_cwx, s_cwbc = ssd_conv_w[:, :, :d], ssd_conv_w[:, :, d:]
    s_cbx, s_cbbc = row3(ssd_conv_b[:, :d]), row3(ssd_conv_b[:, d:])
    s_norm3, dexp3, gate_b3, f_cb3 = row3(ssd_norm), row3(dexp), row3(gate_b), row3(ffn_conv_b)

    tm_proj = _pick(m, (1024, 512, 256, 128, 64))
    tn_proj = _pick(_N_PROJ, (3072, 2048, 1024))
    tb = _pick(t, (512, 256, 128, 64))
    tm_merge = _pick(m, (512, 256, 128, 64))
    tm_ffn = _pick(t, (512, 256, 128, 64))

    h = x.reshape(m, d)
    for l in range(depth):
        proj2, small2 = _proj(h, norm_mix3, w_perm, l, tm_proj, tn_proj)
        proj3 = proj2.reshape(bsz, t, _N_PROJ)
        small3 = small2.reshape(bsz, t, SM_W)
        y_gla = _gla(proj3, small3, wa3, gla_ba3, gla_norm3, l, tb)
        y_m = _mlstm(proj3, small3, mlstm_conv_w, m_cb3, row3(m_bias), col3(m_bias), m_norm3, l, tb)
        y_s = _ssd(proj3, small3, s_cwx, s_cbx, s_cwbc, s_cbbc, row3(s_bias), col3(s_bias), row3(alog),
                   col3(alog), dexp3, s_norm3, emat, l, tb)
        h = _merge(y_gla.reshape(m, d), y_m.reshape(m, d), y_s.reshape(m, d), proj2, gate_b3,
                   w_branch_b, w_out_b, h, l, tm_merge)
        h = _ffn(h.reshape(bsz, t, d), norm_ffn3, w_up_b, ffn_conv_w, f_cb3, w_down_b, norm_final[None, :], l,
                 tm_ffn, norm_out=(l == depth - 1)).reshape(m, d)
    return h.reshape(bsz, t, d)
```

```python
import functools

import jax
import jax.numpy as jnp
from jax import lax
from jax.experimental import pallas as pl
from jax.experimental.pallas import tpu as pltpu

F32 = jnp.float32
BF16 = jnp.bfloat16

EPS = 1e-6
CHUNK = 64
GLA_HEADS = 4
GLA_TAU = 16.0
MLSTM_HEADS = 4
SSD_HEADS = 16
SSD_GROUPS = 2
SSD_STATE = 128
SSD_HEAD_DIM = 64
FFN_CONV = 3
CONV_K = 4
PAD = 16

SM_GA = 0
SM_MI = 16
SM_MF = 20
SM_DT = 24
SM_W = 128

VMEM_LIMIT = 52 * 1024 * 1024


def _dot(a, b):
    return jnp.dot(a.astype(BF16), b.astype(BF16), preferred_element_type=F32)


def _dot_nt(a, b):
    return lax.dot_general(a.astype(BF16), b.astype(BF16), (((1,), (1,)), ((), ())),
                           preferred_element_type=F32)


def _dot_tn(a, b):
    return jnp.dot(a.astype(F32).T.astype(BF16), b.astype(BF16), preferred_element_type=F32)


def _split2(x):
    hi = x.astype(BF16)
    return hi, (x - hi.astype(F32)).astype(BF16)


def _cumsum_rows(tril2, x):
    return jnp.dot(tril2, jnp.concatenate(_split2(x), axis=0), preferred_element_type=F32)


def _cumsum_cols(x, triu2):
    return jnp.dot(jnp.concatenate(_split2(x), axis=1), triu2, preferred_element_type=F32)


def _sigmoid(x):
    return 1.0 / (1.0 + jnp.exp(-x))


def _silu(x):
    return x * _sigmoid(x)


def _softplus(x):
    return jnp.maximum(x, 0.0) + jnp.log1p(jnp.exp(-jnp.abs(x)))


def _log_sigmoid(x):
    return jnp.minimum(x, 0.0) - jnp.log1p(jnp.exp(-jnp.abs(x)))


def _causal():
    row = lax.broadcasted_iota(jnp.int32, (CHUNK, CHUNK), 0)
    col = lax.broadcasted_iota(jnp.int32, (CHUNK, CHUNK), 1)
    return col <= row


def _tril2():
    row = lax.broadcasted_iota(jnp.int32, (CHUNK, 2 * CHUNK), 0)
    col = lax.broadcasted_iota(jnp.int32, (CHUNK, 2 * CHUNK), 1)
    return jnp.where((col <= row) | ((col >= CHUNK) & (col - CHUNK <= row)), 1.0, 0.0).astype(BF16)


def _triu2():
    row = lax.broadcasted_iota(jnp.int32, (2 * CHUNK, CHUNK), 0)
    col = lax.broadcasted_iota(jnp.int32, (2 * CHUNK, CHUNK), 1)
    return jnp.where((row <= col) | ((row >= CHUNK) & (row - CHUNK <= col)), 1.0, 0.0).astype(BF16)


def _shift_mat(taps):
    row = lax.broadcasted_iota(jnp.int32, (taps * CHUNK, PAD + CHUNK), 0)
    col = lax.broadcasted_iota(jnp.int32, (taps * CHUNK, PAD + CHUNK), 1)
    sel = None
    for j in range(taps):
        hit = (row >= j * CHUNK) & (row < (j + 1) * CHUNK) & (col == row - j * CHUNK + PAD - j)
        sel = hit if sel is None else sel | hit
    return jnp.where(sel, 1.0, 0.0).astype(BF16)


def _layer_spec(arr, l):
    zeros = (0,) * (arr.ndim - 1)
    return pl.BlockSpec((None,) + arr.shape[1:], lambda *_: (l,) + zeros)


def _history_spec(tb, width, col_block):
    per = tb // PAD
    return pl.BlockSpec((1, PAD, width), lambda b, i: (b, jnp.maximum(i * per - 1, 0), col_block))


def _conv_window(x_ref, hist_ref, c):
    if c > 0:
        return x_ref[0, c * CHUNK - PAD:(c + 1) * CHUNK, :]
    hist = hist_ref[0]
    hist = jnp.where(pl.program_id(1) > 0, hist, jnp.zeros_like(hist))
    return jnp.concatenate([hist, x_ref[0, 0:CHUNK, :]], axis=0)


def _proj_kernel(x_ref, g_ref, w_ref, o_ref, sm_ref, xn_ref, *, sm_tile, sm_col):
    j = pl.program_id(1)

    @pl.when(j == 0)
    def _():
        x = x_ref[...]
        ms = jnp.mean(x * x, axis=-1, keepdims=True)
        xn_ref[...] = (x * lax.rsqrt(ms + EPS) * g_ref[...]).astype(BF16)

    acc = lax.dot_general(xn_ref[...], w_ref[...], (((1,), (1,)), ((), ())), preferred_element_type=F32)
    o_ref[...] = acc.astype(o_ref.dtype)

    @pl.when(j == sm_tile)
    def _():
        sm_ref[...] = acc[:, sm_col:sm_col + SM_W]


def _proj(x, g, w, l, tm, tn):
    m, d = x.shape
    n = w.shape[1]
    return pl.pallas_call(
        functools.partial(_proj_kernel, sm_tile=_OFF_SMALL // tn, sm_col=_OFF_SMALL % tn),
        grid=(m // tm, n // tn),
        in_specs=[
            pl.BlockSpec((tm, d), lambda i, j: (i, 0)),
            _layer_spec(g, l),
            pl.BlockSpec((None, tn, d), lambda i, j: (l, j, 0)),
        ],
        out_specs=[
            pl.BlockSpec((tm, tn), lambda i, j: (i, j)),
            pl.BlockSpec((tm, SM_W), lambda i, j: (i, 0)),
        ],
        out_shape=[jax.ShapeDtypeStruct((m, n), BF16), jax.ShapeDtypeStruct((m, SM_W), F32)],
        scratch_shapes=[pltpu.VMEM((tm, d), BF16)],
        compiler_params=pltpu.CompilerParams(
            dimension_semantics=("parallel", "arbitrary"), vmem_limit_bytes=VMEM_LIMIT),
        name="in_proj",
    )(x, g, w)


def _gla_kernel(q_ref, k_ref, v_ref, gg_ref, sm_ref, wa_ref, ba_ref, nw_ref, y_ref, st_ref, *, tb):
    dk = q_ref.shape[2] // GLA_HEADS
    dv = v_ref.shape[2] // GLA_HEADS

    @pl.when(pl.program_id(1) == 0)
    def _():
        st_ref[...] = jnp.zeros_like(st_ref)

    chunks = range(tb // CHUNK)
    heads = range(GLA_HEADS)
    causal = _causal()
    tril2 = _tril2()
    scale = dk ** -0.5
    rows = [slice(c * CHUNK, (c + 1) * CHUNK) for c in chunks]

    pre = []
    for c in chunks:
        hi, lo = _split2(sm_ref[0, rows[c], :])
        pre.append(jnp.dot(jnp.concatenate([hi, lo, hi], axis=1), wa_ref[...], preferred_element_type=F32))
    la = [_log_sigmoid(pre[c] + ba_ref[...]) * (1.0 / GLA_TAU) for c in chunks]
    bcum = [_cumsum_rows(tril2, la[c]) for c in chunks]

    for c in chunks:
        r = rows[c]
        kcs = [slice(h * dk, (h + 1) * dk) for h in heads]
        vcs = [slice(h * dv, (h + 1) * dv) for h in heads]
        b = [bcum[c][:, kcs[h]] for h in heads]
        bl = [b[h][CHUNK - 1:CHUNK, :] for h in heads]
        q = [q_ref[0, r, kcs[h]].astype(F32) * scale for h in heads]
        k = [k_ref[0, r, kcs[h]].astype(F32) for h in heads]
        v = [v_ref[0, r, vcs[h]] for h in heads]
        qd = [q[h] * jnp.exp(b[h]) for h in heads]
        ki = [k[h] * jnp.exp(-b[h]) for h in heads]
        kd = [k[h] * jnp.exp(bl[h] - b[h]) for h in heads]
        st = [st_ref[h] for h in heads]
        sc = [_dot_nt(qd[h], ki[h]) for h in heads]
        o2 = [_dot_nt(qd[h], st[h]) for h in heads]
        ds = [_dot_tn(v[h], kd[h]) for h in heads]
        o1 = [_dot(jnp.where(causal, sc[h], 0.0), v[h]) for h in heads]
        for h in heads:
            st_ref[h] = st[h] * jnp.exp(bl[h]) + ds[h]
            o = o1[h] + o2[h]
            ms = jnp.mean(o * o, axis=-1, keepdims=True)
            y = (o * lax.rsqrt(ms + EPS) * nw_ref[:, vcs[h]]) * _silu(gg_ref[0, r, vcs[h]].astype(F32))
            y_ref[0, r, vcs[h]] = y.astype(y_ref.dtype)


def _gla(proj, small, wa3, ba, nw, l, tb):
    bsz, t, _ = proj.shape
    qk_w = ba.shape[2]
    v_w = nw.shape[2]
    dk = qk_w // GLA_HEADS
    dv = v_w // GLA_HEADS
    return pl.pallas_call(
        functools.partial(_gla_kernel, tb=tb),
        grid=(bsz, t // tb),
        in_specs=[
            pl.BlockSpec((1, tb, qk_w), lambda b, i: (b, i, 0)),
            pl.BlockSpec((1, tb, qk_w), lambda b, i: (b, i, 1)),
            pl.BlockSpec((1, tb, v_w), lambda b, i: (b, i, 1)),
            pl.BlockSpec((1, tb, v_w), lambda b, i: (b, i, 2)),
            pl.BlockSpec((1, tb, SM_W), lambda b, i: (b, i, 0)),
            _layer_spec(wa3, l),
            _layer_spec(ba, l),
            _layer_spec(nw, l),
        ],
        out_specs=pl.BlockSpec((1, tb, v_w), lambda b, i: (b, i, 0)),
        out_shape=jax.ShapeDtypeStruct((bsz, t, v_w), BF16),
        scratch_shapes=[pltpu.VMEM((GLA_HEADS, dv, dk), F32)],
        compiler_params=pltpu.CompilerParams(
            dimension_semantics=("parallel", "arbitrary"), vmem_limit_bytes=VMEM_LIMIT),
        name="gla",
    )(proj, proj, proj, proj, small, wa3, ba, nw)


def _mlstm_kernel(qk_ref, qkh_ref, v_ref, og_ref, sm_ref, cw_ref, cb_ref, brow_ref, bcol_ref, nw_ref, y_ref,
                  c_ref, n_ref, m_ref, *, tb):
    qk_w = qk_ref.shape[2] // 2
    dk = qk_w // MLSTM_HEADS
    dv = v_ref.shape[2] // MLSTM_HEADS

    @pl.when(pl.program_id(1) == 0)
    def _():
        c_ref[...] = jnp.zeros_like(c_ref)
        n_ref[...] = jnp.zeros_like(n_ref)
        m_ref[...] = jnp.zeros_like(m_ref)

    chunks = range(tb // CHUNK)
    heads = range(MLSTM_HEADS)
    pairs = [(c, h) for c in chunks for h in heads]
    causal = _causal()
    tril2 = _tril2()
    triu2 = _triu2()
    shift = _shift_mat(CONV_K)
    kscale = dk ** -0.5
    neg_inf = -jnp.inf
    rows = [slice(c * CHUNK, (c + 1) * CHUNK) for c in chunks]

    sm = [sm_ref[0, rows[c], :] for c in chunks]
    pre_col = [sm[c] + brow_ref[...] for c in chunks]
    pre_row = [(sm[c].T + bcol_ref[...])[SM_MI:SM_MI + 2 * MLSTM_HEADS] for c in chunks]
    b_col = [_cumsum_rows(tril2, _log_sigmoid(pre_col[c])) for c in chunks]
    b_row = [_cumsum_cols(_log_sigmoid(pre_row[c]), triu2) for c in chunks]
    xs = [jnp.dot(shift, _conv_window(qk_ref, qkh_ref, c), preferred_element_type=F32) for c in chunks]
    act = []
    for c in chunks:
        acc = cb_ref[...]
        for k in range(CONV_K):
            j = CONV_K - 1 - k
            acc = acc + cw_ref[k:k + 1, :] * xs[c][j * CHUNK:(j + 1) * CHUNK]
        act.append(_silu(acc))

    bc, br, ir, bl, m_loc, w = {}, {}, {}, {}, {}, {}
    for c, h in pairs:
        bc[c, h] = b_col[c][:, SM_MF + h:SM_MF + h + 1]
        br[c, h] = b_row[c][MLSTM_HEADS + h:MLSTM_HEADS + h + 1, :]
        ir[c, h] = pre_row[c][h:h + 1, :]
        bl[c, h] = bc[c, h][CHUNK - 1:CHUNK, :]
        a = bl[c, h] - bc[c, h] + pre_col[c][:, SM_MI + h:SM_MI + h + 1]
        m_loc[c, h] = jnp.max(a, axis=0, keepdims=True)
        w[c, h] = jnp.exp(a - m_loc[c, h])
    m_prev, s_old, s_new = {}, {}, {}
    for h in heads:
        m = m_ref[h][0:1, 0:1]
        for c in chunks:
            m_prev[c, h] = m
            m_new = jnp.maximum(bl[c, h] + m, m_loc[c, h])
            s_old[c, h] = jnp.exp(bl[c, h] + m - m_new)
            s_new[c, h] = jnp.exp(m_loc[c, h] - m_new)
            m = m_new
        m_ref[h] = jnp.broadcast_to(m, m_ref.shape[1:])

    q = {(c, h): act[c][:, h * dk:(h + 1) * dk] for c, h in pairs}
    k = {(c, h): act[c][:, qk_w + h * dk:qk_w + (h + 1) * dk] * kscale for c, h in pairs}
    v = {(c, h): v_ref[0, rows[c], h * dv:(h + 1) * dv] for c, h in pairs}
    wk = {p: w[p] * k[p] for p in pairs}

    qk = {p: _dot_nt(q[p], k[p]) for p in pairs}
    dcs = {p: _dot_tn(wk[p], v[p]) for p in pairs}

    c_prev, n_prev = {}, {}
    for h in heads:
        cst = c_ref[h]
        nst = n_ref[h][0:1, :]
        for c in chunks:
            c_prev[c, h] = cst.astype(BF16)
            n_prev[c, h] = nst
            cst = s_old[c, h] * cst + s_new[c, h] * dcs[c, h]
            nst = s_old[c, h] * nst + s_new[c, h] * jnp.sum(wk[c, h], axis=0, keepdims=True)
        c_ref[h] = cst
        n_ref[h] = jnp.broadcast_to(nst, n_ref.shape[1:])

    log_d = {p: jnp.where(causal, bc[p] - br[p] + ir[p], neg_inf) for p in pairs}
    m_inter = {p: bc[p] + m_prev[p] for p in pairs}
    row_max = {p: jnp.max(log_d[p], axis=-1, keepdims=True) for p in pairs}
    m_t = {p: jnp.maximum(m_inter[p], row_max[p]) for p in pairs}
    wts = {p: qk[p] * jnp.exp(log_d[p] - m_t[p]) for p in pairs}
    s_inter = {p: jnp.exp(m_inter[p] - m_t[p]) for p in pairs}

    pv = {p: _dot(wts[p], v[p]) for p in pairs}
    qc = {p: _dot(q[p], c_prev[p]) for p in pairs}

    row_sum = {p: jnp.sum(wts[p], axis=-1, keepdims=True) for p in pairs}
    qn = {p: jnp.sum(q[p] * n_prev[p], axis=-1, keepdims=True) for p in pairs}
    num = {p: pv[p] + s_inter[p] * qc[p] for p in pairs}
    den = {p: row_sum[p] + s_inter[p] * qn[p] for p in pairs}
    hout = {p: num[p] / jnp.maximum(jnp.abs(den[p]), jnp.exp(-m_t[p])) for p in pairs}
    ms = {p: jnp.mean(hout[p] * hout[p], axis=-1, keepdims=True) for p in pairs}
    for c, h in pairs:
        p = (c, h)
        vc = slice(h * dv, (h + 1) * dv)
        y = (hout[p] * lax.rsqrt(ms[p] + EPS) * nw_ref[:, vc]) * _sigmoid(og_ref[0, rows[c], vc].astype(F32))
        y_ref[0, rows[c], vc] = y.astype(y_ref.dtype)


def _mlstm(proj, small, cw, cb, brow, bcol, nw, l, tb):
    bsz, t, _ = proj.shape
    qk2 = cw.shape[2]
    v_w = nw.shape[2]
    dk = qk2 // 2 // MLSTM_HEADS
    dv = v_w // MLSTM_HEADS
    base = _OFF_MLSTM // v_w
    return pl.pallas_call(
        functools.partial(_mlstm_kernel, tb=tb),
        grid=(bsz, t // tb),
        in_specs=[
            pl.BlockSpec((1, tb, qk2), lambda b, i: (b, i, base)),
            _history_spec(tb, qk2, base),
            pl.BlockSpec((1, tb, v_w), lambda b, i: (b, i, base + 1)),
            pl.BlockSpec((1, tb, v_w), lambda b, i: (b, i, base + 2)),
            pl.BlockSpec((1, tb, SM_W), lambda b, i: (b, i, 0)),
            _layer_spec(cw, l),
            _layer_spec(cb, l),
            _layer_spec(brow, l),
            _layer_spec(bcol, l),
            _layer_spec(nw, l),
        ],
        out_specs=pl.BlockSpec((1, tb, v_w), lambda b, i: (b, i, 0)),
        out_shape=jax.ShapeDtypeStruct((bsz, t, v_w), BF16),
        scratch_shapes=[
            pltpu.VMEM((MLSTM_HEADS, dk, dv), F32),
            pltpu.VMEM((MLSTM_HEADS, 8, dk), F32),
            pltpu.VMEM((MLSTM_HEADS, 8, 128), F32),
        ],
        compiler_params=pltpu.CompilerParams(
            dimension_semantics=("parallel", "arbitrary"), vmem_limit_bytes=VMEM_LIMIT),
        name="mlstm",
    )(proj, proj, proj, proj, small, cw, cb, brow, bcol, nw)


def _ssd_kernel(z_ref, x_ref, xh_ref, bc_ref, bch_ref, sm_ref, cwx_ref, cbx_ref, cwbc_ref, cbbc_ref, brow_ref,
                bcol_ref, alog_row_ref, alog_col_ref, dexp_ref, nw_ref, e_ref, y_ref,
                s_ref, ysc_ref, *, tb):
    inner = x_ref.shape[2]
    gw = inner // SSD_GROUPS
    hpg = SSD_HEADS // SSD_GROUPS
    p = SSD_HEAD_DIM
    z = SSD_STATE

    @pl.when(pl.program_id(1) == 0)
    def _():
        s_ref[...] = jnp.zeros_like(s_ref)

    chunks = range(tb // CHUNK)
    groups = range(SSD_GROUPS)
    causal = _causal()
    tril2 = _tril2()
    triu2 = _triu2()
    shift = _shift_mat(CONV_K)
    neg_inf = -jnp.inf
    a_row = -jnp.exp(alog_row_ref[...])
    a_col = -jnp.exp(alog_col_ref[...])[SM_DT:SM_DT + SSD_HEADS]
    rows = [slice(c * CHUNK, (c + 1) * CHUNK) for c in chunks]

    def conv(xs, cw, cb):
        acc = cb
        for k in range(CONV_K):
            j = CONV_K - 1 - k
            acc = acc + cw[k:k + 1, :] * xs[j * CHUNK:(j + 1) * CHUNK]
        return _silu(acc)

    sm = [sm_ref[0, rows[c], :] for c in chunks]
    dt_col = [_softplus(sm[c] + brow_ref[...]) for c in chunks]
    dt_row = [_softplus((sm[c].T + bcol_ref[...])[SM_DT:SM_DT + SSD_HEADS]) for c in chunks]
    b_col = [_cumsum_rows(tril2, dt_col[c] * a_row) for c in chunks]
    b_row = [_cumsum_cols(dt_row[c] * a_col, triu2) for c in chunks]
    xsx = [jnp.dot(shift, _conv_window(x_ref, xh_ref, c), preferred_element_type=F32) for c in chunks]
    xsb = [jnp.dot(shift, _conv_window(bc_ref, bch_ref, c), preferred_element_type=F32) for c in chunks]
    expd = []
    for c in chunks:
        bl_row = b_col[c][CHUNK - 1:CHUNK, :]
        w_col = jnp.exp(bl_row - b_col[c]) * dt_col[c]
        eb_col = jnp.exp(b_col[c])
        ebl = jnp.broadcast_to(jnp.exp(bl_row), (8, SM_W))
        hi, lo = _split2(jnp.concatenate([w_col, eb_col, ebl], axis=0))
        expd.append(jnp.dot(jnp.concatenate([hi, lo], axis=1), e_ref[...], preferred_element_type=F32))
    xa = [conv(xsx[c], cwx_ref[...], cbx_ref[...]) for c in chunks]
    bca = [conv(xsb[c], cwbc_ref[...], cbbc_ref[...]) for c in chunks]

    for c in chunks:
        w_x = expd[c][0:CHUNK]
        eb_x = expd[c][CHUNK:2 * CHUNK]
        ebl_x = expd[c][2 * CHUNK:2 * CHUNK + 1]
        xg = [xa[c][:, g * gw:(g + 1) * gw] for g in groups]
        bg = [bca[c][:, g * z:(g + 1) * z] for g in groups]
        cg = [bca[c][:, SSD_GROUPS * z + g * z:SSD_GROUPS * z + (g + 1) * z] for g in groups]
        cb = [_dot_nt(cg[g], bg[g]) for g in groups]
        s_prev = [s_ref[g] for g in groups]
        inter = [_dot(cg[g], s_prev[g]) for g in groups]
        dst = [_dot_tn(bg[g], xg[g] * w_x[:, g * gw:(g + 1) * gw]) for g in groups]
        mixes = []
        for h in range(SSD_HEADS):
            bc = b_col[c][:, SM_DT + h:SM_DT + h + 1]
            br = b_row[c][h:h + 1, :]
            dtr = dt_row[c][h:h + 1, :]
            mixes.append(cb[h // hpg] * jnp.exp(jnp.where(causal, bc - br, neg_inf)) * dtr)
        ys = [_dot(mixes[h], xg[h // hpg][:, (h % hpg) * p:(h % hpg + 1) * p]) for h in range(SSD_HEADS)]
        for h in range(SSD_HEADS):
            ysc_ref[:, h * p:(h + 1) * p] = ys[h]
        for g in groups:
            gc = slice(g * gw, (g + 1) * gw)
            s_ref[g] = s_prev[g] * ebl_x[:, gc] + dst[g]
            y = ysc_ref[:, gc] + inter[g] * eb_x[:, gc] + dexp_ref[:, gc] * xg[g]
            y = y * _silu(z_ref[0, rows[c], gc].astype(F32))
            ms = jnp.mean(y * y, axis=-1, keepdims=True)
            y_ref[0, rows[c], gc] = (y * lax.rsqrt(ms + EPS) * nw_ref[:, gc]).astype(y_ref.dtype)


def _ssd(proj, small, cwx, cbx, cwbc, cbbc, brow, bcol, alog_row, alog_col, dexp, nw, emat, l, tb):
    bsz, t, _ = proj.shape
    inner = nw.shape[2]
    bcw = cwbc.shape[2]
    return pl.pallas_call(
        functools.partial(_ssd_kernel, tb=tb),
        grid=(bsz, t // tb),
        in_specs=[
            pl.BlockSpec((1, tb, inner), lambda b, i: (b, i, _OFF_SZ // inner)),
            pl.BlockSpec((1, tb, inner), lambda b, i: (b, i, _OFF_SX // inner)),
            _history_spec(tb, inner, _OFF_SX // inner),
            pl.BlockSpec((1, tb, bcw), lambda b, i: (b, i, _OFF_SBC // bcw)),
            _history_spec(tb, bcw, _OFF_SBC // bcw),
            pl.BlockSpec((1, tb, SM_W), lambda b, i: (b, i, 0)),
            _layer_spec(cwx, l),
            _layer_spec(cbx, l),
            _layer_spec(cwbc, l),
            _layer_spec(cbbc, l),
            _layer_spec(brow, l),
            _layer_spec(bcol, l),
            _layer_spec(alog_row, l),
            _layer_spec(alog_col, l),
            _layer_spec(dexp, l),
            _layer_spec(nw, l),
            pl.BlockSpec(emat.shape, lambda b, i: (0, 0)),
        ],
        out_specs=pl.BlockSpec((1, tb, inner), lambda b, i: (b, i, 0)),
        out_shape=jax.ShapeDtypeStruct((bsz, t, inner), BF16),
        scratch_shapes=[
            pltpu.VMEM((SSD_GROUPS, SSD_STATE, inner // SSD_GROUPS), F32),
            pltpu.VMEM((CHUNK, inner), F32),
        ],
        compiler_params=pltpu.CompilerParams(
            dimension_semantics=("parallel", "arbitrary"), vmem_limit_bytes=VMEM_LIMIT),
        name="ssd",
    )(proj, proj, proj, proj, proj, small, cwx, cbx, cwbc, cbbc, brow, bcol, alog_row, alog_col, dexp, nw, emat)


def _merge_kernel(ya_ref, yb_ref, yc_ref, gates_ref, gb_ref, wb_ref, wo_ref, h_ref, o_ref, *, rows):
    d = h_ref.shape[1]
    for r0 in range(0, h_ref.shape[0], rows):
        r = slice(r0, r0 + rows)
        acc = None
        for k, y_ref in enumerate((ya_ref, yb_ref, yc_ref)):
            zk = jnp.dot(y_ref[r, :], wb_ref[k], preferred_element_type=F32)
            gk = _sigmoid(gates_ref[r, k * d:(k + 1) * d].astype(F32) + gb_ref[:, k * d:(k + 1) * d])
            acc = gk * zk if acc is None else acc + gk * zk
        o_ref[r, :] = h_ref[r, :] + jnp.dot(acc.astype(BF16), wo_ref[...], preferred_element_type=F32)


def _merge(ya, yb, yc, proj2, gate_b, w_branch, w_out, h, l, tm):
    m, d = h.shape
    nb = w_branch.shape[1]
    resident = dict(pipeline_mode=pl.Buffered(1))
    return pl.pallas_call(
        functools.partial(_merge_kernel, rows=min(tm, 512)),
        grid=(m // tm,),
        in_specs=[
            pl.BlockSpec((tm, d), lambda i: (i, 0)),
            pl.BlockSpec((tm, d), lambda i: (i, 0)),
            pl.BlockSpec((tm, d), lambda i: (i, 0)),
            pl.BlockSpec((tm, nb * d), lambda i: (i, _OFF_GATES // (nb * d))),
            _layer_spec(gate_b, l),
            pl.BlockSpec((None,) + w_branch.shape[1:], lambda i: (l, 0, 0, 0), **resident),
            pl.BlockSpec((None,) + w_out.shape[1:], lambda i: (l, 0, 0), **resident),
            pl.BlockSpec((tm, d), lambda i: (i, 0)),
        ],
        out_specs=pl.BlockSpec((tm, d), lambda i: (i, 0)),
        out_shape=jax.ShapeDtypeStruct((m, d), F32),
        compiler_params=pltpu.CompilerParams(
            dimension_semantics=("parallel",), vmem_limit_bytes=VMEM_LIMIT),
        name="merge",
    )(ya, yb, yc, proj2, gate_b, w_branch, w_out, h)


def _ffn_kernel(h_ref, g_ref, wg_ref, wv_ref, cwg_ref, cwv_ref, cbg_ref, cbv_ref, wd_ref, gout_ref, o_ref,
                gpad_ref, vpad_ref, *, tm, norm_out):
    @pl.when(pl.program_id(1) == 0)
    def _():
        gpad_ref[0:8, :] = jnp.zeros((8, gpad_ref.shape[1]), F32)
        vpad_ref[0:8, :] = jnp.zeros((8, vpad_ref.shape[1]), F32)

    x = h_ref[0]
    ms = jnp.mean(x * x, axis=-1, keepdims=True)
    xn = (x * lax.rsqrt(ms + EPS) * g_ref[...]).astype(BF16)
    gpad_ref[8:8 + tm, :] = jnp.dot(xn, wg_ref[...], preferred_element_type=F32)
    vpad_ref[8:8 + tm, :] = jnp.dot(xn, wv_ref[...], preferred_element_type=F32)
    gate = cbg_ref[...]
    val = cbv_ref[...]
    for k in range(FFN_CONV):
        s = FFN_CONV - 1 - k
        gate = gate + cwg_ref[k:k + 1, :] * gpad_ref[8 - s:8 - s + tm, :]
        val = val + cwv_ref[k:k + 1, :] * vpad_ref[8 - s:8 - s + tm, :]
    gpad_ref[0:8, :] = gpad_ref[tm:tm + 8, :]
    vpad_ref[0:8, :] = vpad_ref[tm:tm + 8, :]
    act = (_silu(gate) * val).astype(BF16)
    r = x + jnp.dot(act, wd_ref[...], preferred_element_type=F32)
    if norm_out:
        ms = jnp.mean(r * r, axis=-1, keepdims=True)
        r = r * lax.rsqrt(ms + EPS) * gout_ref[...]
    o_ref[0] = r


def _ffn(h3, g, w_up, cw, cb, w_down, g_out, l, tm, norm_out):
    bsz, t, d = h3.shape
    dff = w_down.shape[1]
    resident = dict(pipeline_mode=pl.Buffered(1))
    return pl.pallas_call(
        functools.partial(_ffn_kernel, tm=tm, norm_out=norm_out),
        grid=(bsz, t // tm),
        in_specs=[
            pl.BlockSpec((1, tm, d), lambda b, i: (b, i, 0)),
            _layer_spec(g, l),
            pl.BlockSpec((None, d, dff), lambda b, i: (l, 0, 0), **resident),
            pl.BlockSpec((None, d, dff), lambda b, i: (l, 0, 1), **resident),
            pl.BlockSpec((None, FFN_CONV, dff), lambda b, i: (l, 0, 0)),
            pl.BlockSpec((None, FFN_CONV, dff), lambda b, i: (l, 0, 1)),
            pl.BlockSpec((None, 1, dff), lambda b, i: (l, 0, 0)),
            pl.BlockSpec((None, 1, dff), lambda b, i: (l, 0, 1)),
            pl.BlockSpec((None, dff, d), lambda b, i: (l, 0, 0), **resident),
            pl.BlockSpec((1, d), lambda b, i: (0, 0)),
        ],
        out_specs=pl.BlockSpec((1, tm, d), lambda b, i: (b, i, 0)),
        out_shape=jax.ShapeDtypeStruct((bsz, t, d), F32),
        scratch_shapes=[
            pltpu.VMEM((tm + 8, dff), F32),
            pltpu.VMEM((tm + 8, dff), F32),
        ],
        compiler_params=pltpu.CompilerParams(
            dimension_semantics=("parallel", "arbitrary"), vmem_limit_bytes=VMEM_LIMIT),
        name="conv_ffn",
    )(h3, g, w_up, w_up, cw, cw, cb, cb, w_down, g_out)


_RL_TILE = 512


def _relayout_kernel(*refs):
    refs[-1][...] = refs[0][0].astype(BF16)


def _relayout_small_kernel(ga_ref, mg_ref, dt_ref, prev_ref, o_ref):
    rows = ga_ref.shape[1] + mg_ref.shape[1] + dt_ref.shape[1]
    sm = jnp.concatenate([ga_ref[0], mg_ref[0], dt_ref[0],
                          jnp.zeros((SM_W - rows, o_ref.shape[1]), F32)], axis=0)
    o_ref[0:SM_W, :] = sm.astype(BF16)
    o_ref[SM_W:, :] = jnp.zeros((o_ref.shape[0] - SM_W, o_ref.shape[1]), BF16)


def _relayout(w_t, segments, small_srcs):
    depth, _, d = w_t.shape
    out = None
    for dst, src, length in segments:
        assert length % _RL_TILE == 0 and dst % _RL_TILE == 0 and src % 8 == 0
        in_specs = [pl.BlockSpec((pl.Element(1), pl.Element(_RL_TILE), pl.Element(d)),
                                 lambda l, k, s=src: (l, pl.multiple_of(s + k * _RL_TILE, 8), 0))]
        args = [w_t]
        aliases = {}
        if out is not None:
            in_specs.append(pl.BlockSpec(memory_space=pl.ANY))
            args.append(out)
            aliases = {1: 0}
        out = pl.pallas_call(
            _relayout_kernel,
            grid=(depth, length // _RL_TILE),
            in_specs=in_specs,
            out_specs=pl.BlockSpec((None, _RL_TILE, d), lambda l, k, o=dst: (l, o // _RL_TILE + k, 0)),
            out_shape=jax.ShapeDtypeStruct((depth, _N_PROJ, d), BF16),
            input_output_aliases=aliases,
            compiler_params=pltpu.CompilerParams(dimension_semantics=("parallel", "parallel")),
            name="w_relayout",
        )(*args)
    blk = lambda src, n: pl.BlockSpec((pl.Element(1), pl.Element(n), pl.Element(d)), lambda l, s=src: (l, s, 0))
    return pl.pallas_call(
        _relayout_small_kernel,
        grid=(depth,),
        in_specs=[blk(*small_srcs[0]), blk(*small_srcs[1]), blk(*small_srcs[2]), pl.BlockSpec(memory_space=pl.ANY)],
        out_specs=pl.BlockSpec((None, _RL_TILE, d), lambda l: (l, _OFF_SMALL // _RL_TILE, 0)),
        out_shape=jax.ShapeDtypeStruct((depth, _N_PROJ, d), BF16),
        input_output_aliases={3: 0},
        compiler_params=pltpu.CompilerParams(dimension_semantics=("parallel",)),
        name="w_relayout_small",
    )(w_t, w_t, w_t, out)


_D = 1024
_OFF_MLSTM = 3 * _D
_OFF_GATES = 6 * _D
_OFF_SZ = 9 * _D
_OFF_SX = 10 * _D
_OFF_SBC = 11 * _D
_OFF_SMALL = 11 * _D + _D // 2
_N_PROJ = 12 * _D


def _pick(n, cands):
    for c in cands:
        if n % c == 0:
            return c
    return n


def kernel(x, norm_mix, w_in, gla_wa, gla_ba, gla_norm, mlstm_conv_w, mlstm_conv_b, mlstm_bi, mlstm_bf,
           mlstm_norm, ssd_conv_w, ssd_conv_b, ssd_dt_bias, ssd_a_log, ssd_d, ssd_norm, gate_b, w_branch,
           w_out, norm_ffn, w_up, ffn_conv_w, ffn_conv_b, w_down, norm_final):
    bsz, t, d = x.shape
    assert d == _D
    depth = w_in.shape[0]
    m = bsz * t
    qk = d // 2
    rank = gla_wa.shape[1]
    bc_w = 2 * SSD_GROUPS * SSD_STATE
    sizes = (qk, qk, d, rank, d, 2 * qk, d, MLSTM_HEADS, MLSTM_HEADS, d, d, d + bc_w, SSD_HEADS, 3 * d)
    offs = [0]
    for s in sizes:
        offs.append(offs[-1] + s)
    (o_gq, o_gk, o_gv, o_ga, o_gg, o_mqk, o_mv, o_mi, o_mf, o_mo, o_sz, o_sxbc, o_sdt, o_gates, o_end) = offs
    assert o_end == w_in.shape[2]

    segments = ((0, o_gq, o_ga - o_gq), (o_ga - o_gq, o_gg, o_mi - o_gg), (_OFF_MLSTM + 2 * d, o_mo, o_sz - o_mo),
                (_OFF_GATES, o_gates, o_end - o_gates), (_OFF_SZ, o_sz, o_sdt - o_sz))
    assert (SM_GA, SM_MI, SM_DT) == (0, rank, rank + 2 * MLSTM_HEADS)
    w_perm = _relayout(jnp.swapaxes(w_in, 1, 2), segments,
                       ((o_ga, rank), (o_mi, 2 * MLSTM_HEADS), (o_sdt, SSD_HEADS)))

    row3 = lambda a: a[:, None, :]
    col3 = lambda a: a[:, :, None]
    pad_small = lambda a, off: jnp.pad(a, ((0, 0), (off, SM_W - off - a.shape[1])))
    wa_pad = jnp.pad(gla_wa, ((0, 0), (SM_GA, SM_W - SM_GA - rank), (0, 0)))
    wa_hi = wa_pad.astype(BF16)
    wa_lo = (wa_pad - wa_hi.astype(F32)).astype(BF16)
    wa3 = jnp.concatenate([wa_hi, wa_hi, wa_lo], axis=1)
    m_bias = pad_small(mlstm_bi, SM_MI) + pad_small(mlstm_bf, SM_MF)
    s_bias = pad_small(ssd_dt_bias, SM_DT)
    alog = pad_small(ssd_a_log, SM_DT)
    dexp = jnp.repeat(ssd_d, SSD_HEAD_DIM, axis=-1)
    head_of_col = jnp.arange(d) // SSD_HEAD_DIM
    emat1 = (jnp.arange(SM_W)[:, None] == (SM_DT + head_of_col)[None, :]).astype(BF16)
    emat = jnp.concatenate([emat1, emat1], axis=0)
    w_branch_b = w_branch.astype(BF16)
    w_out_b = w_out.astype(BF16)
    w_up_b = w_up.astype(BF16)
    w_down_b = w_down.astype(BF16)
    norm_mix3, norm_ffn3 = row3(norm_mix), row3(norm_ffn)
    gla_ba3, gla_norm3 = row3(gla_ba), row3(gla_norm)
    m_cb3, m_norm3 = row3(mlstm_conv_b), row3(mlstm_norm)
    s_cwx, s_cwbc = ssd_conv_w[:, :, :d], ssd_conv_w[:, :, d:]
    s_cbx, s_cbbc = row3(ssd_conv_b[:, :d]), row3(ssd_conv_b[:, d:])
    s_norm3, dexp3, gate_b3, f_cb3 = row3(ssd_norm), row3(dexp), row3(gate_b), row3(ffn_conv_b)

    tm_proj = _pick(m, (1024, 512, 256, 128, 64))
    tn_proj = _pick(_N_PROJ, (3072, 2048, 1024))
    tb = _pick(t, (512, 256, 128, 64))
    tb_gla = _pick(t, (1024, 512, 256, 128, 64))
    tm_merge = _pick(m, (1024, 512, 256, 128, 64))
    tm_ffn = _pick(t, (512, 256, 128, 64))

    h = x.reshape(m, d)
    for l in range(depth):
        proj2, small2 = _proj(h, norm_mix3, w_perm, l, tm_proj, tn_proj)
        proj3 = proj2.reshape(bsz, t, _N_PROJ)
        small3 = small2.reshape(bsz, t, SM_W)
        y_gla = _gla(proj3, small3, wa3, gla_ba3, gla_norm3, l, tb_gla)
        y_m = _mlstm(proj3, small3, mlstm_conv_w, m_cb3, row3(m_bias), col3(m_bias), m_norm3, l, tb)
        y_s = _ssd(proj3, small3, s_cwx, s_cbx, s_cwbc, s_cbbc, row3(s_bias), col3(s_bias), row3(alog),
                   col3(alog), dexp3, s_norm3, emat, l, tb)
        h = _merge(y_gla.reshape(m, d), y_m.reshape(m, d), y_s.reshape(m, d), proj2, gate_b3,
                   w_branch_b, w_out_b, h, l, tm_merge)
        h = _ffn(h.reshape(bsz, t, d), norm_ffn3, w_up_b, ffn_conv_w, f_cb3, w_down_b, norm_final[None, :], l,
                 tm_ffn, norm_out=(l == depth - 1)).reshape(m, d)
    return h.reshape(bsz, t, d)
```

```python
import functools

import jax
import jax.numpy as jnp
from jax import lax
from jax.experimental import pallas as pl
from jax.experimental.pallas import tpu as pltpu

F32 = jnp.float32
BF16 = jnp.bfloat16

EPS = 1e-6
CHUNK = 64
GLA_HEADS = 4
GLA_TAU = 16.0
MLSTM_HEADS = 4
SSD_HEADS = 16
SSD_GROUPS = 2
SSD_STATE = 128
SSD_HEAD_DIM = 64
FFN_CONV = 3
CONV_K = 4
PAD = 16

SM_GA = 0
SM_MI = 16
SM_MF = 20
SM_DT = 24
SM_W = 128

VMEM_LIMIT = 52 * 1024 * 1024


def _dot(a, b):
    return jnp.dot(a.astype(BF16), b.astype(BF16), preferred_element_type=F32)


def _dot_nt(a, b):
    return lax.dot_general(a.astype(BF16), b.astype(BF16), (((1,), (1,)), ((), ())),
                           preferred_element_type=F32)


def _dot_tn(a, b):
    return jnp.dot(a.astype(F32).T.astype(BF16), b.astype(BF16), preferred_element_type=F32)


def _split2(x):
    hi = x.astype(BF16)
    return hi, (x - hi.astype(F32)).astype(BF16)


def _cumsum_rows(tril2, x):
    return jnp.dot(tril2, jnp.concatenate(_split2(x), axis=0), preferred_element_type=F32)


def _cumsum_cols(x, triu2):
    return jnp.dot(jnp.concatenate(_split2(x), axis=1), triu2, preferred_element_type=F32)


def _sigmoid(x):
    return 1.0 / (1.0 + jnp.exp(-x))


def _silu(x):
    return x * _sigmoid(x)


def _softplus(x):
    return jnp.maximum(x, 0.0) + jnp.log1p(jnp.exp(-jnp.abs(x)))


def _log_sigmoid(x):
    return jnp.minimum(x, 0.0) - jnp.log1p(jnp.exp(-jnp.abs(x)))


def _causal():
    row = lax.broadcasted_iota(jnp.int32, (CHUNK, CHUNK), 0)
    col = lax.broadcasted_iota(jnp.int32, (CHUNK, CHUNK), 1)
    return col <= row


def _tril2():
    row = lax.broadcasted_iota(jnp.int32, (CHUNK, 2 * CHUNK), 0)
    col = lax.broadcasted_iota(jnp.int32, (CHUNK, 2 * CHUNK), 1)
    return jnp.where((col <= row) | ((col >= CHUNK) & (col - CHUNK <= row)), 1.0, 0.0).astype(BF16)


def _triu2():
    row = lax.broadcasted_iota(jnp.int32, (2 * CHUNK, CHUNK), 0)
    col = lax.broadcasted_iota(jnp.int32, (2 * CHUNK, CHUNK), 1)
    return jnp.where((row <= col) | ((row >= CHUNK) & (row - CHUNK <= col)), 1.0, 0.0).astype(BF16)


def _shift_mat(taps):
    row = lax.broadcasted_iota(jnp.int32, (taps * CHUNK, PAD + CHUNK), 0)
    col = lax.broadcasted_iota(jnp.int32, (taps * CHUNK, PAD + CHUNK), 1)
    sel = None
    for j in range(taps):
        hit = (row >= j * CHUNK) & (row < (j + 1) * CHUNK) & (col == row - j * CHUNK + PAD - j)
        sel = hit if sel is None else sel | hit
    return jnp.where(sel, 1.0, 0.0).astype(BF16)


def _layer_spec(arr, l):
    zeros = (0,) * (arr.ndim - 1)
    return pl.BlockSpec((None,) + arr.shape[1:], lambda *_: (l,) + zeros)


def _history_spec(tb, width, col_block):
    per = tb // PAD
    return pl.BlockSpec((1, PAD, width), lambda b, i: (b, jnp.maximum(i * per - 1, 0), col_block))


def _conv_window(x_ref, hist_ref, c):
    if c > 0:
        return x_ref[0, c * CHUNK - PAD:(c + 1) * CHUNK, :]
    hist = hist_ref[0]
    hist = jnp.where(pl.program_id(1) > 0, hist, jnp.zeros_like(hist))
    return jnp.concatenate([hist, x_ref[0, 0:CHUNK, :]], axis=0)


def _proj_kernel(x_ref, g_ref, w_ref, o_ref, sm_ref, xn_ref, *, sm_tile, sm_col):
    j = pl.program_id(1)

    @pl.when(j == 0)
    def _():
        x = x_ref[...]
        ms = jnp.mean(x * x, axis=-1, keepdims=True)
        xn_ref[...] = (x * lax.rsqrt(ms + EPS) * g_ref[...]).astype(BF16)

    acc = lax.dot_general(xn_ref[...], w_ref[...], (((1,), (1,)), ((), ())), preferred_element_type=F32)
    o_ref[...] = acc.astype(o_ref.dtype)

    @pl.when(j == sm_tile)
    def _():
        sm_ref[...] = acc[:, sm_col:sm_col + SM_W]


def _proj(x, g, w, l, tm, tn):
    m, d = x.shape
    n = w.shape[1]
    return pl.pallas_call(
        functools.partial(_proj_kernel, sm_tile=_OFF_SMALL // tn, sm_col=_OFF_SMALL % tn),
        grid=(m // tm, n // tn),
        in_specs=[
            pl.BlockSpec((tm, d), lambda i, j: (i, 0)),
            _layer_spec(g, l),
            pl.BlockSpec((None, tn, d), lambda i, j: (l, j, 0)),
        ],
        out_specs=[
            pl.BlockSpec((tm, tn), lambda i, j: (i, j)),
            pl.BlockSpec((tm, SM_W), lambda i, j: (i, 0)),
        ],
        out_shape=[jax.ShapeDtypeStruct((m, n), BF16), jax.ShapeDtypeStruct((m, SM_W), F32)],
        scratch_shapes=[pltpu.VMEM((tm, d), BF16)],
        compiler_params=pltpu.CompilerParams(
            dimension_semantics=("parallel", "arbitrary"), vmem_limit_bytes=VMEM_LIMIT),
        name="in_proj",
    )(x, g, w)


def _gla_kernel(q_ref, k_ref, v_ref, gg_ref, sm_ref, wa_ref, ba_ref, nw_ref, y_ref, st_ref, *, tb):
    dk = q_ref.shape[2] // GLA_HEADS
    dv = v_ref.shape[2] // GLA_HEADS

    @pl.when(pl.program_id(1) == 0)
    def _():
        st_ref[...] = jnp.zeros_like(st_ref)

    chunks = range(tb // CHUNK)
    heads = range(GLA_HEADS)
    causal = _causal()
    tril2 = _tril2()
    scale = dk ** -0.5
    rows = [slice(c * CHUNK, (c + 1) * CHUNK) for c in chunks]

    pre = []
    for c in chunks:
        hi, lo = _split2(sm_ref[0, rows[c], :])
        pre.append(jnp.dot(jnp.concatenate([hi, lo, hi], axis=1), wa_ref[...], preferred_element_type=F32))
    la = [_log_sigmoid(pre[c] + ba_ref[...]) * (1.0 / GLA_TAU) for c in chunks]
    bcum = [_cumsum_rows(tril2, la[c]) for c in chunks]

    for c in chunks:
        r = rows[c]
        kcs = [slice(h * dk, (h + 1) * dk) for h in heads]
        vcs = [slice(h * dv, (h + 1) * dv) for h in heads]
        b = [bcum[c][:, kcs[h]] for h in heads]
        bl = [b[h][CHUNK - 1:CHUNK, :] for h in heads]
        q = [q_ref[0, r, kcs[h]].astype(F32) * scale for h in heads]
        k = [k_ref[0, r, kcs[h]].astype(F32) for h in heads]
        v = [v_ref[0, r, vcs[h]] for h in heads]
        qd = [q[h] * jnp.exp(b[h]) for h in heads]
        ki = [k[h] * jnp.exp(-b[h]) for h in heads]
        kd = [k[h] * jnp.exp(bl[h] - b[h]) for h in heads]
        st = [st_ref[h] for h in heads]
        sc = [_dot_nt(qd[h], ki[h]) for h in heads]
        o2 = [_dot_nt(qd[h], st[h]) for h in heads]
        ds = [_dot_tn(v[h], kd[h]) for h in heads]
        o1 = [_dot(jnp.where(causal, sc[h], 0.0), v[h]) for h in heads]
        for h in heads:
            st_ref[h] = st[h] * jnp.exp(bl[h]) + ds[h]
            o = o1[h] + o2[h]
            ms = jnp.mean(o * o, axis=-1, keepdims=True)
            y = (o * lax.rsqrt(ms + EPS) * nw_ref[:, vcs[h]]) * _silu(gg_ref[0, r, vcs[h]].astype(F32))
            y_ref[0, r, vcs[h]] = y.astype(y_ref.dtype)


def _gla(proj, small, wa3, ba, nw, l, tb):
    bsz, t, _ = proj.shape
    qk_w = ba.shape[2]
    v_w = nw.shape[2]
    dk = qk_w // GLA_HEADS
    dv = v_w // GLA_HEADS
    return pl.pallas_call(
        functools.partial(_gla_kernel, tb=tb),
        grid=(bsz, t // tb),
        in_specs=[
            pl.BlockSpec((1, tb, qk_w), lambda b, i: (b, i, 0)),
            pl.BlockSpec((1, tb, qk_w), lambda b, i: (b, i, 1)),
            pl.BlockSpec((1, tb, v_w), lambda b, i: (b, i, 1)),
            pl.BlockSpec((1, tb, v_w), lambda b, i: (b, i, 2)),
            pl.BlockSpec((1, tb, SM_W), lambda b, i: (b, i, 0)),
            _layer_spec(wa3, l),
            _layer_spec(ba, l),
            _layer_spec(nw, l),
        ],
        out_specs=pl.BlockSpec((1, tb, v_w), lambda b, i: (b, i, 0)),
        out_shape=jax.ShapeDtypeStruct((bsz, t, v_w), BF16),
        scratch_shapes=[pltpu.VMEM((GLA_HEADS, dv, dk), F32)],
        compiler_params=pltpu.CompilerParams(
            dimension_semantics=("parallel", "arbitrary"), vmem_limit_bytes=VMEM_LIMIT),
        name="gla",
    )(proj, proj, proj, proj, small, wa3, ba, nw)


def _mlstm_kernel(qk_ref, qkh_ref, v_ref, og_ref, sm_ref, cw_ref, cb_ref, brow_ref, bcol_ref, nw_ref, y_ref,
                  c_ref, n_ref, m_ref, *, tb):
    qk_w = qk_ref.shape[2] // 2
    dk = qk_w // MLSTM_HEADS
    dv = v_ref.shape[2] // MLSTM_HEADS

    @pl.when(pl.program_id(1) == 0)
    def _():
        c_ref[...] = jnp.zeros_like(c_ref)
        n_ref[...] = jnp.zeros_like(n_ref)
        m_ref[...] = jnp.zeros_like(m_ref)

    chunks = range(tb // CHUNK)
    heads = range(MLSTM_HEADS)
    pairs = [(c, h) for c in chunks for h in heads]
    causal = _causal()
    tril2 = _tril2()
    triu2 = _triu2()
    shift = _shift_mat(CONV_K)
    kscale = dk ** -0.5
    neg_inf = -jnp.inf
    rows = [slice(c * CHUNK, (c + 1) * CHUNK) for c in chunks]

    sm = [sm_ref[0, rows[c], :] for c in chunks]
    pre_col = [sm[c] + brow_ref[...] for c in chunks]
    pre_row = [(sm[c].T + bcol_ref[...])[SM_MI:SM_MI + 2 * MLSTM_HEADS] for c in chunks]
    b_col = [_cumsum_rows(tril2, _log_sigmoid(pre_col[c])) for c in chunks]
    b_row = [_cumsum_cols(_log_sigmoid(pre_row[c]), triu2) for c in chunks]
    xs = [jnp.dot(shift, _conv_window(qk_ref, qkh_ref, c), preferred_element_type=F32) for c in chunks]
    act = []
    for c in chunks:
        acc = cb_ref[...]
        for k in range(CONV_K):
            j = CONV_K - 1 - k
            acc = acc + cw_ref[k:k + 1, :] * xs[c][j * CHUNK:(j + 1) * CHUNK]
        act.append(_silu(acc))

    bc, br, ir, bl, m_loc, w = {}, {}, {}, {}, {}, {}
    for c, h in pairs:
        bc[c, h] = b_col[c][:, SM_MF + h:SM_MF + h + 1]
        br[c, h] = b_row[c][MLSTM_HEADS + h:MLSTM_HEADS + h + 1, :]
        ir[c, h] = pre_row[c][h:h + 1, :]
        bl[c, h] = bc[c, h][CHUNK - 1:CHUNK, :]
        a = bl[c, h] - bc[c, h] + pre_col[c][:, SM_MI + h:SM_MI + h + 1]
        m_loc[c, h] = jnp.max(a, axis=0, keepdims=True)
        w[c, h] = jnp.exp(a - m_loc[c, h])
    m_prev, s_old, s_new = {}, {}, {}
    for h in heads:
        m = m_ref[h][0:1, 0:1]
        for c in chunks:
            m_prev[c, h] = m
            m_new = jnp.maximum(bl[c, h] + m, m_loc[c, h])
            s_old[c, h] = jnp.exp(bl[c, h] + m - m_new)
            s_new[c, h] = jnp.exp(m_loc[c, h] - m_new)
            m = m_new
        m_ref[h] = jnp.broadcast_to(m, m_ref.shape[1:])

    q = {(c, h): act[c][:, h * dk:(h + 1) * dk] for c, h in pairs}
    k = {(c, h): act[c][:, qk_w + h * dk:qk_w + (h + 1) * dk] * kscale for c, h in pairs}
    v = {(c, h): v_ref[0, rows[c], h * dv:(h + 1) * dv] for c, h in pairs}
    wk = {p: w[p] * k[p] for p in pairs}

    qk = {p: _dot_nt(q[p], k[p]) for p in pairs}
    dcs = {p: _dot_tn(wk[p], v[p]) for p in pairs}

    c_prev, n_prev = {}, {}
    for h in heads:
        cst = c_ref[h]
        nst = n_ref[h][0:1, :]
        for c in chunks:
            c_prev[c, h] = cst.astype(BF16)
            n_prev[c, h] = nst
            cst = s_old[c, h] * cst + s_new[c, h] * dcs[c, h]
            nst = s_old[c, h] * nst + s_new[c, h] * jnp.sum(wk[c, h], axis=0, keepdims=True)
        c_ref[h] = cst
        n_ref[h] = jnp.broadcast_to(nst, n_ref.shape[1:])

    log_d = {p: jnp.where(causal, bc[p] - br[p] + ir[p], neg_inf) for p in pairs}
    m_inter = {p: bc[p] + m_prev[p] for p in pairs}
    row_max = {p: jnp.max(log_d[p], axis=-1, keepdims=True) for p in pairs}
    m_t = {p: jnp.maximum(m_inter[p], row_max[p]) for p in pairs}
    wts = {p: qk[p] * jnp.exp(log_d[p] - m_t[p]) for p in pairs}
    s_inter = {p: jnp.exp(m_inter[p] - m_t[p]) for p in pairs}

    pv = {p: _dot(wts[p], v[p]) for p in pairs}
    qc = {p: _dot(q[p], c_prev[p]) for p in pairs}

    row_sum = {p: jnp.sum(wts[p], axis=-1, keepdims=True) for p in pairs}
    qn = {p: jnp.sum(q[p] * n_prev[p], axis=-1, keepdims=True) for p in pairs}
    num = {p: pv[p] + s_inter[p] * qc[p] for p in pairs}
    den = {p: row_sum[p] + s_inter[p] * qn[p] for p in pairs}
    hout = {p: num[p] / jnp.maximum(jnp.abs(den[p]), jnp.exp(-m_t[p])) for p in pairs}
    ms = {p: jnp.mean(hout[p] * hout[p], axis=-1, keepdims=True) for p in pairs}
    for c, h in pairs:
        p = (c, h)
        vc = slice(h * dv, (h + 1) * dv)
        y = (hout[p] * lax.rsqrt(ms[p] + EPS) * nw_ref[:, vc]) * _sigmoid(og_ref[0, rows[c], vc].astype(F32))
        y_ref[0, rows[c], vc] = y.astype(y_ref.dtype)


def _mlstm(proj, small, cw, cb, brow, bcol, nw, l, tb):
    bsz, t, _ = proj.shape
    qk2 = cw.shape[2]
    v_w = nw.shape[2]
    dk = qk2 // 2 // MLSTM_HEADS
    dv = v_w // MLSTM_HEADS
    base = _OFF_MLSTM // v_w
    return pl.pallas_call(
        functools.partial(_mlstm_kernel, tb=tb),
        grid=(bsz, t // tb),
        in_specs=[
            pl.BlockSpec((1, tb, qk2), lambda b, i: (b, i, base)),
            _history_spec(tb, qk2, base),
            pl.BlockSpec((1, tb, v_w), lambda b, i: (b, i, base + 1)),
            pl.BlockSpec((1, tb, v_w), lambda b, i: (b, i, base + 2)),
            pl.BlockSpec((1, tb, SM_W), lambda b, i: (b, i, 0)),
            _layer_spec(cw, l),
            _layer_spec(cb, l),
            _layer_spec(brow, l),
            _layer_spec(bcol, l),
            _layer_spec(nw, l),
        ],
        out_specs=pl.BlockSpec((1, tb, v_w), lambda b, i: (b, i, 0)),
        out_shape=jax.ShapeDtypeStruct((bsz, t, v_w), BF16),
        scratch_shapes=[
            pltpu.VMEM((MLSTM_HEADS, dk, dv), F32),
            pltpu.VMEM((MLSTM_HEADS, 8, dk), F32),
            pltpu.VMEM((MLSTM_HEADS, 8, 128), F32),
        ],
        compiler_params=pltpu.CompilerParams(
            dimension_semantics=("parallel", "arbitrary"), vmem_limit_bytes=VMEM_LIMIT),
        name="mlstm",
    )(proj, proj, proj, proj, small, cw, cb, brow, bcol, nw)


def _ssd_kernel(z_ref, x_ref, xh_ref, bc_ref, bch_ref, sm_ref, cwx_ref, cbx_ref, cwbc_ref, cbbc_ref, brow_ref,
                bcol_ref, alog_row_ref, alog_col_ref, dexp_ref, nw_ref, e_ref, y_ref,
                s_ref, ysc_ref, *, tb):
    inner = x_ref.shape[2]
    gw = inner // SSD_GROUPS
    hpg = SSD_HEADS // SSD_GROUPS
    p = SSD_HEAD_DIM
    z = SSD_STATE

    @pl.when(pl.program_id(1) == 0)
    def _():
        s_ref[...] = jnp.zeros_like(s_ref)

    chunks = range(tb // CHUNK)
    groups = range(SSD_GROUPS)
    causal = _causal()
    tril2 = _tril2()
    triu2 = _triu2()
    shift = _shift_mat(CONV_K)
    neg_inf = -jnp.inf
    a_row = -jnp.exp(alog_row_ref[...])
    a_col = -jnp.exp(alog_col_ref[...])[SM_DT:SM_DT + SSD_HEADS]
    rows = [slice(c * CHUNK, (c + 1) * CHUNK) for c in chunks]

    def conv(xs, cw, cb):
        acc = cb
        for k in range(CONV_K):
            j = CONV_K - 1 - k
            acc = acc + cw[k:k + 1, :] * xs[j * CHUNK:(j + 1) * CHUNK]
        return _silu(acc)

    sm = [sm_ref[0, rows[c], :] for c in chunks]
    dt_col = [_softplus(sm[c] + brow_ref[...]) for c in chunks]
    dt_row = [_softplus((sm[c].T + bcol_ref[...])[SM_DT:SM_DT + SSD_HEADS]) for c in chunks]
    b_col = [_cumsum_rows(tril2, dt_col[c] * a_row) for c in chunks]
    b_row = [_cumsum_cols(dt_row[c] * a_col, triu2) for c in chunks]
    xsx = [jnp.dot(shift, _conv_window(x_ref, xh_ref, c), preferred_element_type=F32) for c in chunks]
    xsb = [jnp.dot(shift, _conv_window(bc_ref, bch_ref, c), preferred_element_type=F32) for c in chunks]
    expd = []
    for c in chunks:
        bl_row = b_col[c][CHUNK - 1:CHUNK, :]
        w_col = jnp.exp(bl_row - b_col[c]) * dt_col[c]
        eb_col = jnp.exp(b_col[c])
        ebl = jnp.broadcast_to(jnp.exp(bl_row), (8, SM_W))
        hi, lo = _split2(jnp.concatenate([w_col, eb_col, ebl], axis=0))
        expd.append(jnp.dot(jnp.concatenate([hi, lo], axis=1), e_ref[...], preferred_element_type=F32))
    xa = [conv(xsx[c], cwx_ref[...], cbx_ref[...]) for c in chunks]
    bca = [conv(xsb[c], cwbc_ref[...], cbbc_ref[...]) for c in chunks]

    for c in chunks:
        w_x = expd[c][0:CHUNK]
        eb_x = expd[c][CHUNK:2 * CHUNK]
        ebl_x = expd[c][2 * CHUNK:2 * CHUNK + 1]
        xg = [xa[c][:, g * gw:(g + 1) * gw] for g in groups]
        bg = [bca[c][:, g * z:(g + 1) * z] for g in groups]
        cg = [bca[c][:, SSD_GROUPS * z + g * z:SSD_GROUPS * z + (g + 1) * z] for g in groups]
        cb = [_dot_nt(cg[g], bg[g]) for g in groups]
        s_prev = [s_ref[g] for g in groups]
        inter = [_dot(cg[g], s_prev[g]) for g in groups]
        dst = [_dot_tn(bg[g], xg[g] * w_x[:, g * gw:(g + 1) * gw]) for g in groups]
        mixes = []
        for h in range(SSD_HEADS):
            bc = b_col[c][:, SM_DT + h:SM_DT + h + 1]
            br = b_row[c][h:h + 1, :]
            dtr = dt_row[c][h:h + 1, :]
            mixes.append(cb[h // hpg] * jnp.exp(jnp.where(causal, bc - br, neg_inf)) * dtr)
        ys = [_dot(mixes[h], xg[h // hpg][:, (h % hpg) * p:(h % hpg + 1) * p]) for h in range(SSD_HEADS)]
        for h in range(SSD_HEADS):
            ysc_ref[:, h * p:(h + 1) * p] = ys[h]
        for g in groups:
            gc = slice(g * gw, (g + 1) * gw)
            s_ref[g] = s_prev[g] * ebl_x[:, gc] + dst[g]
            y = ysc_ref[:, gc] + inter[g] * eb_x[:, gc] + dexp_ref[:, gc] * xg[g]
            y = y * _silu(z_ref[0, rows[c], gc].astype(F32))
            ms = jnp.mean(y * y, axis=-1, keepdims=True)
            y_ref[0, rows[c], gc] = (y * lax.rsqrt(ms + EPS) * nw_ref[:, gc]).astype(y_ref.dtype)


def _ssd(proj, small, cwx, cbx, cwbc, cbbc, brow, bcol, alog_row, alog_col, dexp, nw, emat, l, tb):
    bsz, t, _ = proj.shape
    inner = nw.shape[2]
    bcw = cwbc.shape[2]
    return pl.pallas_call(
        functools.partial(_ssd_kernel, tb=tb),
        grid=(bsz, t // tb),
        in_specs=[
            pl.BlockSpec((1, tb, inner), lambda b, i: (b, i, _OFF_SZ // inner)),
            pl.BlockSpec((1, tb, inner), lambda b, i: (b, i, _OFF_SX // inner)),
            _history_spec(tb, inner, _OFF_SX // inner),
            pl.BlockSpec((1, tb, bcw), lambda b, i: (b, i, _OFF_SBC // bcw)),
            _history_spec(tb, bcw, _OFF_SBC // bcw),
            pl.BlockSpec((1, tb, SM_W), lambda b, i: (b, i, 0)),
            _layer_spec(cwx, l),
            _layer_spec(cbx, l),
            _layer_spec(cwbc, l),
            _layer_spec(cbbc, l),
            _layer_spec(brow, l),
            _layer_spec(bcol, l),
            _layer_spec(alog_row, l),
            _layer_spec(alog_col, l),
            _layer_spec(dexp, l),
            _layer_spec(nw, l),
            pl.BlockSpec(emat.shape, lambda b, i: (0, 0)),
        ],
        out_specs=pl.BlockSpec((1, tb, inner), lambda b, i: (b, i, 0)),
        out_shape=jax.ShapeDtypeStruct((bsz, t, inner), BF16),
        scratch_shapes=[
            pltpu.VMEM((SSD_GROUPS, SSD_STATE, inner // SSD_GROUPS), F32),
            pltpu.VMEM((CHUNK, inner), F32),
        ],
        compiler_params=pltpu.CompilerParams(
            dimension_semantics=("parallel", "arbitrary"), vmem_limit_bytes=VMEM_LIMIT),
        name="ssd",
    )(proj, proj, proj, proj, proj, small, cwx, cbx, cwbc, cbbc, brow, bcol, alog_row, alog_col, dexp, nw, emat)


def _merge_kernel(ya_ref, yb_ref, yc_ref, gates_ref, gb_ref, wb_ref, wo_ref, h_ref, o_ref):
    d = h_ref.shape[1]
    acc = None
    for k, y_ref in enumerate((ya_ref, yb_ref, yc_ref)):
        zk = jnp.dot(y_ref[...], wb_ref[k], preferred_element_type=F32)
        gk = _sigmoid(gates_ref[:, k * d:(k + 1) * d].astype(F32) + gb_ref[:, k * d:(k + 1) * d])
        acc = gk * zk if acc is None else acc + gk * zk
    o_ref[...] = h_ref[...] + jnp.dot(acc.astype(BF16), wo_ref[...], preferred_element_type=F32)


def _merge(ya, yb, yc, proj2, gate_b, w_branch, w_out, h, l, tm):
    m, d = h.shape
    nb = w_branch.shape[1]
    return pl.pallas_call(
        _merge_kernel,
        grid=(m // tm,),
        in_specs=[
            pl.BlockSpec((tm, d), lambda i: (i, 0)),
            pl.BlockSpec((tm, d), lambda i: (i, 0)),
            pl.BlockSpec((tm, d), lambda i: (i, 0)),
            pl.BlockSpec((tm, nb * d), lambda i: (i, _OFF_GATES // (nb * d))),
            _layer_spec(gate_b, l),
            _layer_spec(w_branch, l),
            _layer_spec(w_out, l),
            pl.BlockSpec((tm, d), lambda i: (i, 0)),
        ],
        out_specs=pl.BlockSpec((tm, d), lambda i: (i, 0)),
        out_shape=jax.ShapeDtypeStruct((m, d), F32),
        compiler_params=pltpu.CompilerParams(
            dimension_semantics=("parallel",), vmem_limit_bytes=VMEM_LIMIT),
        name="merge",
    )(ya, yb, yc, proj2, gate_b, w_branch, w_out, h)


def _ffn_kernel(h_ref, g_ref, wg_ref, wv_ref, cwg_ref, cwv_ref, cbg_ref, cbv_ref, wd_ref, gout_ref, o_ref,
                gpad_ref, vpad_ref, *, tm, norm_out):
    @pl.when(pl.program_id(1) == 0)
    def _():
        gpad_ref[0:8, :] = jnp.zeros((8, gpad_ref.shape[1]), F32)
        vpad_ref[0:8, :] = jnp.zeros((8, vpad_ref.shape[1]), F32)

    x = h_ref[0]
    ms = jnp.mean(x * x, axis=-1, keepdims=True)
    xn = (x * lax.rsqrt(ms + EPS) * g_ref[...]).astype(BF16)
    gpad_ref[8:8 + tm, :] = jnp.dot(xn, wg_ref[...], preferred_element_type=F32)
    vpad_ref[8:8 + tm, :] = jnp.dot(xn, wv_ref[...], preferred_element_type=F32)
    gate = cbg_ref[...]
    val = cbv_ref[...]
    for k in range(FFN_CONV):
        s = FFN_CONV - 1 - k
        gate = gate + cwg_ref[k:k + 1, :] * gpad_ref[8 - s:8 - s + tm, :]
        val = val + cwv_ref[k:k + 1, :] * vpad_ref[8 - s:8 - s + tm, :]
    gpad_ref[0:8, :] = gpad_ref[tm:tm + 8, :]
    vpad_ref[0:8, :] = vpad_ref[tm:tm + 8, :]
    act = (_silu(gate) * val).astype(BF16)
    r = x + jnp.dot(act, wd_ref[...], preferred_element_type=F32)
    if norm_out:
        ms = jnp.mean(r * r, axis=-1, keepdims=True)
        r = r * lax.rsqrt(ms + EPS) * gout_ref[...]
    o_ref[0] = r


def _ffn(h3, g, w_up, cw, cb, w_down, g_out, l, tm, norm_out):
    bsz, t, d = h3.shape
    dff = w_down.shape[1]
    resident = dict(pipeline_mode=pl.Buffered(1))
    return pl.pallas_call(
        functools.partial(_ffn_kernel, tm=tm, norm_out=norm_out),
        grid=(bsz, t // tm),
        in_specs=[
            pl.BlockSpec((1, tm, d), lambda b, i: (b, i, 0)),
            _layer_spec(g, l),
            pl.BlockSpec((None, d, dff), lambda b, i: (l, 0, 0), **resident),
            pl.BlockSpec((None, d, dff), lambda b, i: (l, 0, 1), **resident),
            pl.BlockSpec((None, FFN_CONV, dff), lambda b, i: (l, 0, 0)),
            pl.BlockSpec((None, FFN_CONV, dff), lambda b, i: (l, 0, 1)),
            pl.BlockSpec((None, 1, dff), lambda b, i: (l, 0, 0)),
            pl.BlockSpec((None, 1, dff), lambda b, i: (l, 0, 1)),
            pl.BlockSpec((None, dff, d), lambda b, i: (l, 0, 0), **resident),
            pl.BlockSpec((1, d), lambda b, i: (0, 0)),
        ],
        out_specs=pl.BlockSpec((1, tm, d), lambda b, i: (b, i, 0)),
        out_shape=jax.ShapeDtypeStruct((bsz, t, d), F32),
        scratch_shapes=[
            pltpu.VMEM((tm + 8, dff), F32),
            pltpu.VMEM((tm + 8, dff), F32),
        ],
        compiler_params=pltpu.CompilerParams(
            dimension_semantics=("parallel", "arbitrary"), vmem_limit_bytes=VMEM_LIMIT),
        name="conv_ffn",
    )(h3, g, w_up, w_up, cw, cw, cb, cb, w_down, g_out)


_RL_TILE = 512


def _relayout_kernel(src_ref, ga_ref, mg_ref, dt_ref, o_ref, *, n_copy):
    k = pl.program_id(1)

    @pl.when(k < n_copy)
    def _():
        o_ref[...] = src_ref[0].astype(BF16)

    @pl.when(k == n_copy)
    def _():
        rows = ga_ref.shape[1] + mg_ref.shape[1] + dt_ref.shape[1]
        sm = jnp.concatenate([ga_ref[0], mg_ref[0], dt_ref[0],
                              jnp.zeros((SM_W - rows, o_ref.shape[1]), F32)], axis=0)
        o_ref[0:SM_W, :] = sm.astype(BF16)
        o_ref[SM_W:, :] = jnp.zeros((o_ref.shape[0] - SM_W, o_ref.shape[1]), BF16)


def _relayout(w_t, segments, small_srcs):
    depth, _, d = w_t.shape
    n_copy = _OFF_SMALL // _RL_TILE
    assert all(ln % _RL_TILE == 0 and dst % _RL_TILE == 0 and src % 8 == 0 for dst, src, ln in segments)
    assert sorted(dst for dst, _, _ in segments)[0] == 0 and sum(ln for _, _, ln in segments) == _OFF_SMALL

    def src_row(l, k):
        r = k * _RL_TILE
        row = jnp.zeros_like(r)
        for dst, src, ln in segments:
            row = jnp.where((r >= dst) & (r < dst + ln), r + (src - dst), row)
        return (l, pl.multiple_of(row, 8), 0)

    elem = lambda rows, imap: pl.BlockSpec((pl.Element(1), pl.Element(rows), pl.Element(d)), imap)
    small = [elem(n, lambda l, k, s=src: (l, s, 0)) for src, n in small_srcs]
    return pl.pallas_call(
        functools.partial(_relayout_kernel, n_copy=n_copy),
        grid=(depth, n_copy + 1),
        in_specs=[elem(_RL_TILE, src_row)] + small,
        out_specs=pl.BlockSpec((None, _RL_TILE, d), lambda l, k: (l, k, 0)),
        out_shape=jax.ShapeDtypeStruct((depth, _N_PROJ, d), BF16),
        compiler_params=pltpu.CompilerParams(dimension_semantics=("parallel", "arbitrary")),
        name="w_relayout",
    )(w_t, w_t, w_t, w_t)


_D = 1024
_OFF_MLSTM = 3 * _D
_OFF_GATES = 6 * _D
_OFF_SZ = 9 * _D
_OFF_SX = 10 * _D
_OFF_SBC = 11 * _D
_OFF_SMALL = 11 * _D + _D // 2
_N_PROJ = 12 * _D


def _pick(n, cands):
    for c in cands:
        if n % c == 0:
            return c
    return n


def kernel(x, norm_mix, w_in, gla_wa, gla_ba, gla_norm, mlstm_conv_w, mlstm_conv_b, mlstm_bi, mlstm_bf,
           mlstm_norm, ssd_conv_w, ssd_conv_b, ssd_dt_bias, ssd_a_log, ssd_d, ssd_norm, gate_b, w_branch,
           w_out, norm_ffn, w_up, ffn_conv_w, ffn_conv_b, w_down, norm_final):
    bsz, t, d = x.shape
    assert d == _D
    depth = w_in.shape[0]
    m = bsz * t
    qk = d // 2
    rank = gla_wa.shape[1]
    bc_w = 2 * SSD_GROUPS * SSD_STATE
    sizes = (qk, qk, d, rank, d, 2 * qk, d, MLSTM_HEADS, MLSTM_HEADS, d, d, d + bc_w, SSD_HEADS, 3 * d)
    offs = [0]
    for s in sizes:
        offs.append(offs[-1] + s)
    (o_gq, o_gk, o_gv, o_ga, o_gg, o_mqk, o_mv, o_mi, o_mf, o_mo, o_sz, o_sxbc, o_sdt, o_gates, o_end) = offs
    assert o_end == w_in.shape[2]

    segments = ((0, o_gq, o_ga - o_gq), (o_ga - o_gq, o_gg, o_mi - o_gg), (_OFF_MLSTM + 2 * d, o_mo, o_sz - o_mo),
                (_OFF_GATES, o_gates, o_end - o_gates), (_OFF_SZ, o_sz, o_sdt - o_sz))
    assert (SM_GA, SM_MI, SM_DT) == (0, rank, rank + 2 * MLSTM_HEADS)
    w_perm = _relayout(jnp.swapaxes(w_in, 1, 2), segments,
                       ((o_ga, rank), (o_mi, 2 * MLSTM_HEADS), (o_sdt, SSD_HEADS)))

    row3 = lambda a: a[:, None, :]
    col3 = lambda a: a[:, :, None]
    pad_small = lambda a, off: jnp.pad(a, ((0, 0), (off, SM_W - off - a.shape[1])))
    wa_pad = jnp.pad(gla_wa, ((0, 0), (SM_GA, SM_W - SM_GA - rank), (0, 0)))
    wa_hi = wa_pad.astype(BF16)
    wa_lo = (wa_pad - wa_hi.astype(F32)).astype(BF16)
    wa3 = jnp.concatenate([wa_hi, wa_hi, wa_lo], axis=1)
    m_bias = pad_small(mlstm_bi, SM_MI) + pad_small(mlstm_bf, SM_MF)
    s_bias = pad_small(ssd_dt_bias, SM_DT)
    alog = pad_small(ssd_a_log, SM_DT)
    dexp = jnp.repeat(ssd_d, SSD_HEAD_DIM, axis=-1)
    head_of_col = jnp.arange(d) // SSD_HEAD_DIM
    emat1 = (jnp.arange(SM_W)[:, None] == (SM_DT + head_of_col)[None, :]).astype(BF16)
    emat = jnp.concatenate([emat1, emat1], axis=0)
    w_branch_b = w_branch.astype(BF16)
    w_out_b = w_out.astype(BF16)
    w_up_b = w_up.astype(BF16)
    w_down_b = w_down.astype(BF16)
    norm_mix3, norm_ffn3 = row3(norm_mix), row3(norm_ffn)
    gla_ba3, gla_norm3 = row3(gla_ba), row3(gla_norm)
    m_cb3, m_norm3 = row3(mlstm_conv_b), row3(mlstm_norm)
    s_cwx, s_cwbc = ssd_conv_w[:, :, :d], ssd_conv_w[:, :, d:]
    s_cbx, s_cbbc = row3(ssd_conv_b[:, :d]), row3(ssd_conv_b[:, d:])
    s_norm3, dexp3, gate_b3, f_cb3 = row3(ssd_norm), row3(dexp), row3(gate_b), row3(ffn_conv_b)

    tm_proj = _pick(m, (1024, 512, 256, 128, 64))
    tn_proj = _pick(_N_PROJ, (3072, 2048, 1024))
    tb = _pick(t, (512, 256, 128, 64))
    tm_merge = _pick(m, (512, 256, 128, 64))
    tm_ffn = _pick(t, (512, 256, 128, 64))

    h = x.reshape(m, d)
    for l in range(depth):
        proj2, small2 = _proj(h, norm_mix3, w_perm, l, tm_proj, tn_proj)
        proj3 = proj2.reshape(bsz, t, _N_PROJ)
        small3 = small2.reshape(bsz, t, SM_W)
        y_gla = _gla(proj3, small3, wa3, gla_ba3, gla_norm3, l, tb)
        y_m = _mlstm(proj3, small3, mlstm_conv_w, m_cb3, row3(m_bias), col3(m_bias), m_norm3, l, tb)
        y_s = _ssd(proj3, small3, s_cwx, s_cbx, s_cwbc, s_cbbc, row3(s_bias), col3(s_bias), row3(alog),
                   col3(alog), dexp3, s_norm3, emat, l, tb)
        h = _merge(y_gla.reshape(m, d), y_m.reshape(m, d), y_s.reshape(m, d), proj2, gate_b3,
                   w_branch_b, w_out_b, h, l, tm_merge)
        h = _ffn(h.reshape(bsz, t, d), norm_ffn3, w_up_b, ffn_conv_w, f_cb3, w_down_b, norm_final[None, :], l,
                 tm_ffn, norm_out=(l == depth - 1)).reshape(m, d)
    return h.reshape(bsz, t, d)
```

```python
import functools

import jax
import jax.numpy as jnp
from jax import lax
from jax.experimental import pallas as pl
from jax.experimental.pallas import tpu as pltpu

F32 = jnp.float32
BF16 = jnp.bfloat16

EPS = 1e-6
CHUNK = 64
GLA_HEADS = 4
GLA_TAU = 16.0
MLSTM_HEADS = 4
SSD_HEADS = 16
SSD_GROUPS = 2
SSD_STATE = 128
SSD_HEAD_DIM = 64
FFN_CONV = 3
CONV_K = 4
PAD = 16

SM_GA = 0
SM_MI = 16
SM_MF = 20
SM_DT = 24
SM_W = 128

VMEM_LIMIT = 52 * 1024 * 1024


def _dot(a, b):
    return jnp.dot(a.astype(BF16), b.astype(BF16), preferred_element_type=F32)


def _dot_nt(a, b):
    return lax.dot_general(a.astype(BF16), b.astype(BF16), (((1,), (1,)), ((), ())),
                           preferred_element_type=F32)


def _dot_tn(a, b):
    return jnp.dot(a.astype(F32).T.astype(BF16), b.astype(BF16), preferred_element_type=F32)


def _split2(x):
    hi = x.astype(BF16)
    return hi, (x - hi.astype(F32)).astype(BF16)


def _cumsum_rows(tril2, x):
    return jnp.dot(tril2, jnp.concatenate(_split2(x), axis=0), preferred_element_type=F32)


def _cumsum_cols(x, triu2):
    return jnp.dot(jnp.concatenate(_split2(x), axis=1), triu2, preferred_element_type=F32)


def _sigmoid(x):
    return jax.nn.sigmoid(x)


def _silu(x):
    return x * _sigmoid(x)


def _softplus(x):
    return jnp.maximum(x, 0.0) + jnp.log1p(jnp.exp(-jnp.abs(x)))


def _log_sigmoid(x):
    return jnp.minimum(x, 0.0) - jnp.log1p(jnp.exp(-jnp.abs(x)))


def _causal():
    row = lax.broadcasted_iota(jnp.int32, (CHUNK, CHUNK), 0)
    col = lax.broadcasted_iota(jnp.int32, (CHUNK, CHUNK), 1)
    return col <= row


def _tril2():
    row = lax.broadcasted_iota(jnp.int32, (CHUNK, 2 * CHUNK), 0)
    col = lax.broadcasted_iota(jnp.int32, (CHUNK, 2 * CHUNK), 1)
    return jnp.where((col <= row) | ((col >= CHUNK) & (col - CHUNK <= row)), 1.0, 0.0).astype(BF16)


def _triu2():
    row = lax.broadcasted_iota(jnp.int32, (2 * CHUNK, CHUNK), 0)
    col = lax.broadcasted_iota(jnp.int32, (2 * CHUNK, CHUNK), 1)
    return jnp.where((row <= col) | ((row >= CHUNK) & (row - CHUNK <= col)), 1.0, 0.0).astype(BF16)


def _shift_mat(taps):
    row = lax.broadcasted_iota(jnp.int32, (taps * CHUNK, PAD + CHUNK), 0)
    col = lax.broadcasted_iota(jnp.int32, (taps * CHUNK, PAD + CHUNK), 1)
    sel = None
    for j in range(taps):
        hit = (row >= j * CHUNK) & (row < (j + 1) * CHUNK) & (col == row - j * CHUNK + PAD - j)
        sel = hit if sel is None else sel | hit
    return jnp.where(sel, 1.0, 0.0).astype(BF16)


def _layer_spec(arr, l):
    zeros = (0,) * (arr.ndim - 1)
    return pl.BlockSpec((None,) + arr.shape[1:], lambda *_: (l,) + zeros)


def _history_spec(tb, width, col_block):
    per = tb // PAD
    return pl.BlockSpec((1, PAD, width), lambda b, i: (b, jnp.maximum(i * per - 1, 0), col_block))


def _conv_window(x_ref, hist_ref, c):
    if c > 0:
        return x_ref[0, c * CHUNK - PAD:(c + 1) * CHUNK, :]
    hist = hist_ref[0]
    hist = jnp.where(pl.program_id(1) > 0, hist, jnp.zeros_like(hist))
    return jnp.concatenate([hist, x_ref[0, 0:CHUNK, :]], axis=0)


def _proj_kernel(x_ref, g_ref, w_ref, o_ref, sm_ref, xn_ref, *, sm_tile, sm_col):
    j = pl.program_id(1)

    @pl.when(j == 0)
    def _():
        x = x_ref[...]
        ms = jnp.mean(x * x, axis=-1, keepdims=True)
        xn_ref[...] = (x * lax.rsqrt(ms + EPS) * g_ref[...]).astype(BF16)

    acc = lax.dot_general(xn_ref[...], w_ref[...], (((1,), (1,)), ((), ())), preferred_element_type=F32)
    o_ref[...] = acc.astype(o_ref.dtype)

    @pl.when(j == sm_tile)
    def _():
        sm_ref[...] = acc[:, sm_col:sm_col + SM_W]


def _proj(x, g, w, l, tm, tn):
    m, d = x.shape
    n = w.shape[1]
    return pl.pallas_call(
        functools.partial(_proj_kernel, sm_tile=_OFF_SMALL // tn, sm_col=_OFF_SMALL % tn),
        grid=(m // tm, n // tn),
        in_specs=[
            pl.BlockSpec((tm, d), lambda i, j: (i, 0)),
            _layer_spec(g, l),
            pl.BlockSpec((None, tn, d), lambda i, j: (l, j, 0)),
        ],
        out_specs=[
            pl.BlockSpec((tm, tn), lambda i, j: (i, j)),
            pl.BlockSpec((tm, SM_W), lambda i, j: (i, 0)),
        ],
        out_shape=[jax.ShapeDtypeStruct((m, n), BF16), jax.ShapeDtypeStruct((m, SM_W), F32)],
        scratch_shapes=[pltpu.VMEM((tm, d), BF16)],
        compiler_params=pltpu.CompilerParams(
            dimension_semantics=("parallel", "arbitrary"), vmem_limit_bytes=VMEM_LIMIT),
        name="in_proj",
    )(x, g, w)


def _gla_kernel(q_ref, k_ref, v_ref, gg_ref, sm_ref, wa_ref, ba_ref, nw_ref, y_ref, st_ref, *, tb):
    dk = q_ref.shape[2] // GLA_HEADS
    dv = v_ref.shape[2] // GLA_HEADS

    @pl.when(pl.program_id(1) == 0)
    def _():
        st_ref[...] = jnp.zeros_like(st_ref)

    chunks = range(tb // CHUNK)
    heads = range(GLA_HEADS)
    causal = _causal()
    tril2 = _tril2()
    scale = dk ** -0.5
    rows = [slice(c * CHUNK, (c + 1) * CHUNK) for c in chunks]

    pre = []
    for c in chunks:
        hi, lo = _split2(sm_ref[0, rows[c], :])
        pre.append(jnp.dot(jnp.concatenate([hi, lo, hi], axis=1), wa_ref[...], preferred_element_type=F32))
    la = [_log_sigmoid(pre[c] + ba_ref[...]) * (1.0 / GLA_TAU) for c in chunks]
    bcum = [_cumsum_rows(tril2, la[c]) for c in chunks]

    for c in chunks:
        r = rows[c]
        kcs = [slice(h * dk, (h + 1) * dk) for h in heads]
        vcs = [slice(h * dv, (h + 1) * dv) for h in heads]
        b = [bcum[c][:, kcs[h]] for h in heads]
        bl = [b[h][CHUNK - 1:CHUNK, :] for h in heads]
        q = [q_ref[0, r, kcs[h]].astype(F32) * scale for h in heads]
        k = [k_ref[0, r, kcs[h]].astype(F32) for h in heads]
        v = [v_ref[0, r, vcs[h]] for h in heads]
        qd = [q[h] * jnp.exp(b[h]) for h in heads]
        ki = [k[h] * jnp.exp(-b[h]) for h in heads]
        kd = [k[h] * jnp.exp(bl[h] - b[h]) for h in heads]
        st = [st_ref[h] for h in heads]
        sc = [_dot_nt(qd[h], ki[h]) for h in heads]
        o2 = [_dot_nt(qd[h], st[h]) for h in heads]
        ds = [_dot_tn(v[h], kd[h]) for h in heads]
        o1 = [_dot(jnp.where(causal, sc[h], 0.0), v[h]) for h in heads]
        for h in heads:
            st_ref[h] = st[h] * jnp.exp(bl[h]) + ds[h]
            o = o1[h] + o2[h]
            ms = jnp.mean(o * o, axis=-1, keepdims=True)
            y = (o * lax.rsqrt(ms + EPS) * nw_ref[:, vcs[h]]) * _silu(gg_ref[0, r, vcs[h]].astype(F32))
            y_ref[0, r, vcs[h]] = y.astype(y_ref.dtype)


def _gla(proj, small, wa3, ba, nw, l, tb):
    bsz, t, _ = proj.shape
    qk_w = ba.shape[2]
    v_w = nw.shape[2]
    dk = qk_w // GLA_HEADS
    dv = v_w // GLA_HEADS
    return pl.pallas_call(
        functools.partial(_gla_kernel, tb=tb),
        grid=(bsz, t // tb),
        in_specs=[
            pl.BlockSpec((1, tb, qk_w), lambda b, i: (b, i, 0)),
            pl.BlockSpec((1, tb, qk_w), lambda b, i: (b, i, 1)),
            pl.BlockSpec((1, tb, v_w), lambda b, i: (b, i, 1)),
            pl.BlockSpec((1, tb, v_w), lambda b, i: (b, i, 2)),
            pl.BlockSpec((1, tb, SM_W), lambda b, i: (b, i, 0)),
            _layer_spec(wa3, l),
            _layer_spec(ba, l),
            _layer_spec(nw, l),
        ],
        out_specs=pl.BlockSpec((1, tb, v_w), lambda b, i: (b, i, 0)),
        out_shape=jax.ShapeDtypeStruct((bsz, t, v_w), BF16),
        scratch_shapes=[pltpu.VMEM((GLA_HEADS, dv, dk), F32)],
        compiler_params=pltpu.CompilerParams(
            dimension_semantics=("parallel", "arbitrary"), vmem_limit_bytes=VMEM_LIMIT),
        name="gla",
    )(proj, proj, proj, proj, small, wa3, ba, nw)


def _mlstm_kernel(qk_ref, qkh_ref, v_ref, og_ref, sm_ref, cw_ref, cb_ref, brow_ref, bcol_ref, nw_ref, y_ref,
                  c_ref, n_ref, m_ref, *, tb):
    qk_w = qk_ref.shape[2] // 2
    dk = qk_w // MLSTM_HEADS
    dv = v_ref.shape[2] // MLSTM_HEADS

    @pl.when(pl.program_id(1) == 0)
    def _():
        c_ref[...] = jnp.zeros_like(c_ref)
        n_ref[...] = jnp.zeros_like(n_ref)
        m_ref[...] = jnp.zeros_like(m_ref)

    chunks = range(tb // CHUNK)
    heads = range(MLSTM_HEADS)
    pairs = [(c, h) for c in chunks for h in heads]
    causal = _causal()
    tril2 = _tril2()
    triu2 = _triu2()
    shift = _shift_mat(CONV_K)
    kscale = dk ** -0.5
    neg_inf = -jnp.inf
    rows = [slice(c * CHUNK, (c + 1) * CHUNK) for c in chunks]

    sm = [sm_ref[0, rows[c], :] for c in chunks]
    pre_col = [sm[c] + brow_ref[...] for c in chunks]
    pre_row = [(sm[c].T + bcol_ref[...])[SM_MI:SM_MI + 2 * MLSTM_HEADS] for c in chunks]
    b_col = [_cumsum_rows(tril2, _log_sigmoid(pre_col[c])) for c in chunks]
    b_row = [_cumsum_cols(_log_sigmoid(pre_row[c]), triu2) for c in chunks]
    xs = [jnp.dot(shift, _conv_window(qk_ref, qkh_ref, c), preferred_element_type=F32) for c in chunks]
    act = []
    for c in chunks:
        acc = cb_ref[...]
        for k in range(CONV_K):
            j = CONV_K - 1 - k
            acc = acc + cw_ref[k:k + 1, :] * xs[c][j * CHUNK:(j + 1) * CHUNK]
        act.append(_silu(acc))

    bc, br, ir, bl, m_loc, w = {}, {}, {}, {}, {}, {}
    for c, h in pairs:
        bc[c, h] = b_col[c][:, SM_MF + h:SM_MF + h + 1]
        br[c, h] = b_row[c][MLSTM_HEADS + h:MLSTM_HEADS + h + 1, :]
        ir[c, h] = pre_row[c][h:h + 1, :]
        bl[c, h] = bc[c, h][CHUNK - 1:CHUNK, :]
        a = bl[c, h] - bc[c, h] + pre_col[c][:, SM_MI + h:SM_MI + h + 1]
        m_loc[c, h] = jnp.max(a, axis=0, keepdims=True)
        w[c, h] = jnp.exp(a - m_loc[c, h])
    m_prev, s_old, s_new = {}, {}, {}
    for h in heads:
        m = m_ref[h][0:1, 0:1]
        for c in chunks:
            m_prev[c, h] = m
            m_new = jnp.maximum(bl[c, h] + m, m_loc[c, h])
            s_old[c, h] = jnp.exp(bl[c, h] + m - m_new)
            s_new[c, h] = jnp.exp(m_loc[c, h] - m_new)
            m = m_new
        m_ref[h] = jnp.broadcast_to(m, m_ref.shape[1:])

    q = {(c, h): act[c][:, h * dk:(h + 1) * dk] for c, h in pairs}
    k = {(c, h): act[c][:, qk_w + h * dk:qk_w + (h + 1) * dk] * kscale for c, h in pairs}
    v = {(c, h): v_ref[0, rows[c], h * dv:(h + 1) * dv] for c, h in pairs}
    wk = {p: w[p] * k[p] for p in pairs}

    qk = {p: _dot_nt(q[p], k[p]) for p in pairs}
    dcs = {p: _dot_tn(wk[p], v[p]) for p in pairs}

    c_prev, n_prev = {}, {}
    for h in heads:
        cst = c_ref[h]
        nst = n_ref[h][0:1, :]
        for c in chunks:
            c_prev[c, h] = cst.astype(BF16)
            n_prev[c, h] = nst
            cst = s_old[c, h] * cst + s_new[c, h] * dcs[c, h]
            nst = s_old[c, h] * nst + s_new[c, h] * jnp.sum(wk[c, h], axis=0, keepdims=True)
        c_ref[h] = cst
        n_ref[h] = jnp.broadcast_to(nst, n_ref.shape[1:])

    log_d = {p: jnp.where(causal, bc[p] - br[p] + ir[p], neg_inf) for p in pairs}
    m_inter = {p: bc[p] + m_prev[p] for p in pairs}
    row_max = {p: jnp.max(log_d[p], axis=-1, keepdims=True) for p in pairs}
    m_t = {p: jnp.maximum(m_inter[p], row_max[p]) for p in pairs}
    wts = {p: qk[p] * jnp.exp(log_d[p] - m_t[p]) for p in pairs}
    s_inter = {p: jnp.exp(m_inter[p] - m_t[p]) for p in pairs}

    pv = {p: _dot(wts[p], v[p]) for p in pairs}
    qc = {p: _dot(q[p], c_prev[p]) for p in pairs}

    row_sum = {p: jnp.sum(wts[p], axis=-1, keepdims=True) for p in pairs}
    qn = {p: jnp.sum(q[p] * n_prev[p], axis=-1, keepdims=True) for p in pairs}
    num = {p: pv[p] + s_inter[p] * qc[p] for p in pairs}
    den = {p: row_sum[p] + s_inter[p] * qn[p] for p in pairs}
    hout = {p: num[p] / jnp.maximum(jnp.abs(den[p]), jnp.exp(-m_t[p])) for p in pairs}
    ms = {p: jnp.mean(hout[p] * hout[p], axis=-1, keepdims=True) for p in pairs}
    for c, h in pairs:
        p = (c, h)
        vc = slice(h * dv, (h + 1) * dv)
        y = (hout[p] * lax.rsqrt(ms[p] + EPS) * nw_ref[:, vc]) * _sigmoid(og_ref[0, rows[c], vc].astype(F32))
        y_ref[0, rows[c], vc] = y.astype(y_ref.dtype)


def _mlstm(proj, small, cw, cb, brow, bcol, nw, l, tb):
    bsz, t, _ = proj.shape
    qk2 = cw.shape[2]
    v_w = nw.shape[2]
    dk = qk2 // 2 // MLSTM_HEADS
    dv = v_w // MLSTM_HEADS
    base = _OFF_MLSTM // v_w
    return pl.pallas_call(
        functools.partial(_mlstm_kernel, tb=tb),
        grid=(bsz, t // tb),
        in_specs=[
            pl.BlockSpec((1, tb, qk2), lambda b, i: (b, i, base)),
            _history_spec(tb, qk2, base),
            pl.BlockSpec((1, tb, v_w), lambda b, i: (b, i, base + 1)),
            pl.BlockSpec((1, tb, v_w), lambda b, i: (b, i, base + 2)),
            pl.BlockSpec((1, tb, SM_W), lambda b, i: (b, i, 0)),
            _layer_spec(cw, l),
            _layer_spec(cb, l),
            _layer_spec(brow, l),
            _layer_spec(bcol, l),
            _layer_spec(nw, l),
        ],
        out_specs=pl.BlockSpec((1, tb, v_w), lambda b, i: (b, i, 0)),
        out_shape=jax.ShapeDtypeStruct((bsz, t, v_w), BF16),
        scratch_shapes=[
            pltpu.VMEM((MLSTM_HEADS, dk, dv), F32),
            pltpu.VMEM((MLSTM_HEADS, 8, dk), F32),
            pltpu.VMEM((MLSTM_HEADS, 8, 128), F32),
        ],
        compiler_params=pltpu.CompilerParams(
            dimension_semantics=("parallel", "arbitrary"), vmem_limit_bytes=VMEM_LIMIT),
        name="mlstm",
    )(proj, proj, proj, proj, small, cw, cb, brow, bcol, nw)


def _ssd_kernel(z_ref, x_ref, xh_ref, bc_ref, bch_ref, sm_ref, cwx_ref, cbx_ref, cwbc_ref, cbbc_ref, brow_ref,
                bcol_ref, alog_row_ref, alog_col_ref, dexp_ref, nw_ref, e_ref, y_ref,
                s_ref, ysc_ref, *, tb):
    inner = x_ref.shape[2]
    gw = inner // SSD_GROUPS
    hpg = SSD_HEADS // SSD_GROUPS
    p = SSD_HEAD_DIM
    z = SSD_STATE

    @pl.when(pl.program_id(1) == 0)
    def _():
        s_ref[...] = jnp.zeros_like(s_ref)

    chunks = range(tb // CHUNK)
    groups = range(SSD_GROUPS)
    causal = _causal()
    tril2 = _tril2()
    triu2 = _triu2()
    shift = _shift_mat(CONV_K)
    neg_inf = -jnp.inf
    a_row = -jnp.exp(alog_row_ref[...])
    a_col = -jnp.exp(alog_col_ref[...])[SM_DT:SM_DT + SSD_HEADS]
    rows = [slice(c * CHUNK, (c + 1) * CHUNK) for c in chunks]

    def conv(xs, cw, cb):
        acc = cb
        for k in range(CONV_K):
            j = CONV_K - 1 - k
            acc = acc + cw[k:k + 1, :] * xs[j * CHUNK:(j + 1) * CHUNK]
        return _silu(acc)

    sm = [sm_ref[0, rows[c], :] for c in chunks]
    dt_col = [_softplus(sm[c] + brow_ref[...]) for c in chunks]
    dt_row = [_softplus((sm[c].T + bcol_ref[...])[SM_DT:SM_DT + SSD_HEADS]) for c in chunks]
    b_col = [_cumsum_rows(tril2, dt_col[c] * a_row) for c in chunks]
    b_row = [_cumsum_cols(dt_row[c] * a_col, triu2) for c in chunks]
    xsx = [jnp.dot(shift, _conv_window(x_ref, xh_ref, c), preferred_element_type=F32) for c in chunks]
    xsb = [jnp.dot(shift, _conv_window(bc_ref, bch_ref, c), preferred_element_type=F32) for c in chunks]
    expd = []
    for c in chunks:
        bl_row = b_col[c][CHUNK - 1:CHUNK, :]
        w_col = jnp.exp(bl_row - b_col[c]) * dt_col[c]
        eb_col = jnp.exp(b_col[c])
        ebl = jnp.broadcast_to(jnp.exp(bl_row), (8, SM_W))
        hi, lo = _split2(jnp.concatenate([w_col, eb_col, ebl], axis=0))
        expd.append(jnp.dot(jnp.concatenate([hi, lo], axis=1), e_ref[...], preferred_element_type=F32))
    xa = [conv(xsx[c], cwx_ref[...], cbx_ref[...]) for c in chunks]
    bca = [conv(xsb[c], cwbc_ref[...], cbbc_ref[...]) for c in chunks]

    for c in chunks:
        w_x = expd[c][0:CHUNK]
        eb_x = expd[c][CHUNK:2 * CHUNK]
        ebl_x = expd[c][2 * CHUNK:2 * CHUNK + 1]
        xg = [xa[c][:, g * gw:(g + 1) * gw] for g in groups]
        bg = [bca[c][:, g * z:(g + 1) * z] for g in groups]
        cg = [bca[c][:, SSD_GROUPS * z + g * z:SSD_GROUPS * z + (g + 1) * z] for g in groups]
        cb = [_dot_nt(cg[g], bg[g]) for g in groups]
        s_prev = [s_ref[g] for g in groups]
        inter = [_dot(cg[g], s_prev[g]) for g in groups]
        dst = [_dot_tn(bg[g], xg[g] * w_x[:, g * gw:(g + 1) * gw]) for g in groups]
        mixes = []
        for h in range(SSD_HEADS):
            bc = b_col[c][:, SM_DT + h:SM_DT + h + 1]
            br = b_row[c][h:h + 1, :]
            dtr = dt_row[c][h:h + 1, :]
            mixes.append(cb[h // hpg] * jnp.exp(jnp.where(causal, bc - br, neg_inf)) * dtr)
        ys = [_dot(mixes[h], xg[h // hpg][:, (h % hpg) * p:(h % hpg + 1) * p]) for h in range(SSD_HEADS)]
        for h in range(SSD_HEADS):
            ysc_ref[:, h * p:(h + 1) * p] = ys[h]
        for g in groups:
            gc = slice(g * gw, (g + 1) * gw)
            s_ref[g] = s_prev[g] * ebl_x[:, gc] + dst[g]
            y = ysc_ref[:, gc] + inter[g] * eb_x[:, gc] + dexp_ref[:, gc] * xg[g]
            y = y * _silu(z_ref[0, rows[c], gc].astype(F32))
            ms = jnp.mean(y * y, axis=-1, keepdims=True)
            y_ref[0, rows[c], gc] = (y * lax.rsqrt(ms + EPS) * nw_ref[:, gc]).astype(y_ref.dtype)


def _ssd(proj, small, cwx, cbx, cwbc, cbbc, brow, bcol, alog_row, alog_col, dexp, nw, emat, l, tb):
    bsz, t, _ = proj.shape
    inner = nw.shape[2]
    bcw = cwbc.shape[2]
    return pl.pallas_call(
        functools.partial(_ssd_kernel, tb=tb),
        grid=(bsz, t // tb),
        in_specs=[
            pl.BlockSpec((1, tb, inner), lambda b, i: (b, i, _OFF_SZ // inner)),
            pl.BlockSpec((1, tb, inner), lambda b, i: (b, i, _OFF_SX // inner)),
            _history_spec(tb, inner, _OFF_SX // inner),
            pl.BlockSpec((1, tb, bcw), lambda b, i: (b, i, _OFF_SBC // bcw)),
            _history_spec(tb, bcw, _OFF_SBC // bcw),
            pl.BlockSpec((1, tb, SM_W), lambda b, i: (b, i, 0)),
            _layer_spec(cwx, l),
            _layer_spec(cbx, l),
            _layer_spec(cwbc, l),
            _layer_spec(cbbc, l),
            _layer_spec(brow, l),
            _layer_spec(bcol, l),
            _layer_spec(alog_row, l),
            _layer_spec(alog_col, l),
            _layer_spec(dexp, l),
            _layer_spec(nw, l),
            pl.BlockSpec(emat.shape, lambda b, i: (0, 0)),
        ],
        out_specs=pl.BlockSpec((1, tb, inner), lambda b, i: (b, i, 0)),
        out_shape=jax.ShapeDtypeStruct((bsz, t, inner), BF16),
        scratch_shapes=[
            pltpu.VMEM((SSD_GROUPS, SSD_STATE, inner // SSD_GROUPS), F32),
            pltpu.VMEM((CHUNK, inner), F32),
        ],
        compiler_params=pltpu.CompilerParams(
            dimension_semantics=("parallel", "arbitrary"), vmem_limit_bytes=VMEM_LIMIT),
        name="ssd",
    )(proj, proj, proj, proj, proj, small, cwx, cbx, cwbc, cbbc, brow, bcol, alog_row, alog_col, dexp, nw, emat)


def _merge_kernel(ya_ref, yb_ref, yc_ref, gates_ref, gb_ref, wb_ref, wo_ref, h_ref, o_ref):
    d = h_ref.shape[1]
    acc = None
    for k, y_ref in enumerate((ya_ref, yb_ref, yc_ref)):
        zk = jnp.dot(y_ref[...], wb_ref[k], preferred_element_type=F32)
        gk = _sigmoid(gates_ref[:, k * d:(k + 1) * d].astype(F32) + gb_ref[:, k * d:(k + 1) * d])
        acc = gk * zk if acc is None else acc + gk * zk
    o_ref[...] = h_ref[...] + jnp.dot(acc.astype(BF16), wo_ref[...], preferred_element_type=F32)


def _merge(ya, yb, yc, proj2, gate_b, w_branch, w_out, h, l, tm):
    m, d = h.shape
    nb = w_branch.shape[1]
    return pl.pallas_call(
        _merge_kernel,
        grid=(m // tm,),
        in_specs=[
            pl.BlockSpec((tm, d), lambda i: (i, 0)),
            pl.BlockSpec((tm, d), lambda i: (i, 0)),
            pl.BlockSpec((tm, d), lambda i: (i, 0)),
            pl.BlockSpec((tm, nb * d), lambda i: (i, _OFF_GATES // (nb * d))),
            _layer_spec(gate_b, l),
            _layer_spec(w_branch, l),
            _layer_spec(w_out, l),
            pl.BlockSpec((tm, d), lambda i: (i, 0)),
        ],
        out_specs=pl.BlockSpec((tm, d), lambda i: (i, 0)),
        out_shape=jax.ShapeDtypeStruct((m, d), F32),
        compiler_params=pltpu.CompilerParams(
            dimension_semantics=("parallel",), vmem_limit_bytes=VMEM_LIMIT),
        name="merge",
    )(ya, yb, yc, proj2, gate_b, w_branch, w_out, h)


def _ffn_kernel(h_ref, g_ref, wg_ref, wv_ref, cwg_ref, cwv_ref, cbg_ref, cbv_ref, wd_ref, gout_ref, o_ref,
                gpad_ref, vpad_ref, *, tm, norm_out):
    @pl.when(pl.program_id(1) == 0)
    def _():
        gpad_ref[0:8, :] = jnp.zeros((8, gpad_ref.shape[1]), F32)
        vpad_ref[0:8, :] = jnp.zeros((8, vpad_ref.shape[1]), F32)

    x = h_ref[0]
    ms = jnp.mean(x * x, axis=-1, keepdims=True)
    xn = (x * lax.rsqrt(ms + EPS) * g_ref[...]).astype(BF16)
    gpad_ref[8:8 + tm, :] = jnp.dot(xn, wg_ref[...], preferred_element_type=F32)
    vpad_ref[8:8 + tm, :] = jnp.dot(xn, wv_ref[...], preferred_element_type=F32)
    gate = cbg_ref[...]
    val = cbv_ref[...]
    for k in range(FFN_CONV):
        s = FFN_CONV - 1 - k
        gate = gate + cwg_ref[k:k + 1, :] * gpad_ref[8 - s:8 - s + tm, :]
        val = val + cwv_ref[k:k + 1, :] * vpad_ref[8 - s:8 - s + tm, :]
    gpad_ref[0:8, :] = gpad_ref[tm:tm + 8, :]
    vpad_ref[0:8, :] = vpad_ref[tm:tm + 8, :]
    act = (_silu(gate) * val).astype(BF16)
    r = x + jnp.dot(act, wd_ref[...], preferred_element_type=F32)
    if norm_out:
        ms = jnp.mean(r * r, axis=-1, keepdims=True)
        r = r * lax.rsqrt(ms + EPS) * gout_ref[...]
    o_ref[0] = r


def _ffn(h3, g, w_up, cw, cb, w_down, g_out, l, tm, norm_out):
    bsz, t, d = h3.shape
    dff = w_down.shape[1]
    resident = dict(pipeline_mode=pl.Buffered(1))
    return pl.pallas_call(
        functools.partial(_ffn_kernel, tm=tm, norm_out=norm_out),
        grid=(bsz, t // tm),
        in_specs=[
            pl.BlockSpec((1, tm, d), lambda b, i: (b, i, 0)),
            _layer_spec(g, l),
            pl.BlockSpec((None, d, dff), lambda b, i: (l, 0, 0), **resident),
            pl.BlockSpec((None, d, dff), lambda b, i: (l, 0, 1), **resident),
            pl.BlockSpec((None, FFN_CONV, dff), lambda b, i: (l, 0, 0)),
            pl.BlockSpec((None, FFN_CONV, dff), lambda b, i: (l, 0, 1)),
            pl.BlockSpec((None, 1, dff), lambda b, i: (l, 0, 0)),
            pl.BlockSpec((None, 1, dff), lambda b, i: (l, 0, 1)),
            pl.BlockSpec((None, dff, d), lambda b, i: (l, 0, 0), **resident),
            pl.BlockSpec((1, d), lambda b, i: (0, 0)),
        ],
        out_specs=pl.BlockSpec((1, tm, d), lambda b, i: (b, i, 0)),
        out_shape=jax.ShapeDtypeStruct((bsz, t, d), F32),
        scratch_shapes=[
            pltpu.VMEM((tm + 8, dff), F32),
            pltpu.VMEM((tm + 8, dff), F32),
        ],
        compiler_params=pltpu.CompilerParams(
            dimension_semantics=("parallel", "arbitrary"), vmem_limit_bytes=VMEM_LIMIT),
        name="conv_ffn",
    )(h3, g, w_up, w_up, cw, cw, cb, cb, w_down, g_out)


_RL_TILE = 512


def _relayout_kernel(src_ref, ga_ref, mg_ref, dt_ref, o_ref, *, n_copy):
    k = pl.program_id(1)

    @pl.when(k < n_copy)
    def _():
        o_ref[...] = src_ref[0].astype(BF16)

    @pl.when(k == n_copy)
    def _():
        rows = ga_ref.shape[1] + mg_ref.shape[1] + dt_ref.shape[1]
        sm = jnp.concatenate([ga_ref[0], mg_ref[0], dt_ref[0],
                              jnp.zeros((SM_W - rows, o_ref.shape[1]), F32)], axis=0)
        o_ref[0:SM_W, :] = sm.astype(BF16)
        o_ref[SM_W:, :] = jnp.zeros((o_ref.shape[0] - SM_W, o_ref.shape[1]), BF16)


def _relayout(w_t, segments, small_srcs):
    depth, _, d = w_t.shape
    n_copy = _OFF_SMALL // _RL_TILE
    assert all(ln % _RL_TILE == 0 and dst % _RL_TILE == 0 and src % 8 == 0 for dst, src, ln in segments)
    assert sorted(dst for dst, _, _ in segments)[0] == 0 and sum(ln for _, _, ln in segments) == _OFF_SMALL

    def src_row(l, k):
        r = k * _RL_TILE
        row = jnp.zeros_like(r)
        for dst, src, ln in segments:
            row = jnp.where((r >= dst) & (r < dst + ln), r + (src - dst), row)
        return (l, pl.multiple_of(row, 8), 0)

    elem = lambda rows, imap: pl.BlockSpec((pl.Element(1), pl.Element(rows), pl.Element(d)), imap)
    small = [elem(n, lambda l, k, s=src: (l, s, 0)) for src, n in small_srcs]
    return pl.pallas_call(
        functools.partial(_relayout_kernel, n_copy=n_copy),
        grid=(depth, n_copy + 1),
        in_specs=[elem(_RL_TILE, src_row)] + small,
        out_specs=pl.BlockSpec((None, _RL_TILE, d), lambda l, k: (l, k, 0)),
        out_shape=jax.ShapeDtypeStruct((depth, _N_PROJ, d), BF16),
        compiler_params=pltpu.CompilerParams(dimension_semantics=("parallel", "arbitrary")),
        name="w_relayout",
    )(w_t, w_t, w_t, w_t)


_D = 1024
_OFF_MLSTM = 3 * _D
_OFF_GATES = 6 * _D
_OFF_SZ = 9 * _D
_OFF_SX = 10 * _D
_OFF_SBC = 11 * _D
_OFF_SMALL = 11 * _D + _D // 2
_N_PROJ = 12 * _D


def _pick(n, cands):
    for c in cands:
        if n % c == 0:
            return c
    return n


def kernel(x, norm_mix, w_in, gla_wa, gla_ba, gla_norm, mlstm_conv_w, mlstm_conv_b, mlstm_bi, mlstm_bf,
           mlstm_norm, ssd_conv_w, ssd_conv_b, ssd_dt_bias, ssd_a_log, ssd_d, ssd_norm, gate_b, w_branch,
           w_out, norm_ffn, w_up, ffn_conv_w, ffn_conv_b, w_down, norm_final):
    bsz, t, d = x.shape
    assert d == _D
    depth = w_in.shape[0]
    m = bsz * t
    qk = d // 2
    rank = gla_wa.shape[1]
    bc_w = 2 * SSD_GROUPS * SSD_STATE
    sizes = (qk, qk, d, rank, d, 2 * qk, d, MLSTM_HEADS, MLSTM_HEADS, d, d, d + bc_w, SSD_HEADS, 3 * d)
    offs = [0]
    for s in sizes:
        offs.append(offs[-1] + s)
    (o_gq, o_gk, o_gv, o_ga, o_gg, o_mqk, o_mv, o_mi, o_mf, o_mo, o_sz, o_sxbc, o_sdt, o_gates, o_end) = offs
    assert o_end == w_in.shape[2]

    segments = ((0, o_gq, o_ga - o_gq), (o_ga - o_gq, o_gg, o_mi - o_gg), (_OFF_MLSTM + 2 * d, o_mo, o_sz - o_mo),
                (_OFF_GATES, o_gates, o_end - o_gates), (_OFF_SZ, o_sz, o_sdt - o_sz))
    assert (SM_GA, SM_MI, SM_DT) == (0, rank, rank + 2 * MLSTM_HEADS)
    w_perm = _relayout(jnp.swapaxes(w_in, 1, 2), segments,
                       ((o_ga, rank), (o_mi, 2 * MLSTM_HEADS), (o_sdt, SSD_HEADS)))

    row3 = lambda a: a[:, None, :]
    col3 = lambda a: a[:, :, None]
    pad_small = lambda a, off: jnp.pad(a, ((0, 0), (off, SM_W - off - a.shape[1])))
    wa_pad = jnp.pad(gla_wa, ((0, 0), (SM_GA, SM_W - SM_GA - rank), (0, 0)))
    wa_hi = wa_pad.astype(BF16)
    wa_lo = (wa_pad - wa_hi.astype(F32)).astype(BF16)
    wa3 = jnp.concatenate([wa_hi, wa_hi, wa_lo], axis=1)
    m_bias = pad_small(mlstm_bi, SM_MI) + pad_small(mlstm_bf, SM_MF)
    s_bias = pad_small(ssd_dt_bias, SM_DT)
    alog = pad_small(ssd_a_log, SM_DT)
    dexp = jnp.repeat(ssd_d, SSD_HEAD_DIM, axis=-1)
    head_of_col = jnp.arange(d) // SSD_HEAD_DIM
    emat1 = (jnp.arange(SM_W)[:, None] == (SM_DT + head_of_col)[None, :]).astype(BF16)
    emat = jnp.concatenate([emat1, emat1], axis=0)
    w_branch_b = w_branch.astype(BF16)
    w_out_b = w_out.astype(BF16)
    w_up_b = w_up.astype(BF16)
    w_down_b = w_down.astype(BF16)
    norm_mix3, norm_ffn3 = row3(norm_mix), row3(norm_ffn)
    gla_ba3, gla_norm3 = row3(gla_ba), row3(gla_norm)
    m_cb3, m_norm3 = row3(mlstm_conv_b), row3(mlstm_norm)
    s_cwx, s_cwbc = ssd_conv_w[:, :, :d], ssd_conv_w[:, :, d:]
    s_cbx, s_cbbc = row3(ssd_conv_b[:, :d]), row3(ssd_conv_b[:, d:])
    s_norm3, dexp3, gate_b3, f_cb3 = row3(ssd_norm), row3(dexp), row3(gate_b), row3(ffn_conv_b)

    tm_proj = _pick(m, (1024, 512, 256, 128, 64))
    tn_proj = _pick(_N_PROJ, (3072, 2048, 1024))
    tb = _pick(t, (512, 256, 128, 64))
    tm_merge = _pick(m, (512, 256, 128, 64))
    tm_ffn = _pick(t, (512, 256, 128, 64))

    h = x.reshape(m, d)
    for l in range(depth):
        proj2, small2 = _proj(h, norm_mix3, w_perm, l, tm_proj, tn_proj)
        proj3 = proj2.reshape(bsz, t, _N_PROJ)
        small3 = small2.reshape(bsz, t, SM_W)
        y_gla = _gla(proj3, small3, wa3, gla_ba3, gla_norm3, l, tb)
        y_m = _mlstm(proj3, small3, mlstm_conv_w, m_cb3, row3(m_bias), col3(m_bias), m_norm3, l, tb)
        y_s = _ssd(proj3, small3, s_cwx, s_cbx, s_cwbc, s_cbbc, row3(s_bias), col3(s_bias), row3(alog),
                   col3(alog), dexp3, s_norm3, emat, l, tb)
        h = _merge(y_gla.reshape(m, d), y_m.reshape(m, d), y_s.reshape(m, d), proj2, gate_b3,
                   w_branch_b, w_out_b, h, l, tm_merge)
        h = _ffn(h.reshape(bsz, t, d), norm_ffn3, w_up_b, ffn_conv_w, f_cb3, w_down_b, norm_final[None, :], l,
                 tm_ffn, norm_out=(l == depth - 1)).reshape(m, d)
    return h.reshape(bsz, t, d)
```
